```python
import math
import jax
import jax.numpy as jnp
from jax import lax
import numpy as np

D_MODEL = 1024
BATCH = 4
SEQ = 4096
DEPTH = 4
DEC_BATCH = 32
DEC_SEQ = 8
PAST_LEN = 8192
PAGE_SIZE = 128

N_MIXERS = 3
A_HEADS = 8
A_DK = 128
A_DV = D_MODEL // A_HEADS
A_FDIM = A_HEADS * A_DK
B_HEADS = 8
B_DK = 128
B_DV = 128
B_QKV = B_HEADS * (2 * B_DK + B_DV)
CONV_W = 4
C_HEADS = 8
C_DH = D_MODEL // C_HEADS
MOBA_BLOCK = 256
MOBA_TOPK = 3
MOBA_QBLOCK = 32
REL_BUCKETS = 32
REL_MAX_DIST = 128
D_FF = 2752
CHUNK = 64
N_A = len(range(0, DEPTH, N_MIXERS))
N_B = len(range(1, DEPTH, N_MIXERS))
N_C = len(range(2, DEPTH, N_MIXERS))
DN_ALPHA = (2 * DEPTH) ** 0.25
DN_BETA = (8 * DEPTH) ** -0.25
LN_EPS = 1e-5
NEG = -1e30

kernel_name = "hybrid_hgrn2_gdn_moba_macaron_step"

F32 = jnp.float32


def _layer_norm(x, g, b):
    xf = x.astype(F32)
    mu = jnp.mean(xf, -1, keepdims=True)
    var = jnp.mean(jnp.square(xf - mu), -1, keepdims=True)
    y = (xf - mu) * lax.rsqrt(var + LN_EPS) * g.astype(F32) + b.astype(F32)
    return y.astype(x.dtype)


def _rms_norm(x, g):
    xf = x.astype(F32)
    return xf * lax.rsqrt(jnp.mean(jnp.square(xf), -1, keepdims=True) + 1e-6) * g.astype(F32)


def _l2norm(x):
    return x * lax.rsqrt(jnp.sum(jnp.square(x), -1, keepdims=True) + 1e-6)


def _swiglu(h, w_up, w_down):
    a, u = jnp.split(h @ w_up, 2, axis=-1)
    return (jax.nn.silu(a) * u) @ w_down


def _pad_seq(t, pad):
    return jnp.pad(t, [(0, 0), (0, pad)] + [(0, 0)] * (t.ndim - 2))


def _to_chunks(t, c):
    b, l, h = t.shape[:3]
    return t.reshape(b, l // c, c, h, -1).transpose(1, 0, 3, 2, 4)


def _gla_chunked(q, k, v, logf, s0):
    b_, l, h, _ = q.shape
    c = min(CHUNK, l)
    pad = (-l) % c
    q, k, v, logf = (_pad_seq(t, pad) for t in (q, k, v, logf))
    qc, kc, vc, gc = (_to_chunks(t, c) for t in (q, k, v, logf))
    causal = jnp.tril(jnp.ones((c, c), bool))[:, :, None]

    def step(s, inp):
        qi, ki, vi, gi = inp
        cum = jnp.cumsum(gi, axis=2)
        diff = cum[:, :, :, None, :] - cum[:, :, None, :, :]
        decay = jnp.where(causal, jnp.exp(jnp.where(causal, diff, 0.0)), 0.0)
        att = jnp.einsum('bhtk,bhsk,bhtsk->bhts', qi, ki, decay)
        o = jnp.einsum('bhtk,bhkv->bhtv', qi * jnp.exp(cum), s) + jnp.einsum('bhts,bhsv->bhtv', att, vi)
        last = cum[:, :, -1:, :]
        s = s * jnp.exp(last[:, :, 0, :, None]) + jnp.einsum('bhsk,bhsv->bhkv', ki * jnp.exp(last - cum), vi)
        return s, o

    s, o = lax.scan(step, s0, (qc, kc, vc, gc))
    o = o.transpose(1, 0, 3, 2, 4).reshape(b_, -1, h, o.shape[-1])[:, :l]
    return o, s


def _gated_delta_chunked(q, k, v, beta, g, s0):
    b_, l, h, _ = q.shape
    c = min(CHUNK, l)
    pad = (-l) % c
    q, k, v, beta, g = (_pad_seq(t, pad) for t in (q, k, v, beta, g))
    qc, kc, vc = (_to_chunks(t, c) for t in (q, k, v))
    bc, gc = (_to_chunks(t[..., None], c)[..., 0] for t in (beta, g))
    causal = jnp.tril(jnp.ones((c, c), bool))
    eye = jnp.eye(c, dtype=F32)

    def step(s, inp):
        qi, ki, vi, bi, gi = inp
        cum = jnp.cumsum(gi, axis=-1)
        diff = cum[..., :, None] - cum[..., None, :]
        decay = jnp.where(causal, jnp.exp(jnp.where(causal, diff, 0.0)), 0.0)
        kb = ki * bi[..., None]
        m = jnp.einsum('bhtk,bhsk->bhts', kb, ki) * decay * (1.0 - eye)
        a = m + eye
        t_inv = lax.linalg.triangular_solve(a, jnp.broadcast_to(eye, a.shape), left_side=True, lower=True)
        u = t_inv @ (vi * bi[..., None])
        w = t_inv @ (kb * jnp.exp(cum)[..., None])
        v_new = u - w @ s
        att = jnp.einsum('bhtk,bhsk->bhts', qi, ki) * decay
        o = (qi * jnp.exp(cum)[..., None]) @ s + att @ v_new
        last = cum[..., -1:]
        s = s * jnp.exp(last)[..., None] + jnp.einsum('bhsk,bhsv->bhkv', ki * jnp.exp(last - cum)[..., None], v_new)
        return s, o

    s, o = lax.scan(step, s0, (qc, kc, vc, bc, gc))
    o = o.transpose(1, 0, 3, 2, 4).reshape(b_, -1, h, o.shape[-1])[:, :l]
    return o, s


def _hgrn2_mixer(h, s0, lb, w_in, norm_g, w_o):
    bn, l, _ = h.shape
    q, f, i, g = jnp.split(h @ w_in, [A_FDIM, 2 * A_FDIM, 2 * A_FDIM + D_MODEL], axis=-1)
    f = f.astype(F32)
    logf = jnp.log(lb + (1.0 - lb) * jax.nn.sigmoid(f))
    k = (1.0 - lb) * jax.nn.sigmoid(-f)
    shp = (bn, l, A_HEADS, A_DK)
    q = jax.nn.silu(q.astype(F32)).reshape(shp) * A_DK ** -0.5
    v = i.astype(F32).reshape(bn, l, A_HEADS, A_DV)
    o, s = _gla_chunked(q, k.reshape(shp), v, logf.reshape(shp), s0.astype(F32))
    o = _rms_norm(o, norm_g) * jax.nn.sigmoid(g.astype(F32)).reshape(bn, l, A_HEADS, A_DV)
    return o.reshape(bn, l, D_MODEL).astype(h.dtype) @ w_o, s


def _gdn_mixer(h, conv0, s0, w_in, conv_w, a_log, dt_bias, norm_g, w_o):
    bn, l, _ = h.shape
    hh = B_HEADS
    qkv, z, beta, a = jnp.split(h @ w_in, [B_QKV, B_QKV + hh * B_DV, B_QKV + hh * B_DV + hh], axis=-1)
    xc = jnp.concatenate([conv0.astype(qkv.dtype), qkv], axis=1)
    conv = sum(xc[:, w:w + l] * conv_w[w] for w in range(CONV_W))
    new_conv = xc[:, l:]
    conv = jax.nn.silu(conv.astype(F32))
    q, k, v = jnp.split(conv, [hh * B_DK, 2 * hh * B_DK], axis=-1)
    q = _l2norm(q.reshape(bn, l, hh, B_DK)) * B_DK ** -0.5
    k = _l2norm(k.reshape(bn, l, hh, B_DK))
    v = v.reshape(bn, l, hh, B_DV)
    beta = jax.nn.sigmoid(beta.astype(F32))
    g = -jnp.exp(a_log.astype(F32)) * jax.nn.softplus(a.astype(F32) + dt_bias.astype(F32))
    o, s = _gated_delta_chunked(q, k, v, beta, g, s0.astype(F32))
    o = _rms_norm(o, norm_g) * jax.nn.silu(z.astype(F32)).reshape(bn, l, hh, B_DV)
    return o.reshape(bn, l, hh * B_DV).astype(h.dtype) @ w_o, new_conv, s


def _rel_bucket(dist):
    exact = REL_BUCKETS // 2
    d = jnp.maximum(dist, exact).astype(F32)
    large = exact + (jnp.log(d / exact) / math.log(REL_MAX_DIST / exact) * (REL_BUCKETS - exact)).astype(jnp.int32)
    return jnp.where(dist < exact, dist, jnp.minimum(large, REL_BUCKETS - 1))


def _moba_attend(q, k_all, v_all, pos0, rel_bias):
    bq, lq, h, dh = q.shape
    nb = k_all.shape[1] // MOBA_BLOCK
    kb = k_all.reshape(bq, nb, MOBA_BLOCK, h, dh)
    vb = v_all.reshape(bq, nb, MOBA_BLOCK, h, dh)
    means = jnp.mean(kb.astype(F32), axis=2)
    qb = math.gcd(lq, MOBA_QBLOCK)
    nq = lq // qb
    topk = min(MOBA_TOPK, nb)
    q_blocks = q.reshape(bq * nq, qb, h, dh)
    b_ids = jnp.repeat(jnp.arange(bq, dtype=jnp.int32), nq)
    starts = pos0 + jnp.tile(jnp.arange(nq, dtype=jnp.int32) * qb, bq)
    t_off = jnp.arange(MOBA_BLOCK, dtype=jnp.int32)
    h_ix = jnp.arange(h)[None, :, None, None]
    blk_ix = jnp.arange(nb, dtype=jnp.int32)
    rank = jnp.arange(topk, dtype=jnp.int32)

    def one_block(args):
        qc, b, start = args
        pos = start + jnp.arange(qb, dtype=jnp.int32)
        cur = pos // MOBA_BLOCK
        gate = jnp.einsum('qhd,nhd->qhn', qc.astype(F32), means[b])
        gate = jnp.where(blk_ix[None, None, :] < cur[:, None, None], gate, NEG)
        _, sel = lax.top_k(gate, topk)
        idx = jnp.concatenate([sel, jnp.broadcast_to(cur[:, None, None], (qb, h, 1))], -1)
        ok = jnp.concatenate([jnp.broadcast_to(rank[None, None, :] < cur[:, None, None], (qb, h, topk)),
                              jnp.ones((qb, h, 1), bool)], -1)
        kg = kb[b, idx[..., None], t_off, h_ix]
        vg = vb[b, idx[..., None], t_off, h_ix]
        dist = pos[:, None, None, None] - (idx[..., None] * MOBA_BLOCK + t_off)
        bias = rel_bias[_rel_bucket(jnp.maximum(dist, 0)), h_ix].astype(F32)
        s = jnp.einsum('qhd,qhnkd->qhnk', qc, kg).astype(F32) + bias
        s = jnp.where(ok[..., None] & (dist >= 0), s, NEG)
        p = jax.nn.softmax(s.reshape(qb, h, -1), axis=-1)
        return jnp.einsum('qhm,qhmd->qhd', p.astype(vg.dtype), vg.reshape(qb, h, -1, dh))

    o = lax.map(one_block, (q_blocks, b_ids, starts))
    return o.reshape(bq, lq, h, dh)


def _moba_mixer(h, past_k, past_v, pos0, w_qkv, w_o, rel_bias):
    bn, l, _ = h.shape
    q, k, v = (t.reshape(bn, l, C_HEADS, C_DH) for t in jnp.split(h @ w_qkv, 3, axis=-1))
    if past_k is None:
        k_all, v_all = k, v
    else:
        k_all = jnp.concatenate([past_k.astype(k.dtype), k], axis=1)
        v_all = jnp.concatenate([past_v.astype(v.dtype), v], axis=1)
    pad = (-k_all.shape[1]) % MOBA_BLOCK
    k_all, v_all = _pad_seq(k_all, pad), _pad_seq(v_all, pad)
    o = _moba_attend(q * C_DH ** -0.5, k_all, v_all, pos0, rel_bias)
    return o.reshape(bn, l, D_MODEL) @ w_o, k, v


def _modulate(x, m, j):
    return x * (1 + m[:, 3 * j + 1, None, :]) + m[:, 3 * j, None, :]


def _post(x, y, m, j, g, b):
    return _layer_norm(DN_ALPHA * x + (1 + m[:, 3 * j + 2, None, :]) * y, g, b)


def _trunk(x, c, pos0, s_hgrn, s_conv, s_gdn, paged, P):
    bn = x.shape[0]
    mods = jnp.einsum('bd,lde->lbe', jax.nn.silu(c), P['w_ada']) + P['b_ada'][:, None, :]
    lbp = jax.nn.softmax(P['hgrn_lb_logits'].astype(F32), axis=0)
    lbs = jnp.cumsum(lbp, axis=0) - lbp[0]
    new_h, new_c, new_g, new_k, new_v = [], [], [], [], []
    for i in range(DEPTH):
        m = mods[i].reshape(bn, 9, D_MODEL)
        x = _post(x, 0.5 * _swiglu(_modulate(x, m, 0), P['w_ffn1_up'][i], P['w_ffn1_down'][i]),
                  m, 0, P['ln_g'][i, 0], P['ln_b'][i, 0])
        hm = _modulate(x, m, 1)
        kind, li = i % N_MIXERS, i // N_MIXERS
        if kind == 0:
            y, s = _hgrn2_mixer(hm, s_hgrn[li], lbs[li], P['hgrn_w_in'][li], P['hgrn_norm_g'][li], P['hgrn_w_o'][li])
            new_h.append(s)
        elif kind == 1:
            y, cb, s = _gdn_mixer(hm, s_conv[li], s_gdn[li], P['gdn_w_in'][li], P['gdn_conv_w'][li],
                                  P['gdn_a_log'][li], P['gdn_dt_bias'][li], P['gdn_norm_g'][li], P['gdn_w_o'][li])
            new_c.append(cb)
            new_g.append(s)
        else:
            if paged is None:
                pk = pv = None
            else:
                ck, cv, pt = paged
                pk = ck[li, pt].reshape(pt.shape[0], -1, C_HEADS, C_DH)
                pv = cv[li, pt].reshape(pt.shape[0], -1, C_HEADS, C_DH)
            y, k, v = _moba_mixer(hm, pk, pv, pos0, P['moba_w_qkv'][li], P['moba_w_o'][li], P['rel_bias'])
            new_k.append(k)
            new_v.append(v)
        x = _post(x, y, m, 1, P['ln_g'][i, 1], P['ln_b'][i, 1])
        x = _post(x, 0.5 * _swiglu(_modulate(x, m, 2), P['w_ffn2_up'][i], P['w_ffn2_down'][i]),
                  m, 2, P['ln_g'][i, 2], P['ln_b'][i, 2])
    return x, jnp.stack(new_h), jnp.stack(new_c), jnp.stack(new_g), jnp.stack(new_k), jnp.stack(new_v)


def setup_inputs(seed: int = 0) -> dict:
    key = jax.random.key(seed)
    ks = jax.random.split(key, 40)
    D = D_MODEL
    n_pages = PAST_LEN // PAGE_SIZE
    n_pool = (DEC_BATCH * n_pages * 5) // 4

    def nrm(k, shape, s):
        return jax.random.normal(k, shape, F32) * s

    page_table = jax.random.permutation(ks[9], n_pool)[:DEC_BATCH * n_pages].reshape(DEC_BATCH, n_pages).astype(jnp.int32)
    dt = jnp.exp(jax.random.uniform(ks[24], (N_B, B_HEADS), F32, math.log(1e-3), math.log(1e-1)))
    return {
        'x_prompt': nrm(ks[0], (BATCH, SEQ, D), 1.0),
        'x_sample': nrm(ks[1], (DEC_BATCH, DEC_SEQ, D), 1.0),
        'state_hgrn': nrm(ks[2], (N_A, DEC_BATCH, A_HEADS, A_DK, A_DV), 0.5),
        'state_gdn_conv': nrm(ks[3], (N_B, DEC_BATCH, CONV_W - 1, B_QKV), 1.0),
        'state_gdn': nrm(ks[4], (N_B, DEC_BATCH, B_HEADS, B_DK, B_DV), 0.5),
        'cache_k': nrm(ks[5], (N_C, n_pool, PAGE_SIZE, C_HEADS, C_DH), 1.0),
        'cache_v': nrm(ks[6], (N_C, n_pool, PAGE_SIZE, C_HEADS, C_DH), 1.0),
        'page_table': page_table,
        'c_prompt': nrm(ks[7], (BATCH, D), 1.0),
        'c_sample': nrm(ks[8], (DEC_BATCH, D), 1.0),
        'w_ada': nrm(ks[10], (DEPTH, D, 9 * D), 0.1 * D ** -0.5),
        'b_ada': nrm(ks[11], (DEPTH, 9 * D), 0.01),
        'ln_g': 1.0 + nrm(ks[12], (DEPTH, 3, D), 0.02),
        'ln_b': nrm(ks[13], (DEPTH, 3, D), 0.02),
        'w_ffn1_up': nrm(ks[14], (DEPTH, D, 2 * D_FF), D ** -0.5),
        'w_ffn1_down': nrm(ks[15], (DEPTH, D_FF, D), DN_BETA * D_FF ** -0.5),
        'w_ffn2_up': nrm(ks[16], (DEPTH, D, 2 * D_FF), D ** -0.5),
        'w_ffn2_down': nrm(ks[17], (DEPTH, D_FF, D), DN_BETA * D_FF ** -0.5),
        'hgrn_w_in': nrm(ks[18], (N_A, D, 2 * A_FDIM + 2 * D), D ** -0.5),
        'hgrn_lb_logits': nrm(ks[19], (N_A, A_FDIM), 0.5),
        'hgrn_norm_g': 1.0 + nrm(ks[20], (N_A, A_DV), 0.02),
        'hgrn_w_o': nrm(ks[21], (N_A, D, D), DN_BETA * D ** -0.5),
        'gdn_w_in': nrm(ks[22], (N_B, D, B_QKV + B_HEADS * B_DV + 2 * B_HEADS), D ** -0.5),
        'gdn_conv_w': nrm(ks[23], (N_B, CONV_W, B_QKV), CONV_W ** -0.5),
        'gdn_a_log': jnp.log(jax.random.uniform(ks[25], (N_B, B_HEADS), F32, 1.0, 16.0)),
        'gdn_dt_bias': dt + jnp.log(-jnp.expm1(-dt)),
        'gdn_norm_g': 1.0 + nrm(ks[26], (N_B, B_DV), 0.02),
        'gdn_w_o': nrm(ks[27], (N_B, B_HEADS * B_DV, D), DN_BETA * (B_HEADS * B_DV) ** -0.5),
        'moba_w_qkv': nrm(ks[28], (N_C, D, 3 * C_HEADS * C_DH), D ** -0.5),
        'moba_w_o': nrm(ks[29], (N_C, C_HEADS * C_DH, D), DN_BETA * (C_HEADS * C_DH) ** -0.5),
        'rel_bias': nrm(ks[30], (REL_BUCKETS, C_HEADS), 0.5),
    }


def reference(x_prompt, x_sample, state_hgrn, state_gdn_conv, state_gdn, cache_k, cache_v, page_table,
              c_prompt, c_sample, w_ada, b_ada, ln_g, ln_b, w_ffn1_up, w_ffn1_down, w_ffn2_up, w_ffn2_down,
              hgrn_w_in, hgrn_lb_logits, hgrn_norm_g, hgrn_w_o, gdn_w_in, gdn_conv_w, gdn_a_log, gdn_dt_bias,
              gdn_norm_g, gdn_w_o, moba_w_qkv, moba_w_o, rel_bias):
    P = {
        'w_ada': w_ada, 'b_ada': b_ada, 'ln_g': ln_g, 'ln_b': ln_b,
        'w_ffn1_up': w_ffn1_up, 'w_ffn1_down': w_ffn1_down, 'w_ffn2_up': w_ffn2_up, 'w_ffn2_down': w_ffn2_down,
        'hgrn_w_in': hgrn_w_in, 'hgrn_lb_logits': hgrn_lb_logits, 'hgrn_norm_g': hgrn_norm_g, 'hgrn_w_o': hgrn_w_o,
        'gdn_w_in': gdn_w_in, 'gdn_conv_w': gdn_conv_w, 'gdn_a_log': gdn_a_log, 'gdn_dt_bias': gdn_dt_bias,
        'gdn_norm_g': gdn_norm_g, 'gdn_w_o': gdn_w_o, 'moba_w_qkv': moba_w_qkv, 'moba_w_o': moba_w_o,
        'rel_bias': rel_bias,
    }
    bp = x_prompt.shape[0]
    z_hgrn = jnp.zeros((N_A, bp, A_HEADS, A_DK, A_DV), F32)
    z_conv = jnp.zeros((N_B, bp, CONV_W - 1, B_QKV), x_prompt.dtype)
    z_gdn = jnp.zeros((N_B, bp, B_HEADS, B_DK, B_DV), F32)
    y_prompt, hgrn_p, conv_p, gdn_p, k_p, v_p = _trunk(x_prompt, c_prompt, 0, z_hgrn, z_conv, z_gdn, None, P)
    y_sample, hgrn_s, conv_s, gdn_s, k_s, v_s = _trunk(x_sample, c_sample, PAST_LEN, state_hgrn, state_gdn_conv,
                                                       state_gdn, (cache_k, cache_v, page_table), P)
    return (y_prompt, y_sample, hgrn_p, hgrn_s, conv_p, conv_s, gdn_p, gdn_s, k_p, k_s, v_p, v_s)
```

```python
import functools
import math

import jax
import jax.numpy as jnp
from jax import lax
from jax.experimental import pallas as pl
from jax.experimental.pallas import tpu as pltpu

F32 = jnp.float32
BF16 = jnp.bfloat16
HI = lax.Precision.HIGHEST

DEPTH = 4
N_MIXERS = 3
HEADS = 8
HEAD_DIM = 128
CONV_W = 4
CHUNK = 64
MOBA_BLOCK = 256
MOBA_TOPK = 3
REL_BUCKETS = 32
REL_MAX_DIST = 128
PAGE_SIZE = 128
DN_ALPHA = (2 * DEPTH) ** 0.25
LN_EPS = 1e-5
NEG = -1e30

LANES = 128
SUBLANES = 8
MXU_WIDTH = 256
VMEM_LIMIT_BYTES = 56 * 1024 * 1024


def _cparams(*sem):
    return pltpu.CompilerParams(dimension_semantics=sem, vmem_limit_bytes=VMEM_LIMIT_BYTES)


def _round_up(n, m):
    return (n + m - 1) // m * m


def _bdot(a, b):
    return jnp.dot(a.astype(BF16), b.astype(BF16), preferred_element_type=F32)


def _bdot_nt(a, b):
    return lax.dot_general(a.astype(BF16), b.astype(BF16), (((1,), (1,)), ((), ())),
                           preferred_element_type=F32)


def _bdot_tn(a, b):
    return lax.dot_general(a.astype(BF16), b.astype(BF16), (((0,), (0,)), ((), ())),
                           preferred_element_type=F32)


def _fdot(a, b):
    return jnp.dot(a, b, preferred_element_type=F32, precision=HI)


def _sigmoid(x):
    return 1.0 / (1.0 + jnp.exp(-x))


def _silu(x):
    return x * _sigmoid(x)


def _layer_norm_rows(z, g, b):
    mu = jnp.mean(z, axis=-1, keepdims=True)
    zc = z - mu
    var = jnp.mean(zc * zc, axis=-1, keepdims=True)
    return zc * lax.rsqrt(var + LN_EPS) * g + b


def _ada_kernel(c_ref, w_ref, b_ref, o_ref):
    o_ref[...] = _bdot(_silu(c_ref[...]), w_ref[...]) + b_ref[...]


def _ada_mods(c_all, w_ada, b_ada):
    depth, d, n = w_ada.shape
    nb = c_all.shape[0]
    tn = n // 8
    assert n % 8 == 0 and tn % LANES == 0
    return pl.pallas_call(
        _ada_kernel,
        grid=(depth, n // tn),
        in_specs=[pl.BlockSpec((nb, d), lambda l, j: (0, 0)),
                  pl.BlockSpec((None, d, tn), lambda l, j: (l, 0, j)),
                  pl.BlockSpec((None, 1, tn), lambda l, j: (l, 0, j))],
        out_specs=pl.BlockSpec((None, nb, tn), lambda l, j: (l, 0, j)),
        out_shape=jax.ShapeDtypeStruct((depth, nb, n), F32),
        compiler_params=_cparams("parallel", "parallel"),
    )(c_all, w_ada, b_ada.reshape(depth, 1, n))


MAX_ROW_TILE = 1024


def _row_tile(batch, length):
    if length >= MXU_WIDTH:
        tm = MAX_ROW_TILE
        while length % tm:
            tm //= 2
        return tm, length // tm, True
    rows = batch * length
    assert rows % SUBLANES == 0 and rows <= MAX_ROW_TILE
    return rows, 1, False


def _mod_layout(m, batch, length):
    tm, _, per_batch = _row_tile(batch, length)
    mt = jnp.transpose(m, (1, 0, 2))
    if per_batch:
        return mt[:, :, None, :]
    return jnp.repeat(mt, length, axis=1)[:, None, :, :]


def _mod_spec(mod, tiles_per_block):
    _, _, rm, d = mod.shape
    return pl.BlockSpec((9, None, rm, d), lambda i, *_: (0, i // tiles_per_block, 0, 0))


FF_CHUNK = MXU_WIDTH


def _ffn_kernel(x_ref, mod_ref, wa_ref, wu_ref, wd_ref, g_ref, b_ref, o_ref, h_ref, *, j, n_chunks):
    x = x_ref[...]
    h_ref[...] = (x * (1.0 + mod_ref[3 * j + 1]) + mod_ref[3 * j]).astype(BF16)
    o_ref[...] = jnp.zeros_like(o_ref)

    def body(c, carry):
        c0 = pl.multiple_of(c * FF_CHUNK, FF_CHUNK)
        h = h_ref[...]
        a = jnp.dot(h, wa_ref[:, pl.ds(c0, FF_CHUNK)], preferred_element_type=F32)
        u = jnp.dot(h, wu_ref[:, pl.ds(c0, FF_CHUNK)], preferred_element_type=F32)
        act = (_silu(a) * u).astype(BF16)
        o_ref[...] += jnp.dot(act, wd_ref[pl.ds(c0, FF_CHUNK), :], preferred_element_type=F32)
        return carry

    lax.fori_loop(0, n_chunks, body, 0)
    z = DN_ALPHA * x + (1.0 + mod_ref[3 * j + 2]) * (0.5 * o_ref[...])
    o_ref[...] = _layer_norm_rows(z, g_ref[...], b_ref[...])


def _ffn_block(x2d, mod, tiling, j, w_up, w_down, layer, ln_g, ln_b):
    tm, tpb, _ = tiling
    rows, d = x2d.shape
    fp = w_up.shape[-1]
    kern = functools.partial(_ffn_kernel, j=j, n_chunks=fp // FF_CHUNK)
    return pl.pallas_call(
        kern,
        grid=(rows // tm,),
        in_specs=[pl.BlockSpec((tm, d), lambda i: (i, 0)),
                  _mod_spec(mod, tpb),
                  pl.BlockSpec((None, None, d, fp), lambda i: (layer, 0, 0, 0), pipeline_mode=pl.Buffered(1)),
                  pl.BlockSpec((None, None, d, fp), lambda i: (layer, 1, 0, 0), pipeline_mode=pl.Buffered(1)),
                  pl.BlockSpec((None, fp, d), lambda i: (layer, 0, 0), pipeline_mode=pl.Buffered(1)),
                  pl.BlockSpec((1, d), lambda i: (0, 0)),
                  pl.BlockSpec((1, d), lambda i: (0, 0))],
        out_specs=pl.BlockSpec((tm, d), lambda i: (i, 0)),
        out_shape=jax.ShapeDtypeStruct((rows, d), F32),
        scratch_shapes=[pltpu.VMEM((tm, d), BF16)],
        compiler_params=_cparams("parallel"),
    )(x2d, mod, w_up, w_up, w_down, ln_g.reshape(1, d), ln_b.reshape(1, d))


def _prep_ffn_weights(w_up, w_down):
    depth, d, two_ff = w_up.shape
    d_ff = two_ff // 2
    fp = _round_up(d_ff, FF_CHUNK)
    up = jnp.transpose(w_up.reshape(depth, d, 2, d_ff), (0, 2, 1, 3))
    up = jnp.pad(up, ((0, 0), (0, 0), (0, 0), (0, fp - d_ff))).astype(BF16)
    down = jnp.pad(w_down, ((0, 0), (0, fp - d_ff), (0, 0))).astype(BF16)
    return up, down


def _proj_in_kernel(x_ref, mod_ref, w_ref, o_ref, h_ref, *, j):
    @pl.when(pl.program_id(1) == 0)
    def _():
        h_ref[...] = (x_ref[...] * (1.0 + mod_ref[3 * j + 1]) + mod_ref[3 * j]).astype(BF16)

    o_ref[...] = jnp.dot(h_ref[...], w_ref[...], preferred_element_type=F32)


def _proj_in(x2d, mod, tiling, j, w, tn):
    tm, tpb, _ = tiling
    rows, d = x2d.shape
    n = w.shape[1]
    assert n % tn == 0
    return pl.pallas_call(
        functools.partial(_proj_in_kernel, j=j),
        grid=(rows // tm, n // tn),
        in_specs=[pl.BlockSpec((tm, d), lambda i, k: (i, 0)),
                  _mod_spec(mod, tpb),
                  pl.BlockSpec((d, tn), lambda i, k: (0, k))],
        out_specs=pl.BlockSpec((tm, tn), lambda i, k: (i, k)),
        out_shape=jax.ShapeDtypeStruct((rows, n), F32),
        scratch_shapes=[pltpu.VMEM((tm, d), BF16)],
        compiler_params=_cparams("parallel", "arbitrary"),
    )(x2d, mod, w)


def _proj_out_kernel(o_ref, x_ref, mod_ref, w_ref, g_ref, b_ref, y_ref, *, j):
    y = jnp.dot(o_ref[...].astype(BF16), w_ref[...], preferred_element_type=F32)
    z = DN_ALPHA * x_ref[...] + (1.0 + mod_ref[3 * j + 2]) * y
    y_ref[...] = _layer_norm_rows(z, g_ref[...], b_ref[...])


def _proj_out(o2d, x2d, mod, tiling, j, w, ln_g, ln_b):
    tm, tpb, _ = tiling
    rows, d = x2d.shape
    k = o2d.shape[1]
    return pl.pallas_call(
        functools.partial(_proj_out_kernel, j=j),
        grid=(rows // tm,),
        in_specs=[pl.BlockSpec((tm, k), lambda i: (i, 0)),
                  pl.BlockSpec((tm, d), lambda i: (i, 0)),
                  _mod_spec(mod, tpb),
                  pl.BlockSpec((k, d), lambda i: (0, 0)),
                  pl.BlockSpec((1, d), lambda i: (0, 0)),
                  pl.BlockSpec((1, d), lambda i: (0, 0))],
        out_specs=pl.BlockSpec((tm, d), lambda i: (i, 0)),
        out_shape=jax.ShapeDtypeStruct((rows, d), F32),
        compiler_params=_cparams("parallel"),
    )(o2d, x2d, mod, w, ln_g.reshape(1, d), ln_b.reshape(1, d))


def _tri(c, strict=False):
    r = lax.broadcasted_iota(jnp.int32, (c, c), 0)
    s = lax.broadcasted_iota(jnp.int32, (c, c), 1)
    return (r > s) if strict else (r >= s)


def _hgrn_chunk(qr, fr, v, lb, s_prev, c):
    nb = c // SUBLANES
    q = _silu(qr) * (HEAD_DIM ** -0.5)
    logf = jnp.log(lb + (1.0 - lb) * _sigmoid(fr))
    k = (1.0 - lb) * _sigmoid(-fr)
    tril = _tri(c).astype(F32)
    cum = _fdot(tril, logf)
    last_row = cum[c - 1:c, :]
    last_col = lax.dot_general(logf, jnp.ones((c, LANES), F32), (((0,), (0,)), ((), ())),
                               preferred_element_type=F32, precision=HI)

    o = _bdot(q * jnp.exp(cum), s_prev)

    q3 = q.reshape(nb, SUBLANES, LANES)
    k3 = k.reshape(nb, SUBLANES, LANES)
    cum3 = cum.reshape(nb, SUBLANES, LANES)

    sub = lax.broadcasted_iota(jnp.int32, (nb, SUBLANES, LANES), 1)
    row = lax.broadcasted_iota(jnp.int32, (c, c), 0)
    col = lax.broadcasted_iota(jnp.int32, (c, c), 1)
    att = jnp.zeros((c, c), F32)
    for d in range(SUBLANES):
        if d == 0:
            p = q3 * k3
        else:
            valid = sub >= d
            kr = pltpu.roll(k3, d, 1)
            cr = pltpu.roll(cum3, d, 1)
            p = q3 * kr * jnp.exp(jnp.where(valid, cum3 - cr, 0.0))
            p = jnp.where(valid, p, 0.0)
        a_d = jnp.sum(p.reshape(c, LANES), axis=-1, keepdims=True)
        att = att + jnp.where(col == row - d, a_d, 0.0)

    if nb > 1:
        e = cum3[:, SUBLANES - 1:SUBLANES, :]
        k_t = k3 * jnp.exp(e - cum3)
        blk = lax.broadcasted_iota(jnp.int32, (nb, SUBLANES, LANES), 0)
        q_parts, k_parts = [], []
        for jb in range(nb - 1):
            later = blk > jb
            qj = jnp.where(later, q3 * jnp.exp(jnp.where(later, cum3 - e[jb:jb + 1], 0.0)), 0.0)
            q_parts.append(qj.reshape(c, LANES).astype(BF16))
            k_parts.append(jnp.where(blk == jb, k_t, 0.0).reshape(c, LANES).astype(BF16))
        att = att + _bdot_nt(jnp.concatenate(q_parts, axis=1), jnp.concatenate(k_parts, axis=1))

    o = o + _bdot(att, v)
    s_new = s_prev * jnp.exp(last_col) + _bdot_tn(k * jnp.exp(last_row - cum), v)
    return o, s_new


def _hgrn_kernel(q_ref, f_ref, v_ref, g_ref, lb_ref, ng_ref, s0_ref, o_ref, s_ref, *, c, n_chunks):
    s_ref[...] = s0_ref[...]
    lb = lb_ref[...]
    ng = ng_ref[...]

    def body(i, carry):
        r0 = pl.multiple_of(i * c, c)
        rows = pl.ds(r0, c)
        o, s_new = _hgrn_chunk(q_ref[rows, :], f_ref[rows, :], v_ref[rows, :], lb, s_ref[...], c)
        s_ref[...] = s_new
        ms = jnp.mean(o * o, axis=-1, keepdims=True)
        o_ref[rows, :] = o * lax.rsqrt(ms + 1e-6) * ng * _sigmoid(g_ref[rows, :])
        return carry

    lax.fori_loop(0, n_chunks, body, 0)


def _hgrn_recurrence(proj, s0, lb, norm_g):
    b, l, _ = proj.shape
    c = min(CHUNK, l)
    assert l % c == 0 and c % SUBLANES == 0
    col = lambda off: pl.BlockSpec((None, l, HEAD_DIM), lambda bi, h: (bi, 0, off * HEADS + h))
    state = pl.BlockSpec((None, None, HEAD_DIM, HEAD_DIM), lambda bi, h: (bi, h, 0, 0))
    return pl.pallas_call(
        functools.partial(_hgrn_kernel, c=c, n_chunks=l // c),
        grid=(b, HEADS),
        in_specs=[col(0), col(1), col(2), col(3),
                  pl.BlockSpec((1, HEAD_DIM), lambda bi, h: (0, h)),
                  pl.BlockSpec((1, HEAD_DIM), lambda bi, h: (0, 0)),
                  state],
        out_specs=[pl.BlockSpec((None, l, HEAD_DIM), lambda bi, h: (bi, 0, h)), state],
        out_shape=[jax.ShapeDtypeStruct((b, l, HEADS * HEAD_DIM), F32),
                   jax.ShapeDtypeStruct((b, HEADS, HEAD_DIM, HEAD_DIM), F32)],
        compiler_params=_cparams("parallel", "parallel"),
    )(proj, proj, proj, proj, lb.reshape(1, -1), norm_g.reshape(1, -1), s0)


HIST = SUBLANES


def _gdn_conv(x_ext, cw, c):
    acc = x_ext * cw[CONV_W - 1:CONV_W, :]
    for w in range(CONV_W - 1):
        acc = acc + pltpu.roll(x_ext, CONV_W - 1 - w, 0) * cw[w:w + 1, :]
    return _silu(acc[HIST:, :])


def _l2norm_rows(x):
    return x * lax.rsqrt(jnp.sum(x * x, axis=-1, keepdims=True) + 1e-6)


def _gdn_chunk(q, k, v, beta, g, s_prev, c):
    low = _tri(c)
    strict = _tri(c, strict=True)
    tril = low.astype(F32)
    eye = (low & ~strict).astype(F32)
    dm = _fdot(tril, jnp.where(strict, jnp.broadcast_to(g, (c, c)), 0.0))
    decay = jnp.where(low, jnp.exp(jnp.where(low, dm, 0.0)), 0.0)
    cum = _fdot(tril, jnp.broadcast_to(g, (c, LANES)))
    ecum = jnp.exp(cum)
    last = cum[c - 1:c, :]
    kb = k * beta
    kk = _bdot_nt(jnp.concatenate([kb, q], axis=0), k)
    p = -(kk[:c] * decay * strict.astype(F32))
    att = kk[c:] * decay
    t_inv = eye + p
    x = p
    for _ in range(int(math.log2(c)) - 1):
        x = _bdot(x, x)
        t_inv = t_inv + _bdot(x, t_inv)
    uw = _bdot(t_inv, jnp.concatenate([v * beta, kb * ecum], axis=1))
    v_new = uw[:, :HEAD_DIM] - _bdot(uw[:, HEAD_DIM:], s_prev)
    o = _bdot(q * ecum, s_prev) + _bdot(att, v_new)
    s_new = s_prev * jnp.exp(last) + _bdot_tn(k * jnp.exp(last - cum), v_new)
    return o, s_new


def _gdn_kernel(q_ref, k_ref, v_ref, z_ref, ba_ref, c0q_ref, c0k_ref, c0v_ref, cwq_ref, cwk_ref, cwv_ref,
                an_ref, dt_ref, ng_ref, s0_ref, o_ref, s_ref, qx_ref, kx_ref, vx_ref, *, c, n_chunks, l):
    h = pl.program_id(1)
    for src, c0, dst in ((q_ref, c0q_ref, qx_ref), (k_ref, c0k_ref, kx_ref), (v_ref, c0v_ref, vx_ref)):
        dst[0:HIST, :] = jnp.zeros((HIST, LANES), F32)
        dst[HIST - (CONV_W - 1):HIST, :] = c0[...]
        dst[HIST:HIST + l, :] = src[...]
    s_ref[...] = s0_ref[...]
    ng = ng_ref[...]
    a_neg = an_ref[...]
    dt_b = dt_ref[...]
    lane = lax.broadcasted_iota(jnp.int32, (c, LANES), 1)

    def body(i, carry):
        r0 = pl.multiple_of(i * c, c)
        ext = pl.ds(r0, c + HIST)
        rows = pl.ds(r0, c)
        q = _l2norm_rows(_gdn_conv(qx_ref[ext, :], cwq_ref[...], c)) * (HEAD_DIM ** -0.5)
        k = _l2norm_rows(_gdn_conv(kx_ref[ext, :], cwk_ref[...], c))
        v = _gdn_conv(vx_ref[ext, :], cwv_ref[...], c)
        ba = ba_ref[rows, :]
        beta = _sigmoid(jnp.sum(jnp.where(lane == h, ba, 0.0), axis=-1, keepdims=True))
        a_raw = jnp.sum(jnp.where(lane == HEADS + h, ba, 0.0), axis=-1, keepdims=True)
        xa = a_raw + dt_b[:, 0:1]
        softplus = jnp.maximum(xa, 0.0) + jnp.log(1.0 + jnp.exp(-jnp.abs(xa)))
        g = a_neg[:, 0:1] * softplus
        o, s_new = _gdn_chunk(q, k, v, beta, g, s_ref[...], c)
        s_ref[...] = s_new
        ms = jnp.mean(o * o, axis=-1, keepdims=True)
        o_ref[rows, :] = o * lax.rsqrt(ms + 1e-6) * ng * _silu(z_ref[rows, :])
        return carry

    lax.fori_loop(0, n_chunks, body, 0)


def _gdn_recurrence(proj, conv0, s0, conv_w, a_log, dt_bias, norm_g):
    b, l, _ = proj.shape
    c = min(CHUNK, l)
    assert l % c == 0 and c % SUBLANES == 0
    col = lambda off: pl.BlockSpec((None, l, HEAD_DIM), lambda bi, h: (bi, 0, off * HEADS + h))
    c0 = lambda off: pl.BlockSpec((None, CONV_W - 1, HEAD_DIM), lambda bi, h: (bi, 0, off * HEADS + h))
    cw = lambda off: pl.BlockSpec((CONV_W, HEAD_DIM), lambda bi, h: (0, off * HEADS + h))
    per_head = pl.BlockSpec((None, 1, LANES), lambda bi, h: (h, 0, 0))
    state = pl.BlockSpec((None, None, HEAD_DIM, HEAD_DIM), lambda bi, h: (bi, h, 0, 0))
    a_neg = jnp.broadcast_to(-jnp.exp(a_log.astype(F32))[:, None, None], (HEADS, 1, LANES))
    dt_b = jnp.broadcast_to(dt_bias.astype(F32)[:, None, None], (HEADS, 1, LANES))
    xbuf = pltpu.VMEM((HIST + l, LANES), F32)
    return pl.pallas_call(
        functools.partial(_gdn_kernel, c=c, n_chunks=l // c, l=l),
        grid=(b, HEADS),
        in_specs=[col(0), col(1), col(2), col(3),
                  pl.BlockSpec((None, l, LANES), lambda bi, h: (bi, 0, 4 * HEADS)),
                  c0(0), c0(1), c0(2), cw(0), cw(1), cw(2),
                  per_head, per_head,
                  pl.BlockSpec((1, HEAD_DIM), lambda bi, h: (0, 0)),
                  state],
        out_specs=[pl.BlockSpec((None, l, HEAD_DIM), lambda bi, h: (bi, 0, h)), state],
        out_shape=[jax.ShapeDtypeStruct((b, l, HEADS * HEAD_DIM), F32),
                   jax.ShapeDtypeStruct((b, HEADS, HEAD_DIM, HEAD_DIM), F32)],
        scratch_shapes=[xbuf, xbuf, xbuf],
        compiler_params=_cparams("parallel", "parallel"),
    )(proj, proj, proj, proj, proj, conv0, conv0, conv0, conv_w, conv_w, conv_w,
      a_neg, dt_b, norm_g.reshape(1, -1), s0)


def _rel_bucket(dist):
    exact = REL_BUCKETS // 2
    d = jnp.maximum(dist, exact).astype(F32)
    large = exact + (jnp.log(d / exact) / math.log(REL_MAX_DIST / exact) * (REL_BUCKETS - exact)).astype(jnp.int32)
    return jnp.where(dist < exact, dist, jnp.minimum(large, REL_BUCKETS - 1))


def _top_mask(gate, n_valid, k, axis):
    idx = lax.broadcasted_iota(jnp.int32, gate.shape, axis)
    big = jnp.int32(2 ** 30)
    work = jnp.where(idx < n_valid, gate, NEG)
    sel = jnp.zeros(gate.shape, F32)
    for r in range(k):
        mx = jnp.max(work, axis=axis, keepdims=True)
        first = jnp.min(jnp.where(work == mx, idx, big), axis=axis, keepdims=True)
        pick = (idx == first) & (idx < n_valid)
        sel = jnp.where(pick, 1.0, sel)
        work = jnp.where(idx == first, -jnp.inf, work)
    return sel


def _moba_prompt_kernel(q_ref, k_ref, v_ref, bd_ref, bp_ref, bf_ref, o_ref, means_ref, *, nb):
    i = pl.program_id(2)
    blk = MOBA_BLOCK

    @pl.when(i == 0)
    def _():
        means_ref[...] = jnp.zeros_like(means_ref)
        for n in range(nb):
            means_ref[n:n + 1, :] = jnp.mean(k_ref[n * blk:(n + 1) * blk, :], axis=0, keepdims=True)

    q = q_ref[...] * (HEAD_DIM ** -0.5)
    gate = lax.dot_general(q, means_ref[...], (((1,), (1,)), ((), ())),
                           preferred_element_type=F32, precision=HI)
    sel = _top_mask(gate, i, MOBA_TOPK, 1)
    lane = lax.broadcasted_iota(jnp.int32, sel.shape, 1)
    qb = q.astype(BF16)

    r0 = pl.multiple_of(i * blk, blk)
    s = _bdot_nt(qb, k_ref[pl.ds(r0, blk), :]) + bd_ref[...]
    s = jnp.where(_tri(blk), s, NEG)
    m0 = jnp.max(s, axis=-1, keepdims=True)
    p = jnp.exp(s - m0)
    l0 = jnp.sum(p, axis=-1, keepdims=True)
    acc0 = _bdot(p, v_ref[pl.ds(r0, blk), :])

    def past_block(n, carry, bias):
        m, l, acc = carry
        rn = pl.multiple_of(n * blk, blk)
        picked = jnp.sum(jnp.where(lane == n, sel, 0.0), axis=-1, keepdims=True) > 0.0
        sn = jnp.where(picked, _bdot_nt(qb, k_ref[pl.ds(rn, blk), :]) + bias, NEG)
        m_new = jnp.maximum(m, jnp.max(sn, axis=-1, keepdims=True))
        alpha = jnp.exp(m - m_new)
        pn = jnp.exp(sn - m_new)
        l = l * alpha + jnp.sum(pn, axis=-1, keepdims=True)
        acc = acc * alpha + _bdot(pn, v_ref[pl.ds(rn, blk), :])
        return m_new, l, acc

    carry = lax.fori_loop(0, jnp.maximum(i - 1, 0), lambda n, cr: past_block(n, cr, bf_ref[:, 0:1]),
                          (m0, l0, acc0))
    m, l, acc = past_block(jnp.maximum(i - 1, 0), carry, bp_ref[...])
    o_ref[...] = acc / l


def _moba_bias_tables(rel_bias, lq0, blk):
    t = jnp.arange(blk, dtype=jnp.int32)
    d0 = t[:, None] - t[None, :]
    h_ix = jnp.arange(HEADS)[:, None, None]
    bias_diag = rel_bias[_rel_bucket(jnp.maximum(d0, 0))[None], h_ix].astype(F32)
    bias_prev = rel_bias[_rel_bucket(jnp.maximum(d0 + blk, 0))[None], h_ix].astype(F32)
    far = jnp.broadcast_to(rel_bias[REL_BUCKETS - 1].astype(F32)[:, None, None], (HEADS, 1, LANES))
    return bias_diag, bias_prev, far


def _moba_prompt(proj, rel_bias):
    b, l, _ = proj.shape
    blk = MOBA_BLOCK
    assert l % blk == 0 and l // blk <= LANES
    assert 2 * blk > REL_MAX_DIST
    nb = l // blk
    bias_diag, bias_prev, far = _moba_bias_tables(rel_bias, 0, blk)
    tile = pl.BlockSpec((None, blk, blk), lambda bi, h, i: (h, 0, 0))
    return pl.pallas_call(
        functools.partial(_moba_prompt_kernel, nb=nb),
        grid=(b, HEADS, nb),
        in_specs=[pl.BlockSpec((None, blk, HEAD_DIM), lambda bi, h, i: (bi, i, h)),
                  pl.BlockSpec((None, l, HEAD_DIM), lambda bi, h, i: (bi, 0, HEADS + h)),
                  pl.BlockSpec((None, l, HEAD_DIM), lambda bi, h, i: (bi, 0, 2 * HEADS + h)),
                  tile, tile,
                  pl.BlockSpec((None, 1, LANES), lambda bi, h, i: (h, 0, 0))],
        out_specs=pl.BlockSpec((None, blk, HEAD_DIM), lambda bi, h, i: (bi, i, h)),
        out_shape=jax.ShapeDtypeStruct((b, l, HEADS * HEAD_DIM), F32),
        scratch_shapes=[pltpu.VMEM((LANES, HEAD_DIM), F32)],
        compiler_params=_cparams("parallel", "parallel", "arbitrary"),
    )(proj, proj, proj, bias_diag, bias_prev, far)


def _block_diag_rows(x, lq):
    return jnp.concatenate([x[h * lq:(h + 1) * lq, h * HEAD_DIM:(h + 1) * HEAD_DIM] for h in range(HEADS)], axis=0)


def _moba_sample_kernel(pt_ref, k0_ref, k1_ref, v0_ref, v1_ref, qkv_ref, bl_ref, bc_ref, bf_ref, o_ref,
                        qbd_ref, m_ref, l_ref, acc_ref, means_ref, *, nb, lq):
    n = pl.program_id(1)
    d_model = HEADS * HEAD_DIM
    hq = HEADS * lq

    @pl.when(n == 0)
    def _():
        q = qkv_ref[:, 0:d_model] * (HEAD_DIM ** -0.5)
        rows = lax.broadcasted_iota(jnp.int32, (d_model, hq), 0) // HEAD_DIM
        cols = lax.broadcasted_iota(jnp.int32, (d_model, hq), 1)
        spread = (lax.broadcasted_iota(jnp.int32, (lq, hq), 1) % lq
                  == lax.broadcasted_iota(jnp.int32, (lq, hq), 0)).astype(F32)
        qt = lax.dot_general(q, spread, (((0,), (0,)), ((), ())),
                             preferred_element_type=F32, precision=HI)
        qbd_ref[...] = jnp.where(rows == cols // lq, qt, 0.0)
        means_ref[...] = jnp.zeros_like(means_ref)
        m_ref[...] = jnp.zeros_like(m_ref)
        l_ref[...] = jnp.zeros_like(l_ref)

    qbd = qbd_ref[...]
    kblk = jnp.concatenate([k0_ref[...], k1_ref[...]], axis=0)
    vblk = jnp.concatenate([v0_ref[...], v1_ref[...]], axis=0)
    means_ref[pl.ds(n, 1), :] = jnp.mean(kblk, axis=0, keepdims=True)
    bias = jnp.where(n == nb - 1, bl_ref[...], bf_ref[...])
    s = _bdot(kblk, qbd) + bias
    m = jnp.max(s, axis=0, keepdims=True)
    p = jnp.exp(s - m)
    m_ref[pl.ds(n, 1), :] = m
    l_ref[pl.ds(n, 1), :] = jnp.sum(p, axis=0, keepdims=True)
    acc_ref[n] = _block_diag_rows(_bdot_tn(p, vblk), lq)

    @pl.when(n == nb - 1)
    def _():
        k_new = qkv_ref[:, d_model:2 * d_model]
        v_new = qkv_ref[:, 2 * d_model:3 * d_model]
        jq = lax.broadcasted_iota(jnp.int32, (lq, hq), 1) % lq
        jk = lax.broadcasted_iota(jnp.int32, (lq, hq), 0)
        sc = jnp.where(jk <= jq, _bdot(k_new, qbd) + bc_ref[...], NEG)
        mc = jnp.max(sc, axis=0, keepdims=True)
        pc = jnp.exp(sc - mc)
        lc = jnp.sum(pc, axis=0, keepdims=True)
        oc = _block_diag_rows(_bdot_tn(pc, v_new), lq)

        gate = _fdot(means_ref[...], qbd)
        sel = _top_mask(gate, nb, MOBA_TOPK, 0)
        m_all = m_ref[...]
        m_tot = jnp.maximum(jnp.max(jnp.where(sel > 0.0, m_all, NEG), axis=0, keepdims=True), mc)
        w = jnp.where(sel > 0.0, jnp.exp(jnp.where(sel > 0.0, m_all - m_tot, 0.0)), 0.0)
        wc = jnp.exp(mc - m_tot)
        denom = jnp.sum(w * l_ref[...], axis=0, keepdims=True) + wc * lc
        nbp = w.shape[0]
        w_all = jnp.concatenate([w, wc, jnp.zeros((SUBLANES - 1, hq), F32)], axis=0) / denom
        ne = nbp + SUBLANES
        eye = (lax.broadcasted_iota(jnp.int32, (ne, ne), 0)
               == lax.broadcasted_iota(jnp.int32, (ne, ne), 1)).astype(F32)
        wt = lax.dot_general(w_all, eye, (((0,), (0,)), ((), ())),
                             preferred_element_type=F32, precision=HI)
        out = wt[:, nbp:nbp + 1] * oc
        for b_i in range(nb):
            out = out + wt[:, b_i:b_i + 1] * acc_ref[b_i]
        o_ref[...] = out


def _moba_sample(proj, cache_k, cache_v, layer, page_table, rel_bias):
    b, lq, _ = proj.shape
    n_pages = page_table.shape[1]
    d_model = HEADS * HEAD_DIM
    pages_per_block = MOBA_BLOCK // PAGE_SIZE
    assert pages_per_block == 2 and n_pages % pages_per_block == 0 and lq <= MOBA_BLOCK
    past = n_pages * PAGE_SIZE
    nb = n_pages // pages_per_block
    assert nb >= 1 and MOBA_BLOCK >= REL_MAX_DIST
    nbp = _round_up(nb, SUBLANES)
    hq = HEADS * lq
    jq = jnp.arange(lq, dtype=jnp.int32)
    h_cols = jnp.repeat(jnp.arange(HEADS), lq)[None, :]
    t = jnp.arange(MOBA_BLOCK, dtype=jnp.int32)
    dist_last = (past + jnp.tile(jq, HEADS))[None, :] - (past - MOBA_BLOCK + t)[:, None]
    bias_last = rel_bias[_rel_bucket(jnp.maximum(dist_last, 0)), h_cols].astype(F32)
    dist_cur = jnp.tile(jq, HEADS)[None, :] - jq[:, None]
    bias_cur = rel_bias[_rel_bucket(jnp.maximum(dist_cur, 0)), h_cols].astype(F32)
    bias_far = jnp.repeat(rel_bias[REL_BUCKETS - 1].astype(F32), lq)[None, :]

    page = lambda off: pl.BlockSpec((None, None, PAGE_SIZE, d_model),
                                    lambda bi, n, pt: (layer, pt[bi, pages_per_block * n + off], 0, 0))
    full = lambda shape: pl.BlockSpec(shape, lambda bi, n, pt: (0,) * len(shape))
    grid_spec = pltpu.PrefetchScalarGridSpec(
        num_scalar_prefetch=1,
        grid=(b, nb),
        in_specs=[page(0), page(1), page(0), page(1),
                  pl.BlockSpec((None, lq, 3 * d_model), lambda bi, n, pt: (bi, 0, 0)),
                  full((MOBA_BLOCK, hq)), full((lq, hq)), full((1, hq))],
        out_specs=pl.BlockSpec((None, hq, HEAD_DIM), lambda bi, n, pt: (bi, 0, 0)),
        scratch_shapes=[pltpu.VMEM((d_model, hq), F32),
                        pltpu.VMEM((nbp, hq), F32),
                        pltpu.VMEM((nbp, hq), F32),
                        pltpu.VMEM((nb, hq, HEAD_DIM), F32),
                        pltpu.VMEM((nbp, d_model), F32)])
    return pl.pallas_call(
        functools.partial(_moba_sample_kernel, nb=nb, lq=lq),
        grid_spec=grid_spec,
        out_shape=jax.ShapeDtypeStruct((b, hq, HEAD_DIM), F32),
        compiler_params=_cparams("parallel", "arbitrary"),
    )(page_table, cache_k, cache_k, cache_v, cache_v, proj, bias_last, bias_cur, bias_far)


def _trunk(x, mods, s_hgrn, s_conv, s_gdn, paged, W):
    b, l, d = x.shape
    tiling = _row_tile(b, l)
    x2 = x.reshape(b * l, d)
    new_h, new_c, new_g, new_k, new_v = [], [], [], [], []
    for i in range(DEPTH):
        mod = _mod_layout(mods[i], b, l)
        x2 = _ffn_block(x2, mod, tiling, 0, W['ffn1_up'], W['ffn1_down'], i, W['ln_g'][i, 0], W['ln_b'][i, 0])
        kind, li = i % N_MIXERS, i // N_MIXERS
        if kind == 0:
            proj = _proj_in(x2, mod, tiling, 1, W['hgrn_w_in'][li], 1024)
            o, s = _hgrn_recurrence(proj.reshape(b, l, -1), s_hgrn[li], W['hgrn_lb'][li], W['hgrn_norm_g'][li])
            new_h.append(s)
            w_o = W['hgrn_w_o'][li]
            o2 = o.reshape(b * l, d)
        elif kind == 1:
            proj = _proj_in(x2, mod, tiling, 1, W['gdn_w_in'][li], W['gdn_w_in'][li].shape[1] // 3)
            proj = proj.reshape(b, l, -1)
            o, s = _gdn_recurrence(proj, s_conv[li], s_gdn[li], W['gdn_conv_w'][li], W['gdn_a_log'][li],
                                   W['gdn_dt_bias'][li], W['gdn_norm_g'][li])
            n_qkv = 3 * HEADS * HEAD_DIM
            assert l >= CONV_W - 1
            new_c.append(proj[:, l - (CONV_W - 1):, :n_qkv])
            new_g.append(s)
            w_o = W['gdn_w_o'][li]
            o2 = o.reshape(b * l, d)
        else:
            proj = _proj_in(x2, mod, tiling, 1, W['moba_w_qkv'][li], 1024).reshape(b, l, -1)
            if paged is None:
                o2 = _moba_prompt(proj, W['rel_bias']).reshape(b * l, d)
            else:
                ck, cv, pt = paged
                o = _moba_sample(proj, ck, cv, li, pt, W['rel_bias'])
                o2 = jnp.transpose(o.reshape(b, HEADS, l, HEAD_DIM), (0, 2, 1, 3)).reshape(b * l, d)
            new_k.append(proj[:, :, d:2 * d].reshape(b, l, HEADS, HEAD_DIM))
            new_v.append(proj[:, :, 2 * d:3 * d].reshape(b, l, HEADS, HEAD_DIM))
            w_o = W['moba_w_o'][li]
        x2 = _proj_out(o2, x2, mod, tiling, 1, w_o, W['ln_g'][i, 1], W['ln_b'][i, 1])
        x2 = _ffn_block(x2, mod, tiling, 2, W['ffn2_up'], W['ffn2_down'], i, W['ln_g'][i, 2], W['ln_b'][i, 2])
    return (x2.reshape(b, l, d), jnp.stack(new_h), jnp.stack(new_c), jnp.stack(new_g),
            jnp.stack(new_k), jnp.stack(new_v))


def kernel(x_prompt, x_sample, state_hgrn, state_gdn_conv, state_gdn, cache_k, cache_v, page_table, c_prompt, c_sample, w_ada, b_ada, ln_g, ln_b, w_ffn1_up, w_ffn1_down, w_ffn2_up, w_ffn2_down, hgrn_w_in, hgrn_lb_logits, hgrn_norm_g, hgrn_w_o, gdn_w_in, gdn_conv_w, gdn_a_log, gdn_dt_bias, gdn_norm_g, gdn_w_o, moba_w_qkv, moba_w_o, rel_bias):
    bp, _, d = x_prompt.shape
    bs = x_sample.shape[0]
    assert d == HEADS * HEAD_DIM

    nc = _round_up(bp + bs, SUBLANES)
    c_all = jnp.pad(jnp.concatenate([c_prompt, c_sample], axis=0), ((0, nc - bp - bs), (0, 0)))
    mods = _ada_mods(c_all, w_ada, b_ada).reshape(DEPTH, nc, 9, d)

    ffn1_up, ffn1_down = _prep_ffn_weights(w_ffn1_up, w_ffn1_down)
    ffn2_up, ffn2_down = _prep_ffn_weights(w_ffn2_up, w_ffn2_down)
    gdn_n = _round_up(gdn_w_in.shape[2], 3 * LANES)
    lbp = jax.nn.softmax(hgrn_lb_logits.astype(F32), axis=0)
    W = {
        'ffn1_up': ffn1_up, 'ffn1_down': ffn1_down, 'ffn2_up': ffn2_up, 'ffn2_down': ffn2_down,
        'ln_g': ln_g, 'ln_b': ln_b,
        'hgrn_w_in': hgrn_w_in.astype(BF16), 'hgrn_lb': jnp.cumsum(lbp, axis=0) - lbp[0],
        'hgrn_norm_g': hgrn_norm_g, 'hgrn_w_o': hgrn_w_o.astype(BF16),
        'gdn_w_in': jnp.pad(gdn_w_in, ((0, 0), (0, 0), (0, gdn_n - gdn_w_in.shape[2]))).astype(BF16),
        'gdn_conv_w': gdn_conv_w, 'gdn_a_log': gdn_a_log, 'gdn_dt_bias': gdn_dt_bias,
        'gdn_norm_g': gdn_norm_g, 'gdn_w_o': gdn_w_o.astype(BF16),
        'moba_w_qkv': moba_w_qkv.astype(BF16), 'moba_w_o': moba_w_o.astype(BF16), 'rel_bias': rel_bias,
    }

    n_a, n_b = state_hgrn.shape[0], state_gdn.shape[0]
    z_hgrn = jnp.zeros((n_a, bp) + state_hgrn.shape[2:], F32)
    z_conv = jnp.zeros((n_b, bp) + state_gdn_conv.shape[2:], F32)
    z_gdn = jnp.zeros((n_b, bp) + state_gdn.shape[2:], F32)
    y_p, hgrn_p, conv_p, gdn_p, k_p, v_p = _trunk(x_prompt, mods[:, :bp], z_hgrn, z_conv, z_gdn, None, W)

    n_c, n_pool = cache_k.shape[:2]
    ck = cache_k.reshape(n_c, n_pool, PAGE_SIZE, d)
    cv = cache_v.reshape(n_c, n_pool, PAGE_SIZE, d)
    y_s, hgrn_s, conv_s, gdn_s, k_s, v_s = _trunk(x_sample, mods[:, bp:bp + bs], state_hgrn, state_gdn_conv,
                                                  state_gdn, (ck, cv, page_table), W)
    return (y_p, y_s, hgrn_p, hgrn_s, conv_p, conv_s, gdn_p, gdn_s, k_p, k_s, v_p, v_s)
```

```python
import functools
import math

import jax
import jax.numpy as jnp
from jax import lax
from jax.experimental import pallas as pl
from jax.experimental.pallas import tpu as pltpu

F32 = jnp.float32
BF16 = jnp.bfloat16
HI = lax.Precision.HIGHEST

DEPTH = 4
N_MIXERS = 3
HEADS = 8
HEAD_DIM = 128
CONV_W = 4
CHUNK = 64
MOBA_BLOCK = 256
MOBA_TOPK = 3
REL_BUCKETS = 32
REL_MAX_DIST = 128
PAGE_SIZE = 128
DN_ALPHA = (2 * DEPTH) ** 0.25
LN_EPS = 1e-5
NEG = -1e30

LANES = 128
SUBLANES = 8
MXU_WIDTH = 256
VMEM_LIMIT_BYTES = 56 * 1024 * 1024


def _cparams(*sem):
    return pltpu.CompilerParams(dimension_semantics=sem, vmem_limit_bytes=VMEM_LIMIT_BYTES)


def _round_up(n, m):
    return (n + m - 1) // m * m


def _bdot(a, b):
    return jnp.dot(a.astype(BF16), b.astype(BF16), preferred_element_type=F32)


def _bdot_nt(a, b):
    return lax.dot_general(a.astype(BF16), b.astype(BF16), (((1,), (1,)), ((), ())),
                           preferred_element_type=F32)


def _bdot_tn(a, b):
    return lax.dot_general(a.astype(BF16), b.astype(BF16), (((0,), (0,)), ((), ())),
                           preferred_element_type=F32)


def _fdot(a, b):
    return jnp.dot(a, b, preferred_element_type=F32, precision=HI)


def _sigmoid(x):
    return 1.0 / (1.0 + jnp.exp(-x))


def _silu(x):
    return x * _sigmoid(x)


def _layer_norm_rows(z, g, b):
    mu = jnp.mean(z, axis=-1, keepdims=True)
    zc = z - mu
    var = jnp.mean(zc * zc, axis=-1, keepdims=True)
    return zc * lax.rsqrt(var + LN_EPS) * g + b


def _ada_kernel(c_ref, w_ref, b_ref, o_ref):
    o_ref[...] = _bdot(_silu(c_ref[...]), w_ref[...]) + b_ref[...]


def _ada_mods(c_all, w_ada, b_ada):
    depth, d, n = w_ada.shape
    nb = c_all.shape[0]
    tn = n // 8
    assert n % 8 == 0 and tn % LANES == 0
    return pl.pallas_call(
        _ada_kernel,
        grid=(depth, n // tn),
        in_specs=[pl.BlockSpec((nb, d), lambda l, j: (0, 0)),
                  pl.BlockSpec((None, d, tn), lambda l, j: (l, 0, j)),
                  pl.BlockSpec((None, 1, tn), lambda l, j: (l, 0, j))],
        out_specs=pl.BlockSpec((None, nb, tn), lambda l, j: (l, 0, j)),
        out_shape=jax.ShapeDtypeStruct((depth, nb, n), F32),
        name="ada_mods",
        compiler_params=_cparams("parallel", "parallel"),
    )(c_all, w_ada, b_ada.reshape(depth, 1, n))


MAX_ROW_TILE = 1024


def _row_tile(batch, length):
    if length >= MXU_WIDTH:
        tm = MAX_ROW_TILE
        while length % tm:
            tm //= 2
        return tm, length // tm, True
    rows = batch * length
    assert rows % SUBLANES == 0 and rows <= MAX_ROW_TILE
    return rows, 1, False


def _mod_layout(m, batch, length):
    tm, _, per_batch = _row_tile(batch, length)
    mt = jnp.transpose(m, (1, 0, 2))
    if per_batch:
        return mt[:, :, None, :]
    return jnp.repeat(mt, length, axis=1)[:, None, :, :]


def _mod_spec(mod, tiles_per_block):
    _, _, rm, d = mod.shape
    return pl.BlockSpec((9, None, rm, d), lambda i, *_: (0, i // tiles_per_block, 0, 0))


FF_CHUNK = MXU_WIDTH


def _ffn_kernel(x_ref, mod_ref, wa_ref, wu_ref, wd_ref, g_ref, b_ref, o_ref, h_ref, *, j, n_chunks):
    x = x_ref[...]
    h_ref[...] = (x * (1.0 + mod_ref[3 * j + 1]) + mod_ref[3 * j]).astype(BF16)
    o_ref[...] = jnp.zeros_like(o_ref)

    def body(c, carry):
        c0 = pl.multiple_of(c * FF_CHUNK, FF_CHUNK)
        h = h_ref[...]
        a = jnp.dot(h, wa_ref[:, pl.ds(c0, FF_CHUNK)], preferred_element_type=F32)
        u = jnp.dot(h, wu_ref[:, pl.ds(c0, FF_CHUNK)], preferred_element_type=F32)
        act = (_silu(a) * u).astype(BF16)
        o_ref[...] += jnp.dot(act, wd_ref[pl.ds(c0, FF_CHUNK), :], preferred_element_type=F32)
        return carry

    lax.fori_loop(0, n_chunks, body, 0)
    z = DN_ALPHA * x + (1.0 + mod_ref[3 * j + 2]) * (0.5 * o_ref[...])
    o_ref[...] = _layer_norm_rows(z, g_ref[...], b_ref[...])


def _ffn_block(x2d, mod, tiling, j, w_up, w_down, layer, ln_g, ln_b):
    tm, tpb, _ = tiling
    rows, d = x2d.shape
    fp = w_up.shape[-1]
    kern = functools.partial(_ffn_kernel, j=j, n_chunks=fp // FF_CHUNK)
    return pl.pallas_call(
        kern,
        grid=(rows // tm,),
        in_specs=[pl.BlockSpec((tm, d), lambda i: (i, 0)),
                  _mod_spec(mod, tpb),
                  pl.BlockSpec((None, None, d, fp), lambda i: (layer, 0, 0, 0), pipeline_mode=pl.Buffered(1)),
                  pl.BlockSpec((None, None, d, fp), lambda i: (layer, 1, 0, 0), pipeline_mode=pl.Buffered(1)),
                  pl.BlockSpec((None, fp, d), lambda i: (layer, 0, 0), pipeline_mode=pl.Buffered(1)),
                  pl.BlockSpec((1, d), lambda i: (0, 0)),
                  pl.BlockSpec((1, d), lambda i: (0, 0))],
        out_specs=pl.BlockSpec((tm, d), lambda i: (i, 0)),
        out_shape=jax.ShapeDtypeStruct((rows, d), F32),
        scratch_shapes=[pltpu.VMEM((tm, d), BF16)],
        name="ffn_block",
        compiler_params=_cparams("parallel"),
    )(x2d, mod, w_up, w_up, w_down, ln_g.reshape(1, d), ln_b.reshape(1, d))


def _prep_ffn_weights(w_up, w_down):
    depth, d, two_ff = w_up.shape
    d_ff = two_ff // 2
    fp = _round_up(d_ff, FF_CHUNK)
    up = jnp.transpose(w_up.reshape(depth, d, 2, d_ff), (0, 2, 1, 3))
    up = jnp.pad(up, ((0, 0), (0, 0), (0, 0), (0, fp - d_ff))).astype(BF16)
    down = jnp.pad(w_down, ((0, 0), (0, fp - d_ff), (0, 0))).astype(BF16)
    return up, down


def _proj_in_kernel(x_ref, mod_ref, w_ref, o_ref, h_ref, *, j):
    @pl.when(pl.program_id(1) == 0)
    def _():
        h_ref[...] = (x_ref[...] * (1.0 + mod_ref[3 * j + 1]) + mod_ref[3 * j]).astype(BF16)

    o_ref[...] = jnp.dot(h_ref[...], w_ref[...], preferred_element_type=F32)


def _proj_in(x2d, mod, tiling, j, w, tn):
    tm, tpb, _ = tiling
    rows, d = x2d.shape
    n = w.shape[1]
    assert n % tn == 0
    return pl.pallas_call(
        functools.partial(_proj_in_kernel, j=j),
        grid=(rows // tm, n // tn),
        in_specs=[pl.BlockSpec((tm, d), lambda i, k: (i, 0)),
                  _mod_spec(mod, tpb),
                  pl.BlockSpec((d, tn), lambda i, k: (0, k))],
        out_specs=pl.BlockSpec((tm, tn), lambda i, k: (i, k)),
        out_shape=jax.ShapeDtypeStruct((rows, n), F32),
        scratch_shapes=[pltpu.VMEM((tm, d), BF16)],
        name="proj_in",
        compiler_params=_cparams("parallel", "arbitrary"),
    )(x2d, mod, w)


def _proj_out_kernel(o_ref, x_ref, mod_ref, w_ref, g_ref, b_ref, y_ref, *, j):
    y = jnp.dot(o_ref[...].astype(BF16), w_ref[...], preferred_element_type=F32)
    z = DN_ALPHA * x_ref[...] + (1.0 + mod_ref[3 * j + 2]) * y
    y_ref[...] = _layer_norm_rows(z, g_ref[...], b_ref[...])


def _proj_out(o2d, x2d, mod, tiling, j, w, ln_g, ln_b):
    tm, tpb, _ = tiling
    rows, d = x2d.shape
    k = o2d.shape[1]
    return pl.pallas_call(
        functools.partial(_proj_out_kernel, j=j),
        grid=(rows // tm,),
        in_specs=[pl.BlockSpec((tm, k), lambda i: (i, 0)),
                  pl.BlockSpec((tm, d), lambda i: (i, 0)),
                  _mod_spec(mod, tpb),
                  pl.BlockSpec((k, d), lambda i: (0, 0)),
                  pl.BlockSpec((1, d), lambda i: (0, 0)),
                  pl.BlockSpec((1, d), lambda i: (0, 0))],
        out_specs=pl.BlockSpec((tm, d), lambda i: (i, 0)),
        out_shape=jax.ShapeDtypeStruct((rows, d), F32),
        name="proj_out",
        compiler_params=_cparams("parallel"),
    )(o2d, x2d, mod, w, ln_g.reshape(1, d), ln_b.reshape(1, d))


def _tri(c, strict=False):
    r = lax.broadcasted_iota(jnp.int32, (c, c), 0)
    s = lax.broadcasted_iota(jnp.int32, (c, c), 1)
    return (r > s) if strict else (r >= s)


def _hgrn_chunk(qr, fr, v, lb, s_prev, c):
    nb = c // SUBLANES
    q = _silu(qr) * (HEAD_DIM ** -0.5)
    logf = jnp.log(lb + (1.0 - lb) * _sigmoid(fr))
    k = (1.0 - lb) * _sigmoid(-fr)
    tril = _tri(c).astype(F32)
    cum = _fdot(tril, logf)
    last_row = cum[c - 1:c, :]
    last_col = lax.dot_general(logf, jnp.ones((c, LANES), F32), (((0,), (0,)), ((), ())),
                               preferred_element_type=F32, precision=HI)

    o = _bdot(q * jnp.exp(cum), s_prev)

    q3 = q.reshape(nb, SUBLANES, LANES)
    k3 = k.reshape(nb, SUBLANES, LANES)
    cum3 = cum.reshape(nb, SUBLANES, LANES)

    sub = lax.broadcasted_iota(jnp.int32, (nb, SUBLANES, LANES), 1)
    row = lax.broadcasted_iota(jnp.int32, (c, c), 0)
    col = lax.broadcasted_iota(jnp.int32, (c, c), 1)
    att = jnp.zeros((c, c), F32)
    for d in range(SUBLANES):
        if d == 0:
            p = q3 * k3
        else:
            valid = sub >= d
            kr = pltpu.roll(k3, d, 1)
            cr = pltpu.roll(cum3, d, 1)
            p = q3 * kr * jnp.exp(jnp.where(valid, cum3 - cr, 0.0))
            p = jnp.where(valid, p, 0.0)
        a_d = jnp.sum(p.reshape(c, LANES), axis=-1, keepdims=True)
        att = att + jnp.where(col == row - d, a_d, 0.0)

    if nb > 1:
        e = cum3[:, SUBLANES - 1:SUBLANES, :]
        k_t = k3 * jnp.exp(e - cum3)
        blk = lax.broadcasted_iota(jnp.int32, (nb, SUBLANES, LANES), 0)
        q_parts, k_parts = [], []
        for jb in range(nb - 1):
            later = blk > jb
            qj = jnp.where(later, q3 * jnp.exp(jnp.where(later, cum3 - e[jb:jb + 1], 0.0)), 0.0)
            q_parts.append(qj.reshape(c, LANES).astype(BF16))
            k_parts.append(jnp.where(blk == jb, k_t, 0.0).reshape(c, LANES).astype(BF16))
        att = att + _bdot_nt(jnp.concatenate(q_parts, axis=1), jnp.concatenate(k_parts, axis=1))

    o = o + _bdot(att, v)
    s_new = s_prev * jnp.exp(last_col) + _bdot_tn(k * jnp.exp(last_row - cum), v)
    return o, s_new


REC_L_TILE = 512
HGRN_HEADS_PER_STEP = 8
GDN_HEADS_PER_STEP = 4


def _head_lanes(hh):
    return slice(hh * HEAD_DIM, (hh + 1) * HEAD_DIM)


def _rec_tiling(l):
    c = min(CHUNK, l)
    tl = min(REC_L_TILE, l)
    assert l % tl == 0 and tl % c == 0 and c % SUBLANES == 0
    return c, tl


def _hgrn_kernel(q_ref, f_ref, v_ref, g_ref, lb_ref, ng_ref, s0_ref, o_ref, s_ref, *, c, n_chunks, hb):
    @pl.when(pl.program_id(2) == 0)
    def _():
        s_ref[...] = s0_ref[...]

    ng = ng_ref[...]

    def body(i, carry):
        rows = pl.ds(pl.multiple_of(i * c, c), c)
        for hh in range(hb):
            ln = _head_lanes(hh)
            o, s_new = _hgrn_chunk(q_ref[rows, ln], f_ref[rows, ln], v_ref[rows, ln], lb_ref[:, ln], s_ref[hh], c)
            s_ref[hh] = s_new
            ms = jnp.mean(o * o, axis=-1, keepdims=True)
            o_ref[rows, ln] = o * lax.rsqrt(ms + 1e-6) * ng * _sigmoid(g_ref[rows, ln])
        return carry

    lax.fori_loop(0, n_chunks, body, 0)


def _hgrn_recurrence(proj, s0, lb, norm_g):
    b, l, _ = proj.shape
    c, tl = _rec_tiling(l)
    hb = HGRN_HEADS_PER_STEP
    ng_ = HEADS // hb
    col = lambda off: pl.BlockSpec((None, tl, hb * HEAD_DIM), lambda bi, hg, t: (bi, t, off * ng_ + hg))
    state = pl.BlockSpec((None, hb, HEAD_DIM, HEAD_DIM), lambda bi, hg, t: (bi, hg, 0, 0))
    return pl.pallas_call(
        functools.partial(_hgrn_kernel, c=c, n_chunks=tl // c, hb=hb),
        grid=(b, ng_, l // tl),
        in_specs=[col(0), col(1), col(2), col(3),
                  pl.BlockSpec((1, hb * HEAD_DIM), lambda bi, hg, t: (0, hg)),
                  pl.BlockSpec((1, HEAD_DIM), lambda bi, hg, t: (0, 0)),
                  state],
        out_specs=[col(0), state],
        out_shape=[jax.ShapeDtypeStruct((b, l, HEADS * HEAD_DIM), F32),
                   jax.ShapeDtypeStruct((b, HEADS, HEAD_DIM, HEAD_DIM), F32)],
        name="hgrn_recurrence",
        compiler_params=_cparams("parallel", "parallel", "arbitrary"),
    )(proj, proj, proj, proj, lb.reshape(1, -1), norm_g.reshape(1, -1), s0)


HIST = SUBLANES


def _gdn_conv(x_ext, cw, c):
    acc = x_ext * cw[CONV_W - 1:CONV_W, :]
    for w in range(CONV_W - 1):
        acc = acc + pltpu.roll(x_ext, CONV_W - 1 - w, 0) * cw[w:w + 1, :]
    return _silu(acc[HIST:, :])


def _l2norm_rows(x):
    return x * lax.rsqrt(jnp.sum(x * x, axis=-1, keepdims=True) + 1e-6)


def _gdn_chunk(q, k, v, beta, g, s_prev, c):
    low = _tri(c)
    strict = _tri(c, strict=True)
    tril = low.astype(F32)
    eye = (low & ~strict).astype(F32)
    dm = _fdot(tril, jnp.where(strict, jnp.broadcast_to(g, (c, c)), 0.0))
    decay = jnp.where(low, jnp.exp(jnp.where(low, dm, 0.0)), 0.0)
    cum = _fdot(tril, jnp.broadcast_to(g, (c, LANES)))
    ecum = jnp.exp(cum)
    last = cum[c - 1:c, :]
    kb = k * beta
    kk = _bdot_nt(jnp.concatenate([kb, q], axis=0), k)
    p = -(kk[:c] * decay * strict.astype(F32))
    att = kk[c:] * decay
    t_inv = eye + p
    x = p
    for _ in range(int(math.log2(c)) - 1):
        x = _bdot(x, x)
        t_inv = t_inv + _bdot(x, t_inv)
    uw = _bdot(t_inv, jnp.concatenate([v * beta, kb * ecum], axis=1))
    v_new = uw[:, :HEAD_DIM] - _bdot(uw[:, HEAD_DIM:], s_prev)
    o = _bdot(q * ecum, s_prev) + _bdot(att, v_new)
    s_new = s_prev * jnp.exp(last) + _bdot_tn(k * jnp.exp(last - cum), v_new)
    return o, s_new


def _gdn_kernel(q_ref, k_ref, v_ref, z_ref, ba_ref, c0q_ref, c0k_ref, c0v_ref, cwq_ref, cwk_ref, cwv_ref,
                an_ref, dt_ref, ng_ref, s0_ref, o_ref, s_ref, qx_ref, kx_ref, vx_ref, *, c, n_chunks, tl, hb):
    hg = pl.program_id(1)
    t = pl.program_id(2)
    staged = ((q_ref, c0q_ref, qx_ref), (k_ref, c0k_ref, kx_ref), (v_ref, c0v_ref, vx_ref))

    @pl.when(t == 0)
    def _():
        s_ref[...] = s0_ref[...]
        for _, c0, dst in staged:
            dst[0:HIST, :] = jnp.zeros((HIST, hb * HEAD_DIM), F32)
            dst[HIST - (CONV_W - 1):HIST, :] = c0[...]

    @pl.when(t > 0)
    def _():
        for _, _, dst in staged:
            dst[0:HIST, :] = dst[tl:tl + HIST, :]

    for src, _, dst in staged:
        dst[HIST:HIST + tl, :] = src[...]
    ng = ng_ref[...]
    lane = lax.broadcasted_iota(jnp.int32, (c, LANES), 1)

    def body(i, carry):
        r0 = pl.multiple_of(i * c, c)
        ext = pl.ds(r0, c + HIST)
        rows = pl.ds(r0, c)
        ba = ba_ref[rows, :]
        for hh in range(hb):
            ln = _head_lanes(hh)
            head = hg * hb + hh
            q = _l2norm_rows(_gdn_conv(qx_ref[ext, ln], cwq_ref[:, ln], c)) * (HEAD_DIM ** -0.5)
            k = _l2norm_rows(_gdn_conv(kx_ref[ext, ln], cwk_ref[:, ln], c))
            v = _gdn_conv(vx_ref[ext, ln], cwv_ref[:, ln], c)
            beta = _sigmoid(jnp.sum(jnp.where(lane == head, ba, 0.0), axis=-1, keepdims=True))
            a_raw = jnp.sum(jnp.where(lane == HEADS + head, ba, 0.0), axis=-1, keepdims=True)
            xa = a_raw + dt_ref[:, ln][:, 0:1]
            softplus = jnp.maximum(xa, 0.0) + jnp.log(1.0 + jnp.exp(-jnp.abs(xa)))
            g = an_ref[:, ln][:, 0:1] * softplus
            o, s_new = _gdn_chunk(q, k, v, beta, g, s_ref[hh], c)
            s_ref[hh] = s_new
            ms = jnp.mean(o * o, axis=-1, keepdims=True)
            o_ref[rows, ln] = o * lax.rsqrt(ms + 1e-6) * ng * _silu(z_ref[rows, ln])
        return carry

    lax.fori_loop(0, n_chunks, body, 0)


def _gdn_recurrence(proj, conv0, s0, conv_w, a_log, dt_bias, norm_g):
    b, l, _ = proj.shape
    c, tl = _rec_tiling(l)
    hb = GDN_HEADS_PER_STEP
    ng_ = HEADS // hb
    w = hb * HEAD_DIM
    col = lambda off: pl.BlockSpec((None, tl, w), lambda bi, hg, t: (bi, t, off * ng_ + hg))
    c0 = lambda off: pl.BlockSpec((None, CONV_W - 1, w), lambda bi, hg, t: (bi, 0, off * ng_ + hg))
    cw = lambda off: pl.BlockSpec((CONV_W, w), lambda bi, hg, t: (0, off * ng_ + hg))
    per_head = pl.BlockSpec((1, w), lambda bi, hg, t: (0, hg))
    state = pl.BlockSpec((None, hb, HEAD_DIM, HEAD_DIM), lambda bi, hg, t: (bi, hg, 0, 0))
    a_neg = jnp.repeat(-jnp.exp(a_log.astype(F32)), HEAD_DIM)[None, :]
    dt_b = jnp.repeat(dt_bias.astype(F32), HEAD_DIM)[None, :]
    xbuf = pltpu.VMEM((HIST + tl, w), F32)
    return pl.pallas_call(
        functools.partial(_gdn_kernel, c=c, n_chunks=tl // c, tl=tl, hb=hb),
        grid=(b, ng_, l // tl),
        in_specs=[col(0), col(1), col(2), col(3),
                  pl.BlockSpec((None, tl, LANES), lambda bi, hg, t: (bi, t, 4 * HEADS)),
                  c0(0), c0(1), c0(2), cw(0), cw(1), cw(2),
                  per_head, per_head,
                  pl.BlockSpec((1, HEAD_DIM), lambda bi, hg, t: (0, 0)),
                  state],
        out_specs=[col(0), state],
        out_shape=[jax.ShapeDtypeStruct((b, l, HEADS * HEAD_DIM), F32),
                   jax.ShapeDtypeStruct((b, HEADS, HEAD_DIM, HEAD_DIM), F32)],
        scratch_shapes=[xbuf, xbuf, xbuf],
        name="gdn_recurrence",
        compiler_params=_cparams("parallel", "parallel", "arbitrary"),
    )(proj, proj, proj, proj, proj, conv0, conv0, conv0, conv_w, conv_w, conv_w,
      a_neg, dt_b, norm_g.reshape(1, -1), s0)


def _rel_bucket(dist):
    exact = REL_BUCKETS // 2
    d = jnp.maximum(dist, exact).astype(F32)
    large = exact + (jnp.log(d / exact) / math.log(REL_MAX_DIST / exact) * (REL_BUCKETS - exact)).astype(jnp.int32)
    return jnp.where(dist < exact, dist, jnp.minimum(large, REL_BUCKETS - 1))


def _top_mask(gate, n_valid, k, axis):
    idx = lax.broadcasted_iota(jnp.int32, gate.shape, axis)
    big = jnp.int32(2 ** 30)
    work = jnp.where(idx < n_valid, gate, NEG)
    sel = jnp.zeros(gate.shape, F32)
    for r in range(k):
        mx = jnp.max(work, axis=axis, keepdims=True)
        first = jnp.min(jnp.where(work == mx, idx, big), axis=axis, keepdims=True)
        pick = (idx == first) & (idx < n_valid)
        sel = jnp.where(pick, 1.0, sel)
        work = jnp.where(idx == first, -jnp.inf, work)
    return sel


def _moba_prompt_kernel(q_ref, k_ref, v_ref, bd_ref, bp_ref, bf_ref, o_ref, means_ref, *, nb):
    i = pl.program_id(2)
    blk = MOBA_BLOCK

    @pl.when(i == 0)
    def _():
        means_ref[...] = jnp.zeros_like(means_ref)
        for n in range(nb):
            means_ref[n:n + 1, :] = jnp.mean(k_ref[n * blk:(n + 1) * blk, :], axis=0, keepdims=True)

    q = q_ref[...] * (HEAD_DIM ** -0.5)
    gate = lax.dot_general(q, means_ref[...], (((1,), (1,)), ((), ())),
                           preferred_element_type=F32, precision=HI)
    sel = _top_mask(gate, i, MOBA_TOPK, 1)
    lane = lax.broadcasted_iota(jnp.int32, sel.shape, 1)
    qb = q.astype(BF16)

    r0 = pl.multiple_of(i * blk, blk)
    s = _bdot_nt(qb, k_ref[pl.ds(r0, blk), :]) + bd_ref[...]
    s = jnp.where(_tri(blk), s, NEG)
    m0 = jnp.max(s, axis=-1, keepdims=True)
    p = jnp.exp(s - m0)
    l0 = jnp.sum(p, axis=-1, keepdims=True)
    acc0 = _bdot(p, v_ref[pl.ds(r0, blk), :])

    def past_block(n, carry, bias):
        m, l, acc = carry
        rn = pl.multiple_of(n * blk, blk)
        picked = jnp.sum(jnp.where(lane == n, sel, 0.0), axis=-1, keepdims=True) > 0.0
        sn = jnp.where(picked, _bdot_nt(qb, k_ref[pl.ds(rn, blk), :]) + bias, NEG)
        m_new = jnp.maximum(m, jnp.max(sn, axis=-1, keepdims=True))
        alpha = jnp.exp(m - m_new)
        pn = jnp.exp(sn - m_new)
        l = l * alpha + jnp.sum(pn, axis=-1, keepdims=True)
        acc = acc * alpha + _bdot(pn, v_ref[pl.ds(rn, blk), :])
        return m_new, l, acc

    carry = lax.fori_loop(0, jnp.maximum(i - 1, 0), lambda n, cr: past_block(n, cr, bf_ref[:, 0:1]),
                          (m0, l0, acc0))
    m, l, acc = past_block(jnp.maximum(i - 1, 0), carry, bp_ref[...])
    o_ref[...] = acc / l


def _bias_of_distance(rel_bias, dist):
    bucket = _rel_bucket(jnp.maximum(dist, 0))[None]
    out = jnp.zeros((HEADS,) + dist.shape, F32)
    for b in range(REL_BUCKETS):
        out = jnp.where(bucket == b, rel_bias[b].astype(F32).reshape((HEADS,) + (1,) * dist.ndim), out)
    return out


def _moba_bias_tables(rel_bias, blk):
    t = jnp.arange(blk, dtype=jnp.int32)
    d0 = t[:, None] - t[None, :]
    bias_diag = _bias_of_distance(rel_bias, d0)
    bias_prev = _bias_of_distance(rel_bias, d0 + blk)
    far = jnp.broadcast_to(rel_bias[REL_BUCKETS - 1].astype(F32)[:, None, None], (HEADS, 1, LANES))
    return bias_diag, bias_prev, far


def _moba_prompt(proj, rel_bias):
    b, l, _ = proj.shape
    blk = MOBA_BLOCK
    assert l % blk == 0 and l // blk <= LANES
    assert 2 * blk > REL_MAX_DIST
    nb = l // blk
    bias_diag, bias_prev, far = _moba_bias_tables(rel_bias, blk)
    tile = pl.BlockSpec((None, blk, blk), lambda bi, h, i: (h, 0, 0))
    return pl.pallas_call(
        functools.partial(_moba_prompt_kernel, nb=nb),
        grid=(b, HEADS, nb),
        in_specs=[pl.BlockSpec((None, blk, HEAD_DIM), lambda bi, h, i: (bi, i, h)),
                  pl.BlockSpec((None, l, HEAD_DIM), lambda bi, h, i: (bi, 0, HEADS + h)),
                  pl.BlockSpec((None, l, HEAD_DIM), lambda bi, h, i: (bi, 0, 2 * HEADS + h)),
                  tile, tile,
                  pl.BlockSpec((None, 1, LANES), lambda bi, h, i: (h, 0, 0))],
        out_specs=pl.BlockSpec((None, blk, HEAD_DIM), lambda bi, h, i: (bi, i, h)),
        out_shape=jax.ShapeDtypeStruct((b, l, HEADS * HEAD_DIM), F32),
        scratch_shapes=[pltpu.VMEM((LANES, HEAD_DIM), F32)],
        name="moba_prompt",
        compiler_params=_cparams("parallel", "parallel", "arbitrary"),
    )(proj, proj, proj, bias_diag, bias_prev, far)


def _block_diag_rows(x, lq):
    return jnp.concatenate([x[h * lq:(h + 1) * lq, h * HEAD_DIM:(h + 1) * HEAD_DIM] for h in range(HEADS)], axis=0)


def _moba_sample_kernel(pt_ref, k0_ref, k1_ref, v0_ref, v1_ref, qkv_ref, bl_ref, bc_ref, bf_ref, o_ref,
                        qbd_ref, m_ref, l_ref, acc_ref, means_ref, *, nb, lq):
    n = pl.program_id(1)
    d_model = HEADS * HEAD_DIM
    hq = HEADS * lq

    @pl.when(n == 0)
    def _():
        q = qkv_ref[:, 0:d_model] * (HEAD_DIM ** -0.5)
        rows = lax.broadcasted_iota(jnp.int32, (d_model, hq), 0) // HEAD_DIM
        cols = lax.broadcasted_iota(jnp.int32, (d_model, hq), 1)
        spread = (lax.broadcasted_iota(jnp.int32, (lq, hq), 1) % lq
                  == lax.broadcasted_iota(jnp.int32, (lq, hq), 0)).astype(F32)
        qt = lax.dot_general(q, spread, (((0,), (0,)), ((), ())),
                             preferred_element_type=F32, precision=HI)
        qbd_ref[...] = jnp.where(rows == cols // lq, qt, 0.0)
        means_ref[...] = jnp.zeros_like(means_ref)
        m_ref[...] = jnp.zeros_like(m_ref)
        l_ref[...] = jnp.zeros_like(l_ref)

    qbd = qbd_ref[...]

    def rows_by_lanes(*pages):
        return jnp.concatenate(
            [jnp.concatenate([pg[:, h, :] for h in range(HEADS)], axis=1) for pg in pages], axis=0)

    kblk = rows_by_lanes(k0_ref, k1_ref)
    vblk = rows_by_lanes(v0_ref, v1_ref)
    means_ref[pl.ds(n, 1), :] = jnp.mean(kblk, axis=0, keepdims=True)
    bias = jnp.where(n == nb - 1, bl_ref[...], bf_ref[...])
    s = _bdot(kblk, qbd) + bias
    m = jnp.max(s, axis=0, keepdims=True)
    p = jnp.exp(s - m)
    m_ref[pl.ds(n, 1), :] = m
    l_ref[pl.ds(n, 1), :] = jnp.sum(p, axis=0, keepdims=True)
    acc_ref[n] = _block_diag_rows(_bdot_tn(p, vblk), lq)

    @pl.when(n == nb - 1)
    def _():
        k_new = qkv_ref[:, d_model:2 * d_model]
        v_new = qkv_ref[:, 2 * d_model:3 * d_model]
        jq = lax.broadcasted_iota(jnp.int32, (lq, hq), 1) % lq
        jk = lax.broadcasted_iota(jnp.int32, (lq, hq), 0)
        sc = jnp.where(jk <= jq, _bdot(k_new, qbd) + bc_ref[...], NEG)
        mc = jnp.max(sc, axis=0, keepdims=True)
        pc = jnp.exp(sc - mc)
        lc = jnp.sum(pc, axis=0, keepdims=True)
        oc = _block_diag_rows(_bdot_tn(pc, v_new), lq)

        gate = _fdot(means_ref[...], qbd)
        sel = _top_mask(gate, nb, MOBA_TOPK, 0)
        m_all = m_ref[...]
        m_tot = jnp.maximum(jnp.max(jnp.where(sel > 0.0, m_all, NEG), axis=0, keepdims=True), mc)
        w = jnp.where(sel > 0.0, jnp.exp(jnp.where(sel > 0.0, m_all - m_tot, 0.0)), 0.0)
        wc = jnp.exp(mc - m_tot)
        denom = jnp.sum(w * l_ref[...], axis=0, keepdims=True) + wc * lc
        nbp = w.shape[0]
        w_all = jnp.concatenate([w, wc, jnp.zeros((SUBLANES - 1, hq), F32)], axis=0) / denom
        ne = nbp + SUBLANES
        eye = (lax.broadcasted_iota(jnp.int32, (ne, ne), 0)
               == lax.broadcasted_iota(jnp.int32, (ne, ne), 1)).astype(F32)
        wt = lax.dot_general(w_all, eye, (((0,), (0,)), ((), ())),
                             preferred_element_type=F32, precision=HI)
        out = wt[:, nbp:nbp + 1] * oc
        for b_i in range(nb):
            out = out + wt[:, b_i:b_i + 1] * acc_ref[b_i]
        o_ref[...] = out


def _moba_sample(proj, cache_k, cache_v, layer, page_table, rel_bias):
    b, lq, _ = proj.shape
    n_pages = page_table.shape[1]
    d_model = HEADS * HEAD_DIM
    pages_per_block = MOBA_BLOCK // PAGE_SIZE
    assert pages_per_block == 2 and n_pages % pages_per_block == 0 and lq <= MOBA_BLOCK
    assert cache_k.shape[2:] == (PAGE_SIZE, HEADS, HEAD_DIM)
    nb = n_pages // pages_per_block
    assert nb >= 1 and MOBA_BLOCK >= REL_MAX_DIST
    nbp = _round_up(nb, SUBLANES)
    hq = HEADS * lq
    jq = jnp.arange(lq, dtype=jnp.int32)
    t = jnp.arange(MOBA_BLOCK, dtype=jnp.int32)
    by_cols = lambda tab: jnp.transpose(tab, (1, 0, 2)).reshape(tab.shape[1], hq)
    bias_last = by_cols(_bias_of_distance(rel_bias, (MOBA_BLOCK + jq)[None, :] - t[:, None]))
    bias_cur = by_cols(_bias_of_distance(rel_bias, jq[None, :] - jq[:, None]))
    bias_far = jnp.repeat(rel_bias[REL_BUCKETS - 1].astype(F32), lq)[None, :]

    page = lambda off: pl.BlockSpec((None, None, PAGE_SIZE, HEADS, HEAD_DIM),
                                    lambda bi, n, pt: (layer, pt[bi, pages_per_block * n + off], 0, 0, 0))
    full = lambda shape: pl.BlockSpec(shape, lambda bi, n, pt: (0,) * len(shape))
    grid_spec = pltpu.PrefetchScalarGridSpec(
        num_scalar_prefetch=1,
        grid=(b, nb),
        in_specs=[page(0), page(1), page(0), page(1),
                  pl.BlockSpec((None, lq, 3 * d_model), lambda bi, n, pt: (bi, 0, 0)),
                  full((MOBA_BLOCK, hq)), full((lq, hq)), full((1, hq))],
        out_specs=pl.BlockSpec((None, hq, HEAD_DIM), lambda bi, n, pt: (bi, 0, 0)),
        scratch_shapes=[pltpu.VMEM((d_model, hq), F32),
                        pltpu.VMEM((nbp, hq), F32),
                        pltpu.VMEM((nbp, hq), F32),
                        pltpu.VMEM((nb, hq, HEAD_DIM), F32),
                        pltpu.VMEM((nbp, d_model), F32)])
    return pl.pallas_call(
        functools.partial(_moba_sample_kernel, nb=nb, lq=lq),
        grid_spec=grid_spec,
        out_shape=jax.ShapeDtypeStruct((b, hq, HEAD_DIM), F32),
        name="moba_sample",
        compiler_params=_cparams("parallel", "arbitrary"),
    )(page_table, cache_k, cache_k, cache_v, cache_v, proj, bias_last, bias_cur, bias_far)


def _trunk(x, mods, s_hgrn, s_conv, s_gdn, paged, W):
    b, l, d = x.shape
    tiling = _row_tile(b, l)
    x2 = x.reshape(b * l, d)
    new_h, new_c, new_g, new_k, new_v = [], [], [], [], []
    for i in range(DEPTH):
        mod = _mod_layout(mods[i], b, l)
        x2 = _ffn_block(x2, mod, tiling, 0, W['ffn1_up'], W['ffn1_down'], i, W['ln_g'][i, 0], W['ln_b'][i, 0])
        kind, li = i % N_MIXERS, i // N_MIXERS
        if kind == 0:
            proj = _proj_in(x2, mod, tiling, 1, W['hgrn_w_in'][li], 1024)
            o, s = _hgrn_recurrence(proj.reshape(b, l, -1), s_hgrn[li], W['hgrn_lb'][li], W['hgrn_norm_g'][li])
            new_h.append(s)
            w_o = W['hgrn_w_o'][li]
            o2 = o.reshape(b * l, d)
        elif kind == 1:
            proj = _proj_in(x2, mod, tiling, 1, W['gdn_w_in'][li], W['gdn_w_in'][li].shape[1] // 3)
            proj = proj.reshape(b, l, -1)
            o, s = _gdn_recurrence(proj, s_conv[li], s_gdn[li], W['gdn_conv_w'][li], W['gdn_a_log'][li],
                                   W['gdn_dt_bias'][li], W['gdn_norm_g'][li])
            n_qkv = 3 * HEADS * HEAD_DIM
            assert l >= CONV_W - 1
            new_c.append(proj[:, l - (CONV_W - 1):, :n_qkv])
            new_g.append(s)
            w_o = W['gdn_w_o'][li]
            o2 = o.reshape(b * l, d)
        else:
            proj = _proj_in(x2, mod, tiling, 1, W['moba_w_qkv'][li], 1024).reshape(b, l, -1)
            if paged is None:
                o2 = _moba_prompt(proj, W['rel_bias']).reshape(b * l, d)
            else:
                ck, cv, pt = paged
                o = _moba_sample(proj, ck, cv, li, pt, W['rel_bias'])
                o2 = jnp.transpose(o.reshape(b, HEADS, l, HEAD_DIM), (0, 2, 1, 3)).reshape(b * l, d)
            new_k.append(proj[:, :, d:2 * d].reshape(b, l, HEADS, HEAD_DIM))
            new_v.append(proj[:, :, 2 * d:3 * d].reshape(b, l, HEADS, HEAD_DIM))
            w_o = W['moba_w_o'][li]
        x2 = _proj_out(o2, x2, mod, tiling, 1, w_o, W['ln_g'][i, 1], W['ln_b'][i, 1])
        x2 = _ffn_block(x2, mod, tiling, 2, W['ffn2_up'], W['ffn2_down'], i, W['ln_g'][i, 2], W['ln_b'][i, 2])
    return (x2.reshape(b, l, d), jnp.stack(new_h), jnp.stack(new_c), jnp.stack(new_g),
            jnp.stack(new_k), jnp.stack(new_v))


def kernel(x_prompt, x_sample, state_hgrn, state_gdn_conv, state_gdn, cache_k, cache_v, page_table, c_prompt, c_sample, w_ada, b_ada, ln_g, ln_b, w_ffn1_up, w_ffn1_down, w_ffn2_up, w_ffn2_down, hgrn_w_in, hgrn_lb_logits, hgrn_norm_g, hgrn_w_o, gdn_w_in, gdn_conv_w, gdn_a_log, gdn_dt_bias, gdn_norm_g, gdn_w_o, moba_w_qkv, moba_w_o, rel_bias):
    bp, _, d = x_prompt.shape
    bs = x_sample.shape[0]
    assert d == HEADS * HEAD_DIM

    nc = _round_up(bp + bs, SUBLANES)
    c_all = jnp.pad(jnp.concatenate([c_prompt, c_sample], axis=0), ((0, nc - bp - bs), (0, 0)))
    mods = _ada_mods(c_all, w_ada, b_ada).reshape(DEPTH, nc, 9, d)

    ffn1_up, ffn1_down = _prep_ffn_weights(w_ffn1_up, w_ffn1_down)
    ffn2_up, ffn2_down = _prep_ffn_weights(w_ffn2_up, w_ffn2_down)
    gdn_n = _round_up(gdn_w_in.shape[2], 3 * LANES)
    lbp = jax.nn.softmax(hgrn_lb_logits.astype(F32), axis=0)
    W = {
        'ffn1_up': ffn1_up, 'ffn1_down': ffn1_down, 'ffn2_up': ffn2_up, 'ffn2_down': ffn2_down,
        'ln_g': ln_g, 'ln_b': ln_b,
        'hgrn_w_in': hgrn_w_in.astype(BF16), 'hgrn_lb': jnp.cumsum(lbp, axis=0) - lbp[0],
        'hgrn_norm_g': hgrn_norm_g, 'hgrn_w_o': hgrn_w_o.astype(BF16),
        'gdn_w_in': jnp.pad(gdn_w_in, ((0, 0), (0, 0), (0, gdn_n - gdn_w_in.shape[2]))).astype(BF16),
        'gdn_conv_w': gdn_conv_w, 'gdn_a_log': gdn_a_log, 'gdn_dt_bias': gdn_dt_bias,
        'gdn_norm_g': gdn_norm_g, 'gdn_w_o': gdn_w_o.astype(BF16),
        'moba_w_qkv': moba_w_qkv.astype(BF16), 'moba_w_o': moba_w_o.astype(BF16), 'rel_bias': rel_bias,
    }

    n_a, n_b = state_hgrn.shape[0], state_gdn.shape[0]
    z_hgrn = jnp.zeros((n_a, bp) + state_hgrn.shape[2:], F32)
    z_conv = jnp.zeros((n_b, bp) + state_gdn_conv.shape[2:], F32)
    z_gdn = jnp.zeros((n_b, bp) + state_gdn.shape[2:], F32)
    y_p, hgrn_p, conv_p, gdn_p, k_p, v_p = _trunk(x_prompt, mods[:, :bp], z_hgrn, z_conv, z_gdn, None, W)

    y_s, hgrn_s, conv_s, gdn_s, k_s, v_s = _trunk(x_sample, mods[:, bp:bp + bs], state_hgrn, state_gdn_conv,
                                                  state_gdn, (cache_k, cache_v, page_table), W)
    return (y_p, y_s, hgrn_p, hgrn_s, conv_p, conv_s, gdn_p, gdn_s, k_p, k_s, v_p, v_s)
```

```python
import functools
import math

import jax
import jax.numpy as jnp
from jax import lax
from jax.experimental import pallas as pl
from jax.experimental.pallas import tpu as pltpu

F32 = jnp.float32
BF16 = jnp.bfloat16
HI = lax.Precision.HIGHEST

DEPTH = 4
N_MIXERS = 3
HEADS = 8
HEAD_DIM = 128
CONV_W = 4
CHUNK = 64
MOBA_BLOCK = 256
MOBA_TOPK = 3
REL_BUCKETS = 32
REL_MAX_DIST = 128
PAGE_SIZE = 128
DN_ALPHA = (2 * DEPTH) ** 0.25
LN_EPS = 1e-5
NEG = -1e30

LANES = 128
SUBLANES = 8
MXU_WIDTH = 256
VMEM_LIMIT_BYTES = 56 * 1024 * 1024


def _cparams(*sem):
    return pltpu.CompilerParams(dimension_semantics=sem, vmem_limit_bytes=VMEM_LIMIT_BYTES)


def _round_up(n, m):
    return (n + m - 1) // m * m


def _bdot(a, b):
    return jnp.dot(a.astype(BF16), b.astype(BF16), preferred_element_type=F32)


def _bdot_nt(a, b):
    return lax.dot_general(a.astype(BF16), b.astype(BF16), (((1,), (1,)), ((), ())),
                           preferred_element_type=F32)


def _bdot_tn(a, b):
    return lax.dot_general(a.astype(BF16), b.astype(BF16), (((0,), (0,)), ((), ())),
                           preferred_element_type=F32)


def _fdot(a, b):
    return jnp.dot(a, b, preferred_element_type=F32, precision=HI)


def _block_tril3(r, c):
    row = lax.broadcasted_iota(jnp.int32, (r, 3 * r), 0)
    col = lax.broadcasted_iota(jnp.int32, (r, 3 * r), 1) % r
    return jnp.where(((row // c) == (col // c)) & (row >= col), 1.0, 0.0).astype(BF16)


def _prefix_dot(sel3, x):
    hi = x.astype(BF16)
    rest = x - hi.astype(F32)
    mid = rest.astype(BF16)
    lo = (rest - mid.astype(F32)).astype(BF16)
    return jnp.dot(sel3, jnp.concatenate([hi, mid, lo], axis=0), preferred_element_type=F32)


def _sigmoid(x):
    return 1.0 / (1.0 + jnp.exp(-x))


def _silu(x):
    return x * _sigmoid(x)


def _layer_norm_rows(z, g, b):
    mu = jnp.mean(z, axis=-1, keepdims=True)
    zc = z - mu
    var = jnp.mean(zc * zc, axis=-1, keepdims=True)
    return zc * lax.rsqrt(var + LN_EPS) * g + b


def _ada_kernel(c_ref, w_ref, b_ref, o_ref):
    o_ref[...] = _bdot(_silu(c_ref[...]), w_ref[...]) + b_ref[...]


def _ada_mods(c_all, w_ada, b_ada):
    depth, d, n = w_ada.shape
    nb = c_all.shape[0]
    tn = n // 8
    assert n % 8 == 0 and tn % LANES == 0
    return pl.pallas_call(
        _ada_kernel,
        grid=(depth, n // tn),
        in_specs=[pl.BlockSpec((nb, d), lambda l, j: (0, 0)),
                  pl.BlockSpec((None, d, tn), lambda l, j: (l, 0, j)),
                  pl.BlockSpec((None, 1, tn), lambda l, j: (l, 0, j))],
        out_specs=pl.BlockSpec((None, nb, tn), lambda l, j: (l, 0, j)),
        out_shape=jax.ShapeDtypeStruct((depth, nb, n), F32),
        name="ada_mods",
        compiler_params=_cparams("parallel", "parallel"),
    )(c_all, w_ada, b_ada.reshape(depth, 1, n))


MAX_ROW_TILE = 1024


def _row_tile(batch, length):
    if length >= MXU_WIDTH:
        tm = MAX_ROW_TILE
        while length % tm:
            tm //= 2
        return tm, length // tm, True
    rows = batch * length
    assert rows % SUBLANES == 0 and rows <= MAX_ROW_TILE
    return rows, 1, False


def _mod_layout(m, batch, length):
    tm, _, per_batch = _row_tile(batch, length)
    mt = jnp.transpose(m, (1, 0, 2))
    if per_batch:
        return mt[:, :, None, :]
    return jnp.repeat(mt, length, axis=1)[:, None, :, :]


def _mod_spec(mod, tiles_per_block):
    _, _, rm, d = mod.shape
    return pl.BlockSpec((9, None, rm, d), lambda i, *_: (0, i // tiles_per_block, 0, 0))


FF_CHUNK = MXU_WIDTH


def _ffn_kernel(x_ref, mod_ref, wa_ref, wu_ref, wd_ref, g_ref, b_ref, o_ref, h_ref, *, j, n_chunks):
    x = x_ref[...]
    h_ref[...] = (x * (1.0 + mod_ref[3 * j + 1]) + mod_ref[3 * j]).astype(BF16)
    o_ref[...] = jnp.zeros_like(o_ref)

    def body(c, carry):
        c0 = pl.multiple_of(c * FF_CHUNK, FF_CHUNK)
        h = h_ref[...]
        a = jnp.dot(h, wa_ref[:, pl.ds(c0, FF_CHUNK)], preferred_element_type=F32)
        u = jnp.dot(h, wu_ref[:, pl.ds(c0, FF_CHUNK)], preferred_element_type=F32)
        act = (_silu(a) * u).astype(BF16)
        o_ref[...] += jnp.dot(act, wd_ref[pl.ds(c0, FF_CHUNK), :], preferred_element_type=F32)
        return carry

    lax.fori_loop(0, n_chunks, body, 0)
    z = DN_ALPHA * x + (1.0 + mod_ref[3 * j + 2]) * (0.5 * o_ref[...])
    o_ref[...] = _layer_norm_rows(z, g_ref[...], b_ref[...])


def _ffn_block(x2d, mod, tiling, j, w_up, w_down, layer, ln_g, ln_b):
    tm, tpb, _ = tiling
    rows, d = x2d.shape
    fp = w_up.shape[-1]
    kern = functools.partial(_ffn_kernel, j=j, n_chunks=fp // FF_CHUNK)
    return pl.pallas_call(
        kern,
        grid=(rows // tm,),
        in_specs=[pl.BlockSpec((tm, d), lambda i: (i, 0)),
                  _mod_spec(mod, tpb),
                  pl.BlockSpec((None, None, d, fp), lambda i: (layer, 0, 0, 0), pipeline_mode=pl.Buffered(1)),
                  pl.BlockSpec((None, None, d, fp), lambda i: (layer, 1, 0, 0), pipeline_mode=pl.Buffered(1)),
                  pl.BlockSpec((None, fp, d), lambda i: (layer, 0, 0), pipeline_mode=pl.Buffered(1)),
                  pl.BlockSpec((1, d), lambda i: (0, 0)),
                  pl.BlockSpec((1, d), lambda i: (0, 0))],
        out_specs=pl.BlockSpec((tm, d), lambda i: (i, 0)),
        out_shape=jax.ShapeDtypeStruct((rows, d), F32),
        scratch_shapes=[pltpu.VMEM((tm, d), BF16)],
        name="ffn_block",
        compiler_params=_cparams("parallel"),
    )(x2d, mod, w_up, w_up, w_down, ln_g.reshape(1, d), ln_b.reshape(1, d))


def _prep_ffn_weights(w_up, w_down):
    depth, d, two_ff = w_up.shape
    d_ff = two_ff // 2
    fp = _round_up(d_ff, FF_CHUNK)
    up = jnp.transpose(w_up.reshape(depth, d, 2, d_ff), (0, 2, 1, 3))
    up = jnp.pad(up, ((0, 0), (0, 0), (0, 0), (0, fp - d_ff))).astype(BF16)
    down = jnp.pad(w_down, ((0, 0), (0, fp - d_ff), (0, 0))).astype(BF16)
    return up, down


def _proj_in_kernel(x_ref, mod_ref, w_ref, o_ref, h_ref, *, j):
    @pl.when(pl.program_id(1) == 0)
    def _():
        h_ref[...] = (x_ref[...] * (1.0 + mod_ref[3 * j + 1]) + mod_ref[3 * j]).astype(BF16)

    o_ref[...] = jnp.dot(h_ref[...], w_ref[...], preferred_element_type=F32)


def _proj_in(x2d, mod, tiling, j, w, tn):
    tm, tpb, _ = tiling
    rows, d = x2d.shape
    n = w.shape[1]
    assert n % tn == 0
    return pl.pallas_call(
        functools.partial(_proj_in_kernel, j=j),
        grid=(rows // tm, n // tn),
        in_specs=[pl.BlockSpec((tm, d), lambda i, k: (i, 0)),
                  _mod_spec(mod, tpb),
                  pl.BlockSpec((d, tn), lambda i, k: (0, k))],
        out_specs=pl.BlockSpec((tm, tn), lambda i, k: (i, k)),
        out_shape=jax.ShapeDtypeStruct((rows, n), F32),
        scratch_shapes=[pltpu.VMEM((tm, d), BF16)],
        name="proj_in",
        compiler_params=_cparams("parallel", "arbitrary"),
    )(x2d, mod, w)


def _proj_out_kernel(o_ref, x_ref, mod_ref, w_ref, g_ref, b_ref, y_ref, *, j):
    y = jnp.dot(o_ref[...].astype(BF16), w_ref[...], preferred_element_type=F32)
    z = DN_ALPHA * x_ref[...] + (1.0 + mod_ref[3 * j + 2]) * y
    y_ref[...] = _layer_norm_rows(z, g_ref[...], b_ref[...])


def _proj_out(o2d, x2d, mod, tiling, j, w, ln_g, ln_b):
    tm, tpb, _ = tiling
    rows, d = x2d.shape
    k = o2d.shape[1]
    return pl.pallas_call(
        functools.partial(_proj_out_kernel, j=j),
        grid=(rows // tm,),
        in_specs=[pl.BlockSpec((tm, k), lambda i: (i, 0)),
                  pl.BlockSpec((tm, d), lambda i: (i, 0)),
                  _mod_spec(mod, tpb),
                  pl.BlockSpec((k, d), lambda i: (0, 0)),
                  pl.BlockSpec((1, d), lambda i: (0, 0)),
                  pl.BlockSpec((1, d), lambda i: (0, 0))],
        out_specs=pl.BlockSpec((tm, d), lambda i: (i, 0)),
        out_shape=jax.ShapeDtypeStruct((rows, d), F32),
        name="proj_out",
        compiler_params=_cparams("parallel"),
    )(o2d, x2d, mod, w, ln_g.reshape(1, d), ln_b.reshape(1, d))


def _tri(c, strict=False):
    r = lax.broadcasted_iota(jnp.int32, (c, c), 0)
    s = lax.broadcasted_iota(jnp.int32, (c, c), 1)
    return (r > s) if strict else (r >= s)


def _hgrn_diag_masks(c):
    row = lax.broadcasted_iota(jnp.int32, (c, c), 0)
    col = lax.broadcasted_iota(jnp.int32, (c, c), 1)
    return [(col == row - d) & ((row % SUBLANES) >= d) for d in range(SUBLANES)]


def _hgrn_chunk(q, k, cum, v, st_prev, c, diag_masks):
    nb = c // SUBLANES
    last_row = cum[c - 1:c, :]

    o = _bdot_nt(q * jnp.exp(cum), st_prev)

    q3 = q.reshape(nb, SUBLANES, LANES)
    k3 = k.reshape(nb, SUBLANES, LANES)
    cum3 = cum.reshape(nb, SUBLANES, LANES)

    att = jnp.zeros((c, c), F32)
    for d in reversed(range(SUBLANES)):
        if d == 0:
            p = q3 * k3
        else:
            p = q3 * pltpu.roll(k3, d, 1) * jnp.exp(cum3 - pltpu.roll(cum3, d, 1))
        a_d = jnp.sum(p.reshape(c, LANES), axis=-1, keepdims=True)
        att = jnp.where(diag_masks[d], a_d, att)

    if nb > 1:
        e = cum3[:, SUBLANES - 1:SUBLANES, :]
        k_t = k3 * jnp.exp(e - cum3)
        blk = lax.broadcasted_iota(jnp.int32, (nb, SUBLANES, LANES), 0)
        q_parts, k_parts = [], []
        for jb in range(nb - 1):
            qj = q3[jb + 1:] * jnp.exp(cum3[jb + 1:] - e[jb:jb + 1])
            qj = jnp.concatenate([jnp.zeros((jb + 1, SUBLANES, LANES), F32), qj], axis=0)
            q_parts.append(qj.reshape(c, LANES).astype(BF16))
            k_parts.append(jnp.where(blk == jb, k_t, 0.0).reshape(c, LANES).astype(BF16))
        att = att + _bdot_nt(jnp.concatenate(q_parts, axis=1), jnp.concatenate(k_parts, axis=1))

    o = o + _bdot(att, v)
    st_new = st_prev * jnp.exp(last_row) + _bdot_tn(v, k * jnp.exp(last_row - cum))
    return o, st_new


REC_L_TILE = 512
HGRN_HEADS_PER_STEP = 8
GDN_HEADS_PER_STEP = 8
GDN_STACK = 4


def _head_lanes(hh):
    return slice(hh * HEAD_DIM, (hh + 1) * HEAD_DIM)


def _rec_tiling(l):
    c = min(CHUNK, l)
    tl = min(REC_L_TILE, l)
    assert l % tl == 0 and tl % c == 0 and c % SUBLANES == 0
    return c, tl


def _hgrn_kernel(q_ref, f_ref, v_ref, g_ref, lb_ref, ng_ref, s0_ref, o_ref, s_ref, *, c, n_chunks, hb):
    @pl.when(pl.program_id(2) == 0)
    def _():
        for hh in range(hb):
            s_ref[hh] = s0_ref[hh].T

    ng = ng_ref[...]
    lb = lb_ref[...]
    tril3 = _block_tril3(c, c)
    diag_masks = _hgrn_diag_masks(c)

    def body(i, carry):
        rows = pl.ds(pl.multiple_of(i * c, c), c)
        fr = f_ref[rows, :]
        q_all = _silu(q_ref[rows, :]) * (HEAD_DIM ** -0.5)
        k_all = (1.0 - lb) * _sigmoid(-fr)
        cum_all = _prefix_dot(tril3, jnp.log(lb + (1.0 - lb) * _sigmoid(fr)))
        for hh in range(hb):
            ln = _head_lanes(hh)
            o, st_new = _hgrn_chunk(q_all[:, ln], k_all[:, ln], cum_all[:, ln], v_ref[rows, ln], s_ref[hh], c,
                                    diag_masks)
            s_ref[hh] = st_new
            ms = jnp.mean(o * o, axis=-1, keepdims=True)
            o_ref[rows, ln] = o * lax.rsqrt(ms + 1e-6) * ng * _sigmoid(g_ref[rows, ln])
        return carry

    lax.fori_loop(0, n_chunks, body, 0)

    @pl.when(pl.program_id(2) == pl.num_programs(2) - 1)
    def _():
        for hh in range(hb):
            s_ref[hh] = s_ref[hh].T


def _hgrn_recurrence(proj, s0, lb, norm_g):
    b, l, _ = proj.shape
    c, tl = _rec_tiling(l)
    hb = HGRN_HEADS_PER_STEP
    ng_ = HEADS // hb
    col = lambda off: pl.BlockSpec((None, tl, hb * HEAD_DIM), lambda bi, hg, t: (bi, t, off * ng_ + hg))
    state = pl.BlockSpec((None, hb, HEAD_DIM, HEAD_DIM), lambda bi, hg, t: (bi, hg, 0, 0))
    return pl.pallas_call(
        functools.partial(_hgrn_kernel, c=c, n_chunks=tl // c, hb=hb),
        grid=(b, ng_, l // tl),
        in_specs=[col(0), col(1), col(2), col(3),
                  pl.BlockSpec((1, hb * HEAD_DIM), lambda bi, hg, t: (0, hg)),
                  pl.BlockSpec((1, HEAD_DIM), lambda bi, hg, t: (0, 0)),
                  state],
        out_specs=[col(0), state],
        out_shape=[jax.ShapeDtypeStruct((b, l, HEADS * HEAD_DIM), F32),
                   jax.ShapeDtypeStruct((b, HEADS, HEAD_DIM, HEAD_DIM), F32)],
        name="hgrn_recurrence",
        compiler_params=_cparams("parallel", "parallel", "arbitrary"),
    )(proj, proj, proj, proj, lb.reshape(1, -1), norm_g.reshape(1, -1), s0)


HIST = SUBLANES


def _gdn_conv(x_ext, cw, c):
    acc = x_ext * cw[CONV_W - 1:CONV_W, :]
    for w in range(CONV_W - 1):
        acc = acc + pltpu.roll(x_ext, CONV_W - 1 - w, 0) * cw[w:w + 1, :]
    return _silu(acc[HIST:, :])


def _l2norm_rows(x):
    return x * lax.rsqrt(jnp.sum(x * x, axis=-1, keepdims=True) + 1e-6)


def _gdn_masks(r, c):
    row = lax.broadcasted_iota(jnp.int32, (r, r), 0)
    col = lax.broadcasted_iota(jnp.int32, (r, r), 1)
    same = (row // c) == (col // c)
    low = same & (row >= col)
    return low, same & (row > col), jnp.where(row == col, 1.0, 0.0), _block_tril3(r, c)


def _gdn_chunk(qs, ks, vs, betas, gs, s_prev, c, masks):
    n = len(s_prev)
    r = n * c
    low, strict, eye, tril3 = masks
    pre = _prefix_dot(tril3, jnp.concatenate(
        [jnp.where(strict, jnp.broadcast_to(gs, (r, r)), 0.0), jnp.broadcast_to(gs, (r, LANES))], axis=1))
    dm, cum = pre[:, :r], pre[:, r:]
    decay = jnp.where(low, jnp.exp(jnp.where(low, dm, 0.0)), 0.0)
    ecum = jnp.exp(cum)
    kb = ks * betas
    kk = _bdot_nt(jnp.concatenate([kb, qs], axis=0), ks)
    p = -jnp.where(strict, kk[:r] * decay, 0.0)
    att = kk[r:] * decay
    t_inv = eye + p
    x = p
    for _ in range(int(math.log2(c)) - 1):
        x = _bdot(x, x)
        t_inv = t_inv + _bdot(x, t_inv)
    uw = _bdot(t_inv, jnp.concatenate([vs * betas, kb * ecum], axis=1))
    u, w = uw[:, :HEAD_DIM], uw[:, HEAD_DIM:]
    qe = qs * ecum
    o_state, v_new, lasts = [], [], []
    for h in range(n):
        rows = slice(h * c, (h + 1) * c)
        ws = _bdot(jnp.concatenate([w[rows], qe[rows]], axis=0), s_prev[h])
        v_new.append(u[rows] - ws[:c])
        o_state.append(ws[c:])
        lasts.append(cum[(h + 1) * c - 1:(h + 1) * c, :])
    v_new = jnp.concatenate(v_new, axis=0)
    o = jnp.concatenate(o_state, axis=0) + _bdot(att, v_new)
    last_rows = jnp.concatenate([jnp.broadcast_to(l, (c, LANES)) for l in lasts], axis=0)
    kt = ks * jnp.exp(last_rows - cum)
    s_new = []
    for h in range(n):
        rows = slice(h * c, (h + 1) * c)
        s_new.append(s_prev[h] * jnp.exp(lasts[h]) + _bdot_tn(kt[rows], v_new[rows]))
    return o, s_new


def _gdn_kernel(q_ref, k_ref, v_ref, z_ref, ba_ref, c0q_ref, c0k_ref, c0v_ref, cwq_ref, cwk_ref, cwv_ref,
                an_ref, dt_ref, ng_ref, s0_ref, o_ref, s_ref, qx_ref, kx_ref, vx_ref, *, c, n_chunks, tl, hb):
    hg = pl.program_id(1)
    t = pl.program_id(2)
    staged = ((q_ref, c0q_ref, qx_ref), (k_ref, c0k_ref, kx_ref), (v_ref, c0v_ref, vx_ref))

    @pl.when(t == 0)
    def _():
        s_ref[...] = s0_ref[...]
        for _, c0, dst in staged:
            dst[0:HIST, :] = jnp.zeros((HIST, hb * HEAD_DIM), F32)
            dst[HIST - (CONV_W - 1):HIST, :] = c0[...]

    @pl.when(t > 0)
    def _():
        for _, _, dst in staged:
            dst[0:HIST, :] = dst[tl:tl + HIST, :]

    for src, _, dst in staged:
        dst[HIST:HIST + tl, :] = src[...]
    ng = ng_ref[...]
    lane = lax.broadcasted_iota(jnp.int32, (c, LANES), 1)
    masks = _gdn_masks(GDN_STACK * c, c)

    def body(i, carry):
        r0 = pl.multiple_of(i * c, c)
        ext = pl.ds(r0, c + HIST)
        rows = pl.ds(r0, c)
        ba = ba_ref[rows, :]
        for h0 in range(0, hb, GDN_STACK):
            heads = range(h0, h0 + GDN_STACK)
            qs, ks, vs, betas, gs = [], [], [], [], []
            for hh in heads:
                ln = _head_lanes(hh)
                head = hg * hb + hh
                qs.append(_l2norm_rows(_gdn_conv(qx_ref[ext, ln], cwq_ref[:, ln], c)) * (HEAD_DIM ** -0.5))
                ks.append(_l2norm_rows(_gdn_conv(kx_ref[ext, ln], cwk_ref[:, ln], c)))
                vs.append(_gdn_conv(vx_ref[ext, ln], cwv_ref[:, ln], c))
                betas.append(_sigmoid(jnp.sum(jnp.where(lane == head, ba, 0.0), axis=-1, keepdims=True)))
                a_raw = jnp.sum(jnp.where(lane == HEADS + head, ba, 0.0), axis=-1, keepdims=True)
                xa = a_raw + dt_ref[:, ln][:, 0:1]
                softplus = jnp.maximum(xa, 0.0) + jnp.log(1.0 + jnp.exp(-jnp.abs(xa)))
                gs.append(an_ref[:, ln][:, 0:1] * softplus)
            stack = lambda parts: jnp.concatenate(parts, axis=0)
            o, s_new = _gdn_chunk(stack(qs), stack(ks), stack(vs), stack(betas), stack(gs),
                                  [s_ref[hh] for hh in heads], c, masks)
            for j, hh in enumerate(heads):
                ln = _head_lanes(hh)
                s_ref[hh] = s_new[j]
                oh = o[j * c:(j + 1) * c]
                ms = jnp.mean(oh * oh, axis=-1, keepdims=True)
                o_ref[rows, ln] = oh * lax.rsqrt(ms + 1e-6) * ng * _silu(z_ref[rows, ln])
        return carry

    lax.fori_loop(0, n_chunks, body, 0)


def _gdn_recurrence(proj, conv0, s0, conv_w, a_log, dt_bias, norm_g):
    b, l, _ = proj.shape
    c, tl = _rec_tiling(l)
    hb = GDN_HEADS_PER_STEP
    ng_ = HEADS // hb
    w = hb * HEAD_DIM
    col = lambda off: pl.BlockSpec((None, tl, w), lambda bi, hg, t: (bi, t, off * ng_ + hg))
    c0 = lambda off: pl.BlockSpec((None, CONV_W - 1, w), lambda bi, hg, t: (bi, 0, off * ng_ + hg))
    cw = lambda off: pl.BlockSpec((CONV_W, w), lambda bi, hg, t: (0, off * ng_ + hg))
    per_head = pl.BlockSpec((1, w), lambda bi, hg, t: (0, hg))
    state = pl.BlockSpec((None, hb, HEAD_DIM, HEAD_DIM), lambda bi, hg, t: (bi, hg, 0, 0))
    a_neg = jnp.repeat(-jnp.exp(a_log.astype(F32)), HEAD_DIM)[None, :]
    dt_b = jnp.repeat(dt_bias.astype(F32), HEAD_DIM)[None, :]
    xbuf = pltpu.VMEM((HIST + tl, w), F32)
    return pl.pallas_call(
        functools.partial(_gdn_kernel, c=c, n_chunks=tl // c, tl=tl, hb=hb),
        grid=(b, ng_, l // tl),
        in_specs=[col(0), col(1), col(2), col(3),
                  pl.BlockSpec((None, tl, LANES), lambda bi, hg, t: (bi, t, 4 * HEADS)),
                  c0(0), c0(1), c0(2), cw(0), cw(1), cw(2),
                  per_head, per_head,
                  pl.BlockSpec((1, HEAD_DIM), lambda bi, hg, t: (0, 0)),
                  state],
        out_specs=[col(0), state],
        out_shape=[jax.ShapeDtypeStruct((b, l, HEADS * HEAD_DIM), F32),
                   jax.ShapeDtypeStruct((b, HEADS, HEAD_DIM, HEAD_DIM), F32)],
        scratch_shapes=[xbuf, xbuf, xbuf],
        name="gdn_recurrence",
        compiler_params=_cparams("parallel", "parallel", "arbitrary"),
    )(proj, proj, proj, proj, proj, conv0, conv0, conv0, conv_w, conv_w, conv_w,
      a_neg, dt_b, norm_g.reshape(1, -1), s0)


def _rel_bucket(dist):
    exact = REL_BUCKETS // 2
    d = jnp.maximum(dist, exact).astype(F32)
    large = exact + (jnp.log(d / exact) / math.log(REL_MAX_DIST / exact) * (REL_BUCKETS - exact)).astype(jnp.int32)
    return jnp.where(dist < exact, dist, jnp.minimum(large, REL_BUCKETS - 1))


def _top_mask(gate, n_valid, k, axis):
    idx = lax.broadcasted_iota(jnp.int32, gate.shape, axis)
    big = jnp.int32(2 ** 30)
    work = jnp.where(idx < n_valid, gate, NEG)
    sel = jnp.zeros(gate.shape, F32)
    for r in range(k):
        mx = jnp.max(work, axis=axis, keepdims=True)
        first = jnp.min(jnp.where(work == mx, idx, big), axis=axis, keepdims=True)
        pick = (idx == first) & (idx < n_valid)
        sel = jnp.where(pick, 1.0, sel)
        work = jnp.where(idx == first, -jnp.inf, work)
    return sel


MOBA_HEADS_PER_STEP = 4


def _moba_prompt_kernel(q_ref, k_ref, v_ref, bd_ref, bp_ref, bf_ref, o_ref, means_ref, *, nb, hb):
    i = pl.program_id(2)
    blk = MOBA_BLOCK

    @pl.when(i == 0)
    def _():
        means_ref[...] = jnp.zeros_like(means_ref)
        for hh in range(hb):
            for n in range(nb):
                means_ref[hh, n:n + 1, :] = jnp.mean(k_ref[n * blk:(n + 1) * blk, _head_lanes(hh)],
                                                     axis=0, keepdims=True)

    lane = lax.broadcasted_iota(jnp.int32, (blk, LANES), 1)
    causal = _tri(blk)
    r0 = pl.multiple_of(i * blk, blk)
    qbs, sels, init = [], [], []
    for hh in range(hb):
        ln = _head_lanes(hh)
        q = q_ref[:, ln] * (HEAD_DIM ** -0.5)
        gate = lax.dot_general(q, means_ref[hh], (((1,), (1,)), ((), ())),
                               preferred_element_type=F32, precision=HI)
        sels.append(_top_mask(gate, i, MOBA_TOPK, 1))
        qb = q.astype(BF16)
        qbs.append(qb)
        s = jnp.where(causal, _bdot_nt(qb, k_ref[pl.ds(r0, blk), ln]) + bd_ref[hh], NEG)
        m0 = jnp.max(s, axis=-1, keepdims=True)
        p = jnp.exp(s - m0)
        init.append((m0, jnp.sum(p, axis=-1, keepdims=True), _bdot(p, v_ref[pl.ds(r0, blk), ln])))

    def past_block(n, carry, bias_of_head):
        rn = pl.multiple_of(n * blk, blk)
        out = []
        for hh in range(hb):
            ln = _head_lanes(hh)
            m, l, acc = carry[hh]
            picked = jnp.sum(jnp.where(lane == n, sels[hh], 0.0), axis=-1, keepdims=True) > 0.0
            sn = jnp.where(picked, _bdot_nt(qbs[hh], k_ref[pl.ds(rn, blk), ln]) + bias_of_head(hh), NEG)
            m_new = jnp.maximum(m, jnp.max(sn, axis=-1, keepdims=True))
            alpha = jnp.exp(m - m_new)
            pn = jnp.exp(sn - m_new)
            out.append((m_new, l * alpha + jnp.sum(pn, axis=-1, keepdims=True),
                        acc * alpha + _bdot(pn, v_ref[pl.ds(rn, blk), ln])))
        return tuple(out)

    carry = lax.fori_loop(0, jnp.maximum(i - 1, 0),
                          lambda n, cr: past_block(n, cr, lambda hh: bf_ref[:, _head_lanes(hh)][:, 0:1]),
                          tuple(init))
    final = past_block(jnp.maximum(i - 1, 0), carry, lambda hh: bp_ref[hh])
    for hh in range(hb):
        _, l, acc = final[hh]
        o_ref[:, _head_lanes(hh)] = acc / l


def _bias_of_distance(rel_bias, dist):
    bucket = _rel_bucket(jnp.maximum(dist, 0))[None]
    out = jnp.zeros((HEADS,) + dist.shape, F32)
    for b in range(REL_BUCKETS):
        out = jnp.where(bucket == b, rel_bias[b].astype(F32).reshape((HEADS,) + (1,) * dist.ndim), out)
    return out


def _moba_bias_tables(rel_bias, blk):
    t = jnp.arange(blk, dtype=jnp.int32)
    d0 = t[:, None] - t[None, :]
    bias_diag = _bias_of_distance(rel_bias, d0)
    bias_prev = _bias_of_distance(rel_bias, d0 + blk)
    far = jnp.repeat(rel_bias[REL_BUCKETS - 1].astype(F32), HEAD_DIM)[None, :]
    return bias_diag, bias_prev, far


def _moba_prompt(proj, rel_bias):
    b, l, _ = proj.shape
    blk = MOBA_BLOCK
    assert l % blk == 0 and l // blk <= LANES
    assert 2 * blk > REL_MAX_DIST
    nb = l // blk
    bias_diag, bias_prev, far = _moba_bias_tables(rel_bias, blk)
    hb = MOBA_HEADS_PER_STEP
    ng_ = HEADS // hb
    w = hb * HEAD_DIM
    tile = pl.BlockSpec((hb, blk, blk), lambda bi, hg, i: (hg, 0, 0))
    return pl.pallas_call(
        functools.partial(_moba_prompt_kernel, nb=nb, hb=hb),
        grid=(b, ng_, nb),
        in_specs=[pl.BlockSpec((None, blk, w), lambda bi, hg, i: (bi, i, hg)),
                  pl.BlockSpec((None, l, w), lambda bi, hg, i: (bi, 0, ng_ + hg)),
                  pl.BlockSpec((None, l, w), lambda bi, hg, i: (bi, 0, 2 * ng_ + hg)),
                  tile, tile,
                  pl.BlockSpec((1, w), lambda bi, hg, i: (0, hg))],
        out_specs=pl.BlockSpec((None, blk, w), lambda bi, hg, i: (bi, i, hg)),
        out_shape=jax.ShapeDtypeStruct((b, l, HEADS * HEAD_DIM), F32),
        scratch_shapes=[pltpu.VMEM((hb, LANES, HEAD_DIM), F32)],
        name="moba_prompt",
        compiler_params=_cparams("parallel", "parallel", "arbitrary"),
    )(proj, proj, proj, bias_diag, bias_prev, far)


def _block_diag_rows(x, lq):
    return jnp.concatenate([x[h * lq:(h + 1) * lq, h * HEAD_DIM:(h + 1) * HEAD_DIM] for h in range(HEADS)], axis=0)


def _moba_sample_kernel(pt_ref, k0_ref, k1_ref, v0_ref, v1_ref, qkv_ref, bl_ref, bc_ref, bf_ref, o_ref,
                        qbd_ref, m_ref, l_ref, acc_ref, means_ref, *, nb, lq):
    n = pl.program_id(1)
    d_model = HEADS * HEAD_DIM
    hq = HEADS * lq

    @pl.when(n == 0)
    def _():
        q = qkv_ref[:, 0:d_model] * (HEAD_DIM ** -0.5)
        rows = lax.broadcasted_iota(jnp.int32, (d_model, hq), 0) // HEAD_DIM
        cols = lax.broadcasted_iota(jnp.int32, (d_model, hq), 1)
        spread = (lax.broadcasted_iota(jnp.int32, (lq, hq), 1) % lq
                  == lax.broadcasted_iota(jnp.int32, (lq, hq), 0)).astype(F32)
        qt = lax.dot_general(q, spread, (((0,), (0,)), ((), ())),
                             preferred_element_type=F32, precision=HI)
        qbd_ref[...] = jnp.where(rows == cols // lq, qt, 0.0)
        means_ref[...] = jnp.zeros_like(means_ref)
        m_ref[...] = jnp.zeros_like(m_ref)
        l_ref[...] = jnp.zeros_like(l_ref)

    qbd = qbd_ref[...]

    def rows_by_lanes(*pages):
        return jnp.concatenate(
            [jnp.concatenate([pg[pl.ds(h, PAGE_SIZE, stride=HEADS), :] for h in range(HEADS)], axis=1)
             for pg in pages], axis=0)

    kblk = rows_by_lanes(k0_ref, k1_ref)
    vblk = rows_by_lanes(v0_ref, v1_ref)
    means_ref[pl.ds(n, 1), :] = jnp.mean(kblk, axis=0, keepdims=True)
    bias = jnp.where(n == nb - 1, bl_ref[...], bf_ref[...])
    s = _bdot(kblk, qbd) + bias
    m = jnp.max(s, axis=0, keepdims=True)
    p = jnp.exp(s - m)
    m_ref[pl.ds(n, 1), :] = m
    l_ref[pl.ds(n, 1), :] = jnp.sum(p, axis=0, keepdims=True)
    acc_ref[n] = _block_diag_rows(_bdot_tn(p, vblk), lq)

    @pl.when(n == nb - 1)
    def _():
        k_new = qkv_ref[:, d_model:2 * d_model]
        v_new = qkv_ref[:, 2 * d_model:3 * d_model]
        jq = lax.broadcasted_iota(jnp.int32, (lq, hq), 1) % lq
        jk = lax.broadcasted_iota(jnp.int32, (lq, hq), 0)
        sc = jnp.where(jk <= jq, _bdot(k_new, qbd) + bc_ref[...], NEG)
        mc = jnp.max(sc, axis=0, keepdims=True)
        pc = jnp.exp(sc - mc)
        lc = jnp.sum(pc, axis=0, keepdims=True)
        oc = _block_diag_rows(_bdot_tn(pc, v_new), lq)

        gate = _fdot(means_ref[...], qbd)
        sel = _top_mask(gate, nb, MOBA_TOPK, 0)
        m_all = m_ref[...]
        m_tot = jnp.maximum(jnp.max(jnp.where(sel > 0.0, m_all, NEG), axis=0, keepdims=True), mc)
        w = jnp.where(sel > 0.0, jnp.exp(jnp.where(sel > 0.0, m_all - m_tot, 0.0)), 0.0)
        wc = jnp.exp(mc - m_tot)
        denom = jnp.sum(w * l_ref[...], axis=0, keepdims=True) + wc * lc
        nbp = w.shape[0]
        w_all = jnp.concatenate([w, wc, jnp.zeros((SUBLANES - 1, hq), F32)], axis=0) / denom
        ne = nbp + SUBLANES
        eye = (lax.broadcasted_iota(jnp.int32, (ne, ne), 0)
               == lax.broadcasted_iota(jnp.int32, (ne, ne), 1)).astype(F32)
        wt = lax.dot_general(w_all, eye, (((0,), (0,)), ((), ())),
                             preferred_element_type=F32, precision=HI)
        out = wt[:, nbp:nbp + 1] * oc
        for b_i in range(nb):
            out = out + wt[:, b_i:b_i + 1] * acc_ref[b_i]
        o_ref[...] = out


def _moba_sample(proj, cache_k, cache_v, layer, page_table, rel_bias):
    b, lq, _ = proj.shape
    n_pages = page_table.shape[1]
    d_model = HEADS * HEAD_DIM
    pages_per_block = MOBA_BLOCK // PAGE_SIZE
    assert pages_per_block == 2 and n_pages % pages_per_block == 0 and lq <= MOBA_BLOCK
    assert cache_k.shape[2:] == (PAGE_SIZE, HEADS, HEAD_DIM)
    nb = n_pages // pages_per_block
    assert nb >= 1 and MOBA_BLOCK >= REL_MAX_DIST
    nbp = _round_up(nb, SUBLANES)
    hq = HEADS * lq
    jq = jnp.arange(lq, dtype=jnp.int32)
    t = jnp.arange(MOBA_BLOCK, dtype=jnp.int32)
    by_cols = lambda tab: jnp.transpose(tab, (1, 0, 2)).reshape(tab.shape[1], hq)
    bias_last = by_cols(_bias_of_distance(rel_bias, (MOBA_BLOCK + jq)[None, :] - t[:, None]))
    bias_cur = by_cols(_bias_of_distance(rel_bias, jq[None, :] - jq[:, None]))
    bias_far = jnp.repeat(rel_bias[REL_BUCKETS - 1].astype(F32), lq)[None, :]

    n_c, n_pool = cache_k.shape[:2]
    cache_k = cache_k.reshape(n_c, n_pool, PAGE_SIZE * HEADS, HEAD_DIM)
    cache_v = cache_v.reshape(n_c, n_pool, PAGE_SIZE * HEADS, HEAD_DIM)
    page = lambda off: pl.BlockSpec((None, None, PAGE_SIZE * HEADS, HEAD_DIM),
                                    lambda bi, n, pt: (layer, pt[bi, pages_per_block * n + off], 0, 0))
    full = lambda shape: pl.BlockSpec(shape, lambda bi, n, pt: (0,) * len(shape))
    grid_spec = pltpu.PrefetchScalarGridSpec(
        num_scalar_prefetch=1,
        grid=(b, nb),
        in_specs=[page(0), page(1), page(0), page(1),
                  pl.BlockSpec((None, lq, 3 * d_model), lambda bi, n, pt: (bi, 0, 0)),
                  full((MOBA_BLOCK, hq)), full((lq, hq)), full((1, hq))],
        out_specs=pl.BlockSpec((None, hq, HEAD_DIM), lambda bi, n, pt: (bi, 0, 0)),
        scratch_shapes=[pltpu.VMEM((d_model, hq), F32),
                        pltpu.VMEM((nbp, hq), F32),
                        pltpu.VMEM((nbp, hq), F32),
                        pltpu.VMEM((nb, hq, HEAD_DIM), F32),
                        pltpu.VMEM((nbp, d_model), F32)])
    return pl.pallas_call(
        functools.partial(_moba_sample_kernel, nb=nb, lq=lq),
        grid_spec=grid_spec,
        out_shape=jax.ShapeDtypeStruct((b, hq, HEAD_DIM), F32),
        name="moba_sample",
        compiler_params=_cparams("parallel", "arbitrary"),
    )(page_table, cache_k, cache_k, cache_v, cache_v, proj, bias_last, bias_cur, bias_far)


def _trunk(x, mods, s_hgrn, s_conv, s_gdn, paged, W):
    b, l, d = x.shape
    tiling = _row_tile(b, l)
    x2 = x.reshape(b * l, d)
    new_h, new_c, new_g, new_k, new_v = [], [], [], [], []
    for i in range(DEPTH):
        mod = _mod_layout(mods[i], b, l)
        x2 = _ffn_block(x2, mod, tiling, 0, W['ffn1_up'], W['ffn1_down'], i, W['ln_g'][i, 0], W['ln_b'][i, 0])
        kind, li = i % N_MIXERS, i // N_MIXERS
        if kind == 0:
            proj = _proj_in(x2, mod, tiling, 1, W['hgrn_w_in'][li], 1024)
            o, s = _hgrn_recurrence(proj.reshape(b, l, -1), s_hgrn[li], W['hgrn_lb'][li], W['hgrn_norm_g'][li])
            new_h.append(s)
            w_o = W['hgrn_w_o'][li]
            o2 = o.reshape(b * l, d)
        elif kind == 1:
            proj = _proj_in(x2, mod, tiling, 1, W['gdn_w_in'][li], W['gdn_w_in'][li].shape[1] // 3)
            proj = proj.reshape(b, l, -1)
            o, s = _gdn_recurrence(proj, s_conv[li], s_gdn[li], W['gdn_conv_w'][li], W['gdn_a_log'][li],
                                   W['gdn_dt_bias'][li], W['gdn_norm_g'][li])
            n_qkv = 3 * HEADS * HEAD_DIM
            assert l >= CONV_W - 1
            new_c.append(proj[:, l - (CONV_W - 1):, :n_qkv])
            new_g.append(s)
            w_o = W['gdn_w_o'][li]
            o2 = o.reshape(b * l, d)
        else:
            proj = _proj_in(x2, mod, tiling, 1, W['moba_w_qkv'][li], 1024).reshape(b, l, -1)
            if paged is None:
                o2 = _moba_prompt(proj, W['rel_bias']).reshape(b * l, d)
            else:
                ck, cv, pt = paged
                o = _moba_sample(proj, ck, cv, li, pt, W['rel_bias'])
                o2 = jnp.transpose(o.reshape(b, HEADS, l, HEAD_DIM), (0, 2, 1, 3)).reshape(b * l, d)
            new_k.append(proj[:, :, d:2 * d].reshape(b, l, HEADS, HEAD_DIM))
            new_v.append(proj[:, :, 2 * d:3 * d].reshape(b, l, HEADS, HEAD_DIM))
            w_o = W['moba_w_o'][li]
        x2 = _proj_out(o2, x2, mod, tiling, 1, w_o, W['ln_g'][i, 1], W['ln_b'][i, 1])
        x2 = _ffn_block(x2, mod, tiling, 2, W['ffn2_up'], W['ffn2_down'], i, W['ln_g'][i, 2], W['ln_b'][i, 2])
    return (x2.reshape(b, l, d), jnp.stack(new_h), jnp.stack(new_c), jnp.stack(new_g),
            jnp.stack(new_k), jnp.stack(new_v))


def kernel(x_prompt, x_sample, state_hgrn, state_gdn_conv, state_gdn, cache_k, cache_v, page_table, c_prompt, c_sample, w_ada, b_ada, ln_g, ln_b, w_ffn1_up, w_ffn1_down, w_ffn2_up, w_ffn2_down, hgrn_w_in, hgrn_lb_logits, hgrn_norm_g, hgrn_w_o, gdn_w_in, gdn_conv_w, gdn_a_log, gdn_dt_bias, gdn_norm_g, gdn_w_o, moba_w_qkv, moba_w_o, rel_bias):
    bp, _, d = x_prompt.shape
    bs = x_sample.shape[0]
    assert d == HEADS * HEAD_DIM

    nc = _round_up(bp + bs, SUBLANES)
    c_all = jnp.pad(jnp.concatenate([c_prompt, c_sample], axis=0), ((0, nc - bp - bs), (0, 0)))
    mods = _ada_mods(c_all, w_ada, b_ada).reshape(DEPTH, nc, 9, d)

    ffn1_up, ffn1_down = _prep_ffn_weights(w_ffn1_up, w_ffn1_down)
    ffn2_up, ffn2_down = _prep_ffn_weights(w_ffn2_up, w_ffn2_down)
    gdn_n = _round_up(gdn_w_in.shape[2], 3 * LANES)
    lbp = jax.nn.softmax(hgrn_lb_logits.astype(F32), axis=0)
    W = {
        'ffn1_up': ffn1_up, 'ffn1_down': ffn1_down, 'ffn2_up': ffn2_up, 'ffn2_down': ffn2_down,
        'ln_g': ln_g, 'ln_b': ln_b,
        'hgrn_w_in': hgrn_w_in.astype(BF16), 'hgrn_lb': jnp.cumsum(lbp, axis=0) - lbp[0],
        'hgrn_norm_g': hgrn_norm_g, 'hgrn_w_o': hgrn_w_o.astype(BF16),
        'gdn_w_in': jnp.pad(gdn_w_in, ((0, 0), (0, 0), (0, gdn_n - gdn_w_in.shape[2]))).astype(BF16),
        'gdn_conv_w': gdn_conv_w, 'gdn_a_log': gdn_a_log, 'gdn_dt_bias': gdn_dt_bias,
        'gdn_norm_g': gdn_norm_g, 'gdn_w_o': gdn_w_o.astype(BF16),
        'moba_w_qkv': moba_w_qkv.astype(BF16), 'moba_w_o': moba_w_o.astype(BF16), 'rel_bias': rel_bias,
    }

    n_a, n_b = state_hgrn.shape[0], state_gdn.shape[0]
    z_hgrn = jnp.zeros((n_a, bp) + state_hgrn.shape[2:], F32)
    z_conv = jnp.zeros((n_b, bp) + state_gdn_conv.shape[2:], F32)
    z_gdn = jnp.zeros((n_b, bp) + state_gdn.shape[2:], F32)
    y_p, hgrn_p, conv_p, gdn_p, k_p, v_p = _trunk(x_prompt, mods[:, :bp], z_hgrn, z_conv, z_gdn, None, W)

    y_s, hgrn_s, conv_s, gdn_s, k_s, v_s = _trunk(x_sample, mods[:, bp:bp + bs], state_hgrn, state_gdn_conv,
                                                  state_gdn, (cache_k, cache_v, page_table), W)
    return (y_p, y_s, hgrn_p, hgrn_s, conv_p, conv_s, gdn_p, gdn_s, k_p, k_s, v_p, v_s)
```

```python
import functools
import math

import jax
import jax.numpy as jnp
from jax import lax
from jax.experimental import pallas as pl
from jax.experimental.pallas import tpu as pltpu

F32 = jnp.float32
BF16 = jnp.bfloat16
HI = lax.Precision.HIGHEST

DEPTH = 4
N_MIXERS = 3
HEADS = 8
HEAD_DIM = 128
CONV_W = 4
CHUNK = 64
MOBA_BLOCK = 256
MOBA_TOPK = 3
REL_BUCKETS = 32
REL_MAX_DIST = 128
PAGE_SIZE = 128
DN_ALPHA = (2 * DEPTH) ** 0.25
LN_EPS = 1e-5
NEG = -1e30

LANES = 128
SUBLANES = 8
MXU_WIDTH = 256
VMEM_LIMIT_BYTES = 56 * 1024 * 1024


def _cparams(*sem):
    return pltpu.CompilerParams(dimension_semantics=sem, vmem_limit_bytes=VMEM_LIMIT_BYTES)


def _round_up(n, m):
    return (n + m - 1) // m * m


def _bdot(a, b):
    return jnp.dot(a.astype(BF16), b.astype(BF16), preferred_element_type=F32)


def _bdot_nt(a, b):
    return lax.dot_general(a.astype(BF16), b.astype(BF16), (((1,), (1,)), ((), ())),
                           preferred_element_type=F32)


def _bdot_tn(a, b):
    return lax.dot_general(a.astype(BF16), b.astype(BF16), (((0,), (0,)), ((), ())),
                           preferred_element_type=F32)


def _fdot(a, b):
    return jnp.dot(a, b, preferred_element_type=F32, precision=HI)


def _block_tril3(r, c):
    row = lax.broadcasted_iota(jnp.int32, (r, 3 * r), 0)
    col = lax.broadcasted_iota(jnp.int32, (r, 3 * r), 1) % r
    return jnp.where(((row // c) == (col // c)) & (row >= col), 1.0, 0.0).astype(BF16)


def _prefix_dot(sel3, x):
    hi = x.astype(BF16)
    rest = x - hi.astype(F32)
    mid = rest.astype(BF16)
    lo = (rest - mid.astype(F32)).astype(BF16)
    return jnp.dot(sel3, jnp.concatenate([hi, mid, lo], axis=0), preferred_element_type=F32)


def _sigmoid(x):
    return 1.0 / (1.0 + jnp.exp(-x))


def _silu(x):
    return x * _sigmoid(x)


def _layer_norm_rows(z, g, b):
    mu = jnp.mean(z, axis=-1, keepdims=True)
    zc = z - mu
    var = jnp.mean(zc * zc, axis=-1, keepdims=True)
    return zc * lax.rsqrt(var + LN_EPS) * g + b


def _ada_kernel(c_ref, w_ref, b_ref, o_ref):
    o_ref[...] = _bdot(_silu(c_ref[...]), w_ref[...]) + b_ref[...]


def _ada_mods(c_all, w_ada, b_ada):
    depth, d, n = w_ada.shape
    nb = c_all.shape[0]
    tn = n // 8
    assert n % 8 == 0 and tn % LANES == 0
    return pl.pallas_call(
        _ada_kernel,
        grid=(depth, n // tn),
        in_specs=[pl.BlockSpec((nb, d), lambda l, j: (0, 0)),
                  pl.BlockSpec((None, d, tn), lambda l, j: (l, 0, j)),
                  pl.BlockSpec((None, 1, tn), lambda l, j: (l, 0, j))],
        out_specs=pl.BlockSpec((None, nb, tn), lambda l, j: (l, 0, j)),
        out_shape=jax.ShapeDtypeStruct((depth, nb, n), F32),
        name="ada_mods",
        compiler_params=_cparams("parallel", "parallel"),
    )(c_all, w_ada, b_ada.reshape(depth, 1, n))


MAX_ROW_TILE = 1024


def _row_tile(batch, length):
    if length >= MXU_WIDTH:
        tm = MAX_ROW_TILE
        while length % tm:
            tm //= 2
        return tm, length // tm, True
    rows = batch * length
    assert rows % SUBLANES == 0 and rows <= MAX_ROW_TILE
    return rows, 1, False


def _mod_layout(m, batch, length):
    tm, _, per_batch = _row_tile(batch, length)
    mt = jnp.transpose(m, (1, 0, 2))
    if per_batch:
        return mt[:, :, None, :]
    return jnp.repeat(mt, length, axis=1)[:, None, :, :]


def _mod_spec(mod, tiles_per_block):
    _, _, rm, d = mod.shape
    return pl.BlockSpec((9, None, rm, d), lambda i, *_: (0, i // tiles_per_block, 0, 0))


FF_CHUNK = MXU_WIDTH


def _ffn_kernel(x_ref, mod_ref, wa_ref, wu_ref, wd_ref, g_ref, b_ref, o_ref, h_ref, *, j, n_chunks):
    x = x_ref[...]
    h_ref[...] = (x * (1.0 + mod_ref[3 * j + 1]) + mod_ref[3 * j]).astype(BF16)
    o_ref[...] = jnp.zeros_like(o_ref)

    def body(c, carry):
        c0 = pl.multiple_of(c * FF_CHUNK, FF_CHUNK)
        h = h_ref[...]
        a = jnp.dot(h, wa_ref[:, pl.ds(c0, FF_CHUNK)], preferred_element_type=F32)
        u = jnp.dot(h, wu_ref[:, pl.ds(c0, FF_CHUNK)], preferred_element_type=F32)
        act = (_silu(a) * u).astype(BF16)
        o_ref[...] += jnp.dot(act, wd_ref[pl.ds(c0, FF_CHUNK), :], preferred_element_type=F32)
        return carry

    lax.fori_loop(0, n_chunks, body, 0)
    z = DN_ALPHA * x + (1.0 + mod_ref[3 * j + 2]) * (0.5 * o_ref[...])
    o_ref[...] = _layer_norm_rows(z, g_ref[...], b_ref[...])


def _ffn_block(x2d, mod, tiling, j, weights, layer, ln_g, ln_b):
    tm, tpb, _ = tiling
    rows, d = x2d.shape
    w_a, w_u, w_down = weights
    fp = w_a.shape[-1]
    kern = functools.partial(_ffn_kernel, j=j, n_chunks=fp // FF_CHUNK)
    return pl.pallas_call(
        kern,
        grid=(rows // tm,),
        in_specs=[pl.BlockSpec((tm, d), lambda i: (i, 0)),
                  _mod_spec(mod, tpb),
                  pl.BlockSpec((None, d, fp), lambda i: (layer, 0, 0), pipeline_mode=pl.Buffered(1)),
                  pl.BlockSpec((None, d, fp), lambda i: (layer, 0, 0), pipeline_mode=pl.Buffered(1)),
                  pl.BlockSpec((None, fp, d), lambda i: (layer, 0, 0), pipeline_mode=pl.Buffered(1)),
                  pl.BlockSpec((1, d), lambda i: (0, 0)),
                  pl.BlockSpec((1, d), lambda i: (0, 0))],
        out_specs=pl.BlockSpec((tm, d), lambda i: (i, 0)),
        out_shape=jax.ShapeDtypeStruct((rows, d), F32),
        scratch_shapes=[pltpu.VMEM((tm, d), BF16)],
        name="ffn_block",
        compiler_params=_cparams("parallel"),
    )(x2d, mod, w_a, w_u, w_down, ln_g.reshape(1, d), ln_b.reshape(1, d))


def _prep_ffn_weights(w_up, w_down):
    d_ff = w_up.shape[2] // 2
    fp = _round_up(d_ff, FF_CHUNK)
    pad_cols = lambda w: jnp.pad(w.astype(BF16), ((0, 0), (0, 0), (0, fp - d_ff)))
    down = jnp.pad(w_down.astype(BF16), ((0, 0), (0, fp - d_ff), (0, 0)))
    return pad_cols(w_up[:, :, :d_ff]), pad_cols(w_up[:, :, d_ff:]), down


def _proj_in_kernel(x_ref, mod_ref, w_ref, o_ref, h_ref, *, j):
    @pl.when(pl.program_id(1) == 0)
    def _():
        h_ref[...] = (x_ref[...] * (1.0 + mod_ref[3 * j + 1]) + mod_ref[3 * j]).astype(BF16)

    o_ref[...] = jnp.dot(h_ref[...], w_ref[...], preferred_element_type=F32)


def _proj_in(x2d, mod, tiling, j, w, tn):
    tm, tpb, _ = tiling
    rows, d = x2d.shape
    n = w.shape[1]
    assert n % tn == 0
    return pl.pallas_call(
        functools.partial(_proj_in_kernel, j=j),
        grid=(rows // tm, n // tn),
        in_specs=[pl.BlockSpec((tm, d), lambda i, k: (i, 0)),
                  _mod_spec(mod, tpb),
                  pl.BlockSpec((d, tn), lambda i, k: (0, k))],
        out_specs=pl.BlockSpec((tm, tn), lambda i, k: (i, k)),
        out_shape=jax.ShapeDtypeStruct((rows, n), F32),
        scratch_shapes=[pltpu.VMEM((tm, d), BF16)],
        name="proj_in",
        compiler_params=_cparams("parallel", "arbitrary"),
    )(x2d, mod, w)


def _proj_out_kernel(o_ref, x_ref, mod_ref, w_ref, g_ref, b_ref, y_ref, *, j):
    y = jnp.dot(o_ref[...].astype(BF16), w_ref[...], preferred_element_type=F32)
    z = DN_ALPHA * x_ref[...] + (1.0 + mod_ref[3 * j + 2]) * y
    y_ref[...] = _layer_norm_rows(z, g_ref[...], b_ref[...])


def _proj_out(o2d, x2d, mod, tiling, j, w, ln_g, ln_b):
    tm, tpb, _ = tiling
    rows, d = x2d.shape
    k = o2d.shape[1]
    return pl.pallas_call(
        functools.partial(_proj_out_kernel, j=j),
        grid=(rows // tm,),
        in_specs=[pl.BlockSpec((tm, k), lambda i: (i, 0)),
                  pl.BlockSpec((tm, d), lambda i: (i, 0)),
                  _mod_spec(mod, tpb),
                  pl.BlockSpec((k, d), lambda i: (0, 0)),
                  pl.BlockSpec((1, d), lambda i: (0, 0)),
                  pl.BlockSpec((1, d), lambda i: (0, 0))],
        out_specs=pl.BlockSpec((tm, d), lambda i: (i, 0)),
        out_shape=jax.ShapeDtypeStruct((rows, d), F32),
        name="proj_out",
        compiler_params=_cparams("parallel"),
    )(o2d, x2d, mod, w, ln_g.reshape(1, d), ln_b.reshape(1, d))


def _tri(c, strict=False):
    r = lax.broadcasted_iota(jnp.int32, (c, c), 0)
    s = lax.broadcasted_iota(jnp.int32, (c, c), 1)
    return (r > s) if strict else (r >= s)


def _hgrn_diag_masks(c):
    row = lax.broadcasted_iota(jnp.int32, (c, c), 0)
    col = lax.broadcasted_iota(jnp.int32, (c, c), 1)
    return [(col == row - d) & ((row % SUBLANES) >= d) for d in range(SUBLANES)]


def _hgrn_chunk(q, k, cum, v, st_prev, c, diag_masks):
    hs = range(len(q))
    nb = c // SUBLANES
    last_row = [cum[h][c - 1:c, :] for h in hs]
    o_state = [_bdot_nt(q[h] * jnp.exp(cum[h]), st_prev[h]) for h in hs]
    q3 = [q[h].reshape(nb, SUBLANES, LANES) for h in hs]
    k3 = [k[h].reshape(nb, SUBLANES, LANES) for h in hs]
    cum3 = [cum[h].reshape(nb, SUBLANES, LANES) for h in hs]

    att_off = None
    if nb > 1:
        blk = lax.broadcasted_iota(jnp.int32, (nb, SUBLANES, LANES), 0)
        q_cat, k_cat = [], []
        for h in hs:
            e = cum3[h][:, SUBLANES - 1:SUBLANES, :]
            k_t = k3[h] * jnp.exp(e - cum3[h])
            q_parts, k_parts = [], []
            for jb in range(nb - 1):
                qj = q3[h][jb + 1:] * jnp.exp(cum3[h][jb + 1:] - e[jb:jb + 1])
                qj = jnp.concatenate([jnp.zeros((jb + 1, SUBLANES, LANES), F32), qj], axis=0)
                q_parts.append(qj.reshape(c, LANES).astype(BF16))
                k_parts.append(jnp.where(blk == jb, k_t, 0.0).reshape(c, LANES).astype(BF16))
            q_cat.append(jnp.concatenate(q_parts, axis=1))
            k_cat.append(jnp.concatenate(k_parts, axis=1))
        att_off = [_bdot_nt(q_cat[h], k_cat[h]) for h in hs]

    att = [jnp.zeros((c, c), F32) for _ in hs]
    for d in reversed(range(SUBLANES)):
        for h in hs:
            if d == 0:
                p = q3[h] * k3[h]
            else:
                p = q3[h] * pltpu.roll(k3[h], d, 1) * jnp.exp(cum3[h] - pltpu.roll(cum3[h], d, 1))
            a_d = jnp.sum(p.reshape(c, LANES), axis=-1, keepdims=True)
            att[h] = jnp.where(diag_masks[d], a_d, att[h])
    if att_off is not None:
        att = [att[h] + att_off[h] for h in hs]

    o = [o_state[h] + _bdot(att[h], v[h]) for h in hs]
    st_new = [st_prev[h] * jnp.exp(last_row[h]) + _bdot_tn(v[h], k[h] * jnp.exp(last_row[h] - cum[h])) for h in hs]
    return o, st_new


REC_L_TILE = 512
HGRN_HEADS_PER_STEP = 8
GDN_HEADS_PER_STEP = 8
GDN_STACK = 4


def _head_lanes(hh):
    return slice(hh * HEAD_DIM, (hh + 1) * HEAD_DIM)


def _rec_tiling(l):
    c = min(CHUNK, l)
    tl = min(REC_L_TILE, l)
    assert l % tl == 0 and tl % c == 0 and c % SUBLANES == 0
    return c, tl


def _hgrn_kernel(q_ref, f_ref, v_ref, g_ref, lb_ref, ng_ref, s0_ref, o_ref, s_ref, *, c, n_chunks, hb):
    @pl.when(pl.program_id(2) == 0)
    def _():
        for hh in range(hb):
            s_ref[hh] = s0_ref[hh].T

    ng = ng_ref[...]
    lb = lb_ref[...]
    tril3 = _block_tril3(c, c)
    diag_masks = _hgrn_diag_masks(c)

    def body(i, carry):
        rows = pl.ds(pl.multiple_of(i * c, c), c)
        fr = f_ref[rows, :]
        q_all = _silu(q_ref[rows, :]) * (HEAD_DIM ** -0.5)
        k_all = (1.0 - lb) * _sigmoid(-fr)
        cum_all = _prefix_dot(tril3, jnp.log(lb + (1.0 - lb) * _sigmoid(fr)))
        lanes = [_head_lanes(hh) for hh in range(hb)]
        o, st_new = _hgrn_chunk([q_all[:, ln] for ln in lanes], [k_all[:, ln] for ln in lanes],
                                [cum_all[:, ln] for ln in lanes], [v_ref[rows, ln] for ln in lanes],
                                [s_ref[hh] for hh in range(hb)], c, diag_masks)
        for hh, ln in enumerate(lanes):
            s_ref[hh] = st_new[hh]
            ms = jnp.mean(o[hh] * o[hh], axis=-1, keepdims=True)
            o_ref[rows, ln] = o[hh] * lax.rsqrt(ms + 1e-6) * ng * _sigmoid(g_ref[rows, ln])
        return carry

    lax.fori_loop(0, n_chunks, body, 0)

    @pl.when(pl.program_id(2) == pl.num_programs(2) - 1)
    def _():
        for hh in range(hb):
            s_ref[hh] = s_ref[hh].T


def _hgrn_recurrence(proj, s0, lb, norm_g):
    b, l, _ = proj.shape
    c, tl = _rec_tiling(l)
    hb = HGRN_HEADS_PER_STEP
    ng_ = HEADS // hb
    col = lambda off: pl.BlockSpec((None, tl, hb * HEAD_DIM), lambda bi, hg, t: (bi, t, off * ng_ + hg))
    state = pl.BlockSpec((None, hb, HEAD_DIM, HEAD_DIM), lambda bi, hg, t: (bi, hg, 0, 0))
    return pl.pallas_call(
        functools.partial(_hgrn_kernel, c=c, n_chunks=tl // c, hb=hb),
        grid=(b, ng_, l // tl),
        in_specs=[col(0), col(1), col(2), col(3),
                  pl.BlockSpec((1, hb * HEAD_DIM), lambda bi, hg, t: (0, hg)),
                  pl.BlockSpec((1, HEAD_DIM), lambda bi, hg, t: (0, 0)),
                  state],
        out_specs=[col(0), state],
        out_shape=[jax.ShapeDtypeStruct((b, l, HEADS * HEAD_DIM), F32),
                   jax.ShapeDtypeStruct((b, HEADS, HEAD_DIM, HEAD_DIM), F32)],
        name="hgrn_recurrence",
        compiler_params=_cparams("parallel", "parallel", "arbitrary"),
    )(proj, proj, proj, proj, lb.reshape(1, -1), norm_g.reshape(1, -1), s0)


HIST = SUBLANES


def _gdn_conv(x_ext, cw, c):
    acc = x_ext * cw[CONV_W - 1:CONV_W, :]
    for w in range(CONV_W - 1):
        acc = acc + pltpu.roll(x_ext, CONV_W - 1 - w, 0) * cw[w:w + 1, :]
    return _silu(acc[HIST:, :])


def _l2norm_rows(x):
    return x * lax.rsqrt(jnp.sum(x * x, axis=-1, keepdims=True) + 1e-6)


def _gdn_masks(r, c):
    row = lax.broadcasted_iota(jnp.int32, (r, r), 0)
    col = lax.broadcasted_iota(jnp.int32, (r, r), 1)
    same = (row // c) == (col // c)
    low = same & (row >= col)
    return low, same & (row > col), jnp.where(row == col, 1.0, 0.0), _block_tril3(r, c)


def _gdn_chunk(groups, c, masks):
    low, strict, eye, tril3 = masks
    gr = range(len(groups))
    qs, ks, vs, betas, gs, s_prev = zip(*groups)
    n = len(s_prev[0])
    r = n * c
    pre = [_prefix_dot(tril3, jnp.concatenate(
        [jnp.where(strict, jnp.broadcast_to(gs[g], (r, r)), 0.0), jnp.broadcast_to(gs[g], (r, LANES))], axis=1))
        for g in gr]
    cum = [pre[g][:, r:] for g in gr]
    decay = [jnp.where(low, jnp.exp(jnp.where(low, pre[g][:, :r], 0.0)), 0.0) for g in gr]
    ecum = [jnp.exp(cum[g]) for g in gr]
    kb = [ks[g] * betas[g] for g in gr]
    kk = [_bdot_nt(jnp.concatenate([kb[g], qs[g]], axis=0), ks[g]) for g in gr]
    p = [-jnp.where(strict, kk[g][:r] * decay[g], 0.0) for g in gr]
    att = [kk[g][r:] * decay[g] for g in gr]
    t_inv = [eye + p[g] for g in gr]
    x = p
    for _ in range(int(math.log2(c)) - 1):
        x = [_bdot(x[g], x[g]) for g in gr]
        t_inv = [t_inv[g] + _bdot(x[g], t_inv[g]) for g in gr]
    uw = [_bdot(t_inv[g], jnp.concatenate([vs[g] * betas[g], kb[g] * ecum[g]], axis=1)) for g in gr]
    qe = [qs[g] * ecum[g] for g in gr]
    head_rows = [slice(h * c, (h + 1) * c) for h in range(n)]
    ws = [[_bdot(jnp.concatenate([uw[g][rows, HEAD_DIM:], qe[g][rows]], axis=0), s_prev[g][h])
           for h, rows in enumerate(head_rows)] for g in gr]
    v_new = [jnp.concatenate([uw[g][rows, :HEAD_DIM] - ws[g][h][:c] for h, rows in enumerate(head_rows)], axis=0)
             for g in gr]
    o = [jnp.concatenate([ws[g][h][c:] for h in range(n)], axis=0) + _bdot(att[g], v_new[g]) for g in gr]
    lasts = [[cum[g][(h + 1) * c - 1:(h + 1) * c, :] for h in range(n)] for g in gr]
    kt = [ks[g] * jnp.exp(jnp.concatenate([jnp.broadcast_to(l, (c, LANES)) for l in lasts[g]], axis=0) - cum[g])
          for g in gr]
    s_new = [[s_prev[g][h] * jnp.exp(lasts[g][h]) + _bdot_tn(kt[g][rows], v_new[g][rows])
              for h, rows in enumerate(head_rows)] for g in gr]
    return [(o[g], s_new[g]) for g in gr]


def _gdn_kernel(q_ref, k_ref, v_ref, z_ref, ba_ref, c0q_ref, c0k_ref, c0v_ref, cwq_ref, cwk_ref, cwv_ref,
                an_ref, dt_ref, ng_ref, s0_ref, o_ref, s_ref, qx_ref, kx_ref, vx_ref, *, c, n_chunks, tl, hb):
    hg = pl.program_id(1)
    t = pl.program_id(2)
    staged = ((q_ref, c0q_ref, qx_ref), (k_ref, c0k_ref, kx_ref), (v_ref, c0v_ref, vx_ref))

    @pl.when(t == 0)
    def _():
        s_ref[...] = s0_ref[...]
        for _, c0, dst in staged:
            dst[0:HIST, :] = jnp.zeros((HIST, hb * HEAD_DIM), F32)
            dst[HIST - (CONV_W - 1):HIST, :] = c0[...]

    @pl.when(t > 0)
    def _():
        for _, _, dst in staged:
            dst[0:HIST, :] = dst[tl:tl + HIST, :]

    for src, _, dst in staged:
        dst[HIST:HIST + tl, :] = src[...]
    ng = ng_ref[...]
    lane = lax.broadcasted_iota(jnp.int32, (c, LANES), 1)
    masks = _gdn_masks(GDN_STACK * c, c)

    def body(i, carry):
        r0 = pl.multiple_of(i * c, c)
        ext = pl.ds(r0, c + HIST)
        rows = pl.ds(r0, c)
        ba = ba_ref[rows, :]
        stack = lambda parts: jnp.concatenate(parts, axis=0)
        groups = []
        for h0 in range(0, hb, GDN_STACK):
            qs, ks, vs, betas, gs = [], [], [], [], []
            for hh in range(h0, h0 + GDN_STACK):
                ln = _head_lanes(hh)
                head = hg * hb + hh
                qs.append(_l2norm_rows(_gdn_conv(qx_ref[ext, ln], cwq_ref[:, ln], c)) * (HEAD_DIM ** -0.5))
                ks.append(_l2norm_rows(_gdn_conv(kx_ref[ext, ln], cwk_ref[:, ln], c)))
                vs.append(_gdn_conv(vx_ref[ext, ln], cwv_ref[:, ln], c))
                betas.append(_sigmoid(jnp.sum(jnp.where(lane == head, ba, 0.0), axis=-1, keepdims=True)))
                a_raw = jnp.sum(jnp.where(lane == HEADS + head, ba, 0.0), axis=-1, keepdims=True)
                xa = a_raw + dt_ref[:, ln][:, 0:1]
                softplus = jnp.maximum(xa, 0.0) + jnp.log(1.0 + jnp.exp(-jnp.abs(xa)))
                gs.append(an_ref[:, ln][:, 0:1] * softplus)
            groups.append((stack(qs), stack(ks), stack(vs), stack(betas), stack(gs),
                           [s_ref[hh] for hh in range(h0, h0 + GDN_STACK)]))
        for gi, (o, s_new) in enumerate(_gdn_chunk(groups, c, masks)):
            for j in range(GDN_STACK):
                hh = gi * GDN_STACK + j
                ln = _head_lanes(hh)
                s_ref[hh] = s_new[j]
                oh = o[j * c:(j + 1) * c]
                ms = jnp.mean(oh * oh, axis=-1, keepdims=True)
                o_ref[rows, ln] = oh * lax.rsqrt(ms + 1e-6) * ng * _silu(z_ref[rows, ln])
        return carry

    lax.fori_loop(0, n_chunks, body, 0)


def _gdn_recurrence(proj, conv0, s0, conv_w, a_log, dt_bias, norm_g):
    b, l, _ = proj.shape
    c, tl = _rec_tiling(l)
    hb = GDN_HEADS_PER_STEP
    ng_ = HEADS // hb
    w = hb * HEAD_DIM
    col = lambda off: pl.BlockSpec((None, tl, w), lambda bi, hg, t: (bi, t, off * ng_ + hg))
    c0 = lambda off: pl.BlockSpec((None, CONV_W - 1, w), lambda bi, hg, t: (bi, 0, off * ng_ + hg))
    cw = lambda off: pl.BlockSpec((CONV_W, w), lambda bi, hg, t: (0, off * ng_ + hg))
    per_head = pl.BlockSpec((1, w), lambda bi, hg, t: (0, hg))
    state = pl.BlockSpec((None, hb, HEAD_DIM, HEAD_DIM), lambda bi, hg, t: (bi, hg, 0, 0))
    a_neg = jnp.repeat(-jnp.exp(a_log.astype(F32)), HEAD_DIM)[None, :]
    dt_b = jnp.repeat(dt_bias.astype(F32), HEAD_DIM)[None, :]
    xbuf = pltpu.VMEM((HIST + tl, w), F32)
    return pl.pallas_call(
        functools.partial(_gdn_kernel, c=c, n_chunks=tl // c, tl=tl, hb=hb),
        grid=(b, ng_, l // tl),
        in_specs=[col(0), col(1), col(2), col(3),
                  pl.BlockSpec((None, tl, LANES), lambda bi, hg, t: (bi, t, 4 * HEADS)),
                  c0(0), c0(1), c0(2), cw(0), cw(1), cw(2),
                  per_head, per_head,
                  pl.BlockSpec((1, HEAD_DIM), lambda bi, hg, t: (0, 0)),
                  state],
        out_specs=[col(0), state],
        out_shape=[jax.ShapeDtypeStruct((b, l, HEADS * HEAD_DIM), F32),
                   jax.ShapeDtypeStruct((b, HEADS, HEAD_DIM, HEAD_DIM), F32)],
        scratch_shapes=[xbuf, xbuf, xbuf],
        name="gdn_recurrence",
        compiler_params=_cparams("parallel", "parallel", "arbitrary"),
    )(proj, proj, proj, proj, proj, conv0, conv0, conv0, conv_w, conv_w, conv_w,
      a_neg, dt_b, norm_g.reshape(1, -1), s0)


def _rel_bucket(dist):
    exact = REL_BUCKETS // 2
    d = jnp.maximum(dist, exact).astype(F32)
    large = exact + (jnp.log(d / exact) / math.log(REL_MAX_DIST / exact) * (REL_BUCKETS - exact)).astype(jnp.int32)
    return jnp.where(dist < exact, dist, jnp.minimum(large, REL_BUCKETS - 1))


def _top_mask(gate, n_valid, k, axis):
    idx = lax.broadcasted_iota(jnp.int32, gate.shape, axis)
    big = jnp.int32(2 ** 30)
    work = jnp.where(idx < n_valid, gate, NEG)
    sel = jnp.zeros(gate.shape, F32)
    for r in range(k):
        mx = jnp.max(work, axis=axis, keepdims=True)
        first = jnp.min(jnp.where(work == mx, idx, big), axis=axis, keepdims=True)
        pick = (idx == first) & (idx < n_valid)
        sel = jnp.where(pick, 1.0, sel)
        work = jnp.where(idx == first, -jnp.inf, work)
    return sel


MOBA_HEADS_PER_STEP = 4


def _moba_prompt_kernel(q_ref, k_ref, v_ref, bd_ref, bp_ref, bf_ref, o_ref, means_ref, sel_ref, *, nb, hb):
    i = pl.program_id(2)
    blk = MOBA_BLOCK

    @pl.when(i == 0)
    def _():
        means_ref[...] = jnp.zeros_like(means_ref)
        for hh in range(hb):
            for n in range(nb):
                means_ref[hh, n:n + 1, :] = jnp.mean(k_ref[n * blk:(n + 1) * blk, _head_lanes(hh)],
                                                     axis=0, keepdims=True)

    causal = ~_tri(blk, strict=True)
    r0 = pl.multiple_of(i * blk, blk)
    heads = range(hb)
    qs = [q_ref[:, _head_lanes(hh)] * (HEAD_DIM ** -0.5) for hh in heads]
    qbs = [q.astype(BF16) for q in qs]
    gates = [lax.dot_general(means_ref[hh], qs[hh], (((1,), (1,)), ((), ())),
                             preferred_element_type=F32, precision=HI) for hh in heads]
    qk = [_bdot_nt(k_ref[pl.ds(r0, blk), _head_lanes(hh)], qbs[hh]) for hh in heads]
    for hh in heads:
        sel_ref[hh] = _top_mask(gates[hh], i, MOBA_TOPK, 0)
    s0 = [jnp.where(causal, qk[hh] + bd_ref[hh], NEG) for hh in heads]
    m0 = [jnp.max(s, axis=0, keepdims=True) for s in s0]
    p0 = [jnp.exp(s0[hh] - m0[hh]) for hh in heads]
    init = [(m0[hh], jnp.sum(p0[hh], axis=0, keepdims=True),
             _bdot_tn(v_ref[pl.ds(r0, blk), _head_lanes(hh)], p0[hh])) for hh in heads]

    def past_block(n, carry, bias_of_head):
        rn = pl.multiple_of(n * blk, blk)
        heads = range(hb)
        qk = [_bdot_nt(k_ref[pl.ds(rn, blk), _head_lanes(hh)], qbs[hh]) for hh in heads]
        sn = [jnp.where(sel_ref[hh, pl.ds(n, 1), :] > 0.0, qk[hh] + bias_of_head(hh), NEG) for hh in heads]
        m_new = [jnp.maximum(carry[hh][0], jnp.max(sn[hh], axis=0, keepdims=True)) for hh in heads]
        pn = [jnp.exp(sn[hh] - m_new[hh]) for hh in heads]
        pv = [_bdot_tn(v_ref[pl.ds(rn, blk), _head_lanes(hh)], pn[hh]) for hh in heads]
        out = []
        for hh in heads:
            m, l, acc = carry[hh]
            alpha = jnp.exp(m - m_new[hh])
            out.append((m_new[hh], l * alpha + jnp.sum(pn[hh], axis=0, keepdims=True), acc * alpha + pv[hh]))
        return tuple(out)

    carry = lax.fori_loop(0, jnp.maximum(i - 1, 0),
                          lambda n, cr: past_block(n, cr, lambda hh: bf_ref[:, _head_lanes(hh)][:, 0:1]),
                          tuple(init))
    final = past_block(jnp.maximum(i - 1, 0), carry, lambda hh: bp_ref[hh])
    for hh in range(hb):
        _, l, acc = final[hh]
        o_ref[:, _head_lanes(hh)] = (acc / l).T


def _bias_of_distance(rel_bias, dist):
    bucket = _rel_bucket(jnp.maximum(dist, 0))[None]
    out = jnp.zeros((HEADS,) + dist.shape, F32)
    for b in range(REL_BUCKETS):
        out = jnp.where(bucket == b, rel_bias[b].astype(F32).reshape((HEADS,) + (1,) * dist.ndim), out)
    return out


def _moba_bias_tables(rel_bias, blk):
    t = jnp.arange(blk, dtype=jnp.int32)
    d0 = t[None, :] - t[:, None]
    bias_diag = _bias_of_distance(rel_bias, d0)
    bias_prev = _bias_of_distance(rel_bias, d0 + blk)
    far = jnp.repeat(rel_bias[REL_BUCKETS - 1].astype(F32), HEAD_DIM)[None, :]
    return bias_diag, bias_prev, far


def _moba_prompt(proj, rel_bias):
    b, l, _ = proj.shape
    blk = MOBA_BLOCK
    assert l % blk == 0 and l // blk <= LANES
    assert 2 * blk > REL_MAX_DIST
    nb = l // blk
    bias_diag, bias_prev, far = _moba_bias_tables(rel_bias, blk)
    hb = MOBA_HEADS_PER_STEP
    ng_ = HEADS // hb
    w = hb * HEAD_DIM
    tile = pl.BlockSpec((hb, blk, blk), lambda bi, hg, i: (hg, 0, 0))
    return pl.pallas_call(
        functools.partial(_moba_prompt_kernel, nb=nb, hb=hb),
        grid=(b, ng_, nb),
        in_specs=[pl.BlockSpec((None, blk, w), lambda bi, hg, i: (bi, i, hg)),
                  pl.BlockSpec((None, l, w), lambda bi, hg, i: (bi, 0, ng_ + hg)),
                  pl.BlockSpec((None, l, w), lambda bi, hg, i: (bi, 0, 2 * ng_ + hg)),
                  tile, tile,
                  pl.BlockSpec((1, w), lambda bi, hg, i: (0, hg))],
        out_specs=pl.BlockSpec((None, blk, w), lambda bi, hg, i: (bi, i, hg)),
        out_shape=jax.ShapeDtypeStruct((b, l, HEADS * HEAD_DIM), F32),
        scratch_shapes=[pltpu.VMEM((hb, _round_up(nb, SUBLANES), HEAD_DIM), F32),
                        pltpu.VMEM((hb, _round_up(nb, SUBLANES), blk), F32)],
        name="moba_prompt",
        compiler_params=_cparams("parallel", "parallel", "arbitrary"),
    )(proj, proj, proj, bias_diag, bias_prev, far)


def _block_diag_rows(x, lq):
    return jnp.concatenate([x[h * lq:(h + 1) * lq, h * HEAD_DIM:(h + 1) * HEAD_DIM] for h in range(HEADS)], axis=0)


def _moba_sample_kernel(pt_ref, k0_ref, k1_ref, v0_ref, v1_ref, qkv_ref, bl_ref, bc_ref, bf_ref, o_ref,
                        qbd_ref, m_ref, l_ref, acc_ref, means_ref, *, nb, lq):
    n = pl.program_id(1)
    d_model = HEADS * HEAD_DIM
    hq = HEADS * lq

    @pl.when(n == 0)
    def _():
        q = qkv_ref[:, 0:d_model] * (HEAD_DIM ** -0.5)
        rows = lax.broadcasted_iota(jnp.int32, (d_model, hq), 0) // HEAD_DIM
        cols = lax.broadcasted_iota(jnp.int32, (d_model, hq), 1)
        spread = (lax.broadcasted_iota(jnp.int32, (lq, hq), 1) % lq
                  == lax.broadcasted_iota(jnp.int32, (lq, hq), 0)).astype(F32)
        qt = lax.dot_general(q, spread, (((0,), (0,)), ((), ())),
                             preferred_element_type=F32, precision=HI)
        qbd_ref[...] = jnp.where(rows == cols // lq, qt, 0.0)
        means_ref[...] = jnp.zeros_like(means_ref)
        m_ref[...] = jnp.zeros_like(m_ref)
        l_ref[...] = jnp.zeros_like(l_ref)

    qbd = qbd_ref[...]

    def rows_by_lanes(*pages):
        return jnp.concatenate(
            [jnp.concatenate([pg[pl.ds(h, PAGE_SIZE, stride=HEADS), :] for h in range(HEADS)], axis=1)
             for pg in pages], axis=0)

    kblk = rows_by_lanes(k0_ref, k1_ref)
    vblk = rows_by_lanes(v0_ref, v1_ref)
    means_ref[pl.ds(n, 1), :] = jnp.mean(kblk, axis=0, keepdims=True)
    bias = jnp.where(n == nb - 1, bl_ref[...], bf_ref[...])
    s = _bdot(kblk, qbd) + bias
    m = jnp.max(s, axis=0, keepdims=True)
    p = jnp.exp(s - m)
    m_ref[pl.ds(n, 1), :] = m
    l_ref[pl.ds(n, 1), :] = jnp.sum(p, axis=0, keepdims=True)
    acc_ref[n] = _block_diag_rows(_bdot_tn(p, vblk), lq)

    @pl.when(n == nb - 1)
    def _():
        k_new = qkv_ref[:, d_model:2 * d_model]
        v_new = qkv_ref[:, 2 * d_model:3 * d_model]
        jq = lax.broadcasted_iota(jnp.int32, (lq, hq), 1) % lq
        jk = lax.broadcasted_iota(jnp.int32, (lq, hq), 0)
        sc = jnp.where(jk <= jq, _bdot(k_new, qbd) + bc_ref[...], NEG)
        mc = jnp.max(sc, axis=0, keepdims=True)
        pc = jnp.exp(sc - mc)
        lc = jnp.sum(pc, axis=0, keepdims=True)
        oc = _block_diag_rows(_bdot_tn(pc, v_new), lq)

        gate = _fdot(means_ref[...], qbd)
        sel = _top_mask(gate, nb, MOBA_TOPK, 0)
        m_all = m_ref[...]
        m_tot = jnp.maximum(jnp.max(jnp.where(sel > 0.0, m_all, NEG), axis=0, keepdims=True), mc)
        w = jnp.where(sel > 0.0, jnp.exp(jnp.where(sel > 0.0, m_all - m_tot, 0.0)), 0.0)
        wc = jnp.exp(mc - m_tot)
        denom = jnp.sum(w * l_ref[...], axis=0, keepdims=True) + wc * lc
        nbp = w.shape[0]
        w_all = jnp.concatenate([w, wc, jnp.zeros((SUBLANES - 1, hq), F32)], axis=0) / denom
        ne = nbp + SUBLANES
        eye = (lax.broadcasted_iota(jnp.int32, (ne, ne), 0)
               == lax.broadcasted_iota(jnp.int32, (ne, ne), 1)).astype(F32)
        wt = lax.dot_general(w_all, eye, (((0,), (0,)), ((), ())),
                             preferred_element_type=F32, precision=HI)
        out = wt[:, nbp:nbp + 1] * oc
        for b_i in range(nb):
            out = out + wt[:, b_i:b_i + 1] * acc_ref[b_i]
        o_ref[...] = out


def _moba_sample(proj, cache_k, cache_v, layer, page_table, rel_bias):
    b, lq, _ = proj.shape
    n_pages = page_table.shape[1]
    d_model = HEADS * HEAD_DIM
    pages_per_block = MOBA_BLOCK // PAGE_SIZE
    assert pages_per_block == 2 and n_pages % pages_per_block == 0 and lq <= MOBA_BLOCK
    assert cache_k.shape[2:] == (PAGE_SIZE, HEADS, HEAD_DIM)
    nb = n_pages // pages_per_block
    assert nb >= 1 and MOBA_BLOCK >= REL_MAX_DIST
    nbp = _round_up(nb, SUBLANES)
    hq = HEADS * lq
    jq = jnp.arange(lq, dtype=jnp.int32)
    t = jnp.arange(MOBA_BLOCK, dtype=jnp.int32)
    by_cols = lambda tab: jnp.transpose(tab, (1, 0, 2)).reshape(tab.shape[1], hq)
    bias_last = by_cols(_bias_of_distance(rel_bias, (MOBA_BLOCK + jq)[None, :] - t[:, None]))
    bias_cur = by_cols(_bias_of_distance(rel_bias, jq[None, :] - jq[:, None]))
    bias_far = jnp.repeat(rel_bias[REL_BUCKETS - 1].astype(F32), lq)[None, :]

    n_c, n_pool = cache_k.shape[:2]
    cache_k = cache_k.reshape(n_c, n_pool, PAGE_SIZE * HEADS, HEAD_DIM)
    cache_v = cache_v.reshape(n_c, n_pool, PAGE_SIZE * HEADS, HEAD_DIM)
    page = lambda off: pl.BlockSpec((None, None, PAGE_SIZE * HEADS, HEAD_DIM),
                                    lambda bi, n, pt: (layer, pt[bi, pages_per_block * n + off], 0, 0))
    full = lambda shape: pl.BlockSpec(shape, lambda bi, n, pt: (0,) * len(shape))
    grid_spec = pltpu.PrefetchScalarGridSpec(
        num_scalar_prefetch=1,
        grid=(b, nb),
        in_specs=[page(0), page(1), page(0), page(1),
                  pl.BlockSpec((None, lq, 3 * d_model), lambda bi, n, pt: (bi, 0, 0)),
                  full((MOBA_BLOCK, hq)), full((lq, hq)), full((1, hq))],
        out_specs=pl.BlockSpec((None, hq, HEAD_DIM), lambda bi, n, pt: (bi, 0, 0)),
        scratch_shapes=[pltpu.VMEM((d_model, hq), F32),
                        pltpu.VMEM((nbp, hq), F32),
                        pltpu.VMEM((nbp, hq), F32),
                        pltpu.VMEM((nb, hq, HEAD_DIM), F32),
                        pltpu.VMEM((nbp, d_model), F32)])
    return pl.pallas_call(
        functools.partial(_moba_sample_kernel, nb=nb, lq=lq),
        grid_spec=grid_spec,
        out_shape=jax.ShapeDtypeStruct((b, hq, HEAD_DIM), F32),
        name="moba_sample",
        compiler_params=_cparams("parallel", "arbitrary"),
    )(page_table, cache_k, cache_k, cache_v, cache_v, proj, bias_last, bias_cur, bias_far)


def _trunk(x, mods, s_hgrn, s_conv, s_gdn, paged, W):
    b, l, d = x.shape
    tiling = _row_tile(b, l)
    x2 = x.reshape(b * l, d)
    new_h, new_c, new_g, new_k, new_v = [], [], [], [], []
    for i in range(DEPTH):
        mod = _mod_layout(mods[i], b, l)
        x2 = _ffn_block(x2, mod, tiling, 0, W['ffn1'], i, W['ln_g'][i, 0], W['ln_b'][i, 0])
        kind, li = i % N_MIXERS, i // N_MIXERS
        if kind == 0:
            proj = _proj_in(x2, mod, tiling, 1, W['hgrn_w_in'][li], 1024)
            o, s = _hgrn_recurrence(proj.reshape(b, l, -1), s_hgrn[li], W['hgrn_lb'][li], W['hgrn_norm_g'][li])
            new_h.append(s)
            w_o = W['hgrn_w_o'][li]
            o2 = o.reshape(b * l, d)
        elif kind == 1:
            proj = _proj_in(x2, mod, tiling, 1, W['gdn_w_in'][li], W['gdn_w_in'][li].shape[1] // 3)
            proj = proj.reshape(b, l, -1)
            o, s = _gdn_recurrence(proj, s_conv[li], s_gdn[li], W['gdn_conv_w'][li], W['gdn_a_log'][li],
                                   W['gdn_dt_bias'][li], W['gdn_norm_g'][li])
            n_qkv = 3 * HEADS * HEAD_DIM
            assert l >= CONV_W - 1
            new_c.append(proj[:, l - (CONV_W - 1):, :n_qkv])
            new_g.append(s)
            w_o = W['gdn_w_o'][li]
            o2 = o.reshape(b * l, d)
        else:
            proj = _proj_in(x2, mod, tiling, 1, W['moba_w_qkv'][li], 1024).reshape(b, l, -1)
            if paged is None:
                o2 = _moba_prompt(proj, W['rel_bias']).reshape(b * l, d)
            else:
                ck, cv, pt = paged
                o = _moba_sample(proj, ck, cv, li, pt, W['rel_bias'])
                o2 = jnp.transpose(o.reshape(b, HEADS, l, HEAD_DIM), (0, 2, 1, 3)).reshape(b * l, d)
            new_k.append(proj[:, :, d:2 * d].reshape(b, l, HEADS, HEAD_DIM))
            new_v.append(proj[:, :, 2 * d:3 * d].reshape(b, l, HEADS, HEAD_DIM))
            w_o = W['moba_w_o'][li]
        x2 = _proj_out(o2, x2, mod, tiling, 1, w_o, W['ln_g'][i, 1], W['ln_b'][i, 1])
        x2 = _ffn_block(x2, mod, tiling, 2, W['ffn2'], i, W['ln_g'][i, 2], W['ln_b'][i, 2])
    return (x2.reshape(b, l, d), jnp.stack(new_h), jnp.stack(new_c), jnp.stack(new_g),
            jnp.stack(new_k), jnp.stack(new_v))


def kernel(x_prompt, x_sample, state_hgrn, state_gdn_conv, state_gdn, cache_k, cache_v, page_table, c_prompt, c_sample, w_ada, b_ada, ln_g, ln_b, w_ffn1_up, w_ffn1_down, w_ffn2_up, w_ffn2_down, hgrn_w_in, hgrn_lb_logits, hgrn_norm_g, hgrn_w_o, gdn_w_in, gdn_conv_w, gdn_a_log, gdn_dt_bias, gdn_norm_g, gdn_w_o, moba_w_qkv, moba_w_o, rel_bias):
    bp, _, d = x_prompt.shape
    bs = x_sample.shape[0]
    assert d == HEADS * HEAD_DIM

    nc = _round_up(bp + bs, SUBLANES)
    c_all = jnp.pad(jnp.concatenate([c_prompt, c_sample], axis=0), ((0, nc - bp - bs), (0, 0)))
    mods = _ada_mods(c_all, w_ada, b_ada).reshape(DEPTH, nc, 9, d)

    gdn_n = _round_up(gdn_w_in.shape[2], 3 * LANES)
    lbp = jax.nn.softmax(hgrn_lb_logits.astype(F32), axis=0)
    W = {
        'ffn1': _prep_ffn_weights(w_ffn1_up, w_ffn1_down), 'ffn2': _prep_ffn_weights(w_ffn2_up, w_ffn2_down),
        'ln_g': ln_g, 'ln_b': ln_b,
        'hgrn_w_in': hgrn_w_in.astype(BF16), 'hgrn_lb': jnp.cumsum(lbp, axis=0) - lbp[0],
        'hgrn_norm_g': hgrn_norm_g, 'hgrn_w_o': hgrn_w_o.astype(BF16),
        'gdn_w_in': jnp.pad(gdn_w_in, ((0, 0), (0, 0), (0, gdn_n - gdn_w_in.shape[2]))).astype(BF16),
        'gdn_conv_w': gdn_conv_w, 'gdn_a_log': gdn_a_log, 'gdn_dt_bias': gdn_dt_bias,
        'gdn_norm_g': gdn_norm_g, 'gdn_w_o': gdn_w_o.astype(BF16),
        'moba_w_qkv': moba_w_qkv.astype(BF16), 'moba_w_o': moba_w_o.astype(BF16), 'rel_bias': rel_bias,
    }

    n_a, n_b = state_hgrn.shape[0], state_gdn.shape[0]
    z_hgrn = jnp.zeros((n_a, bp) + state_hgrn.shape[2:], F32)
    z_conv = jnp.zeros((n_b, bp) + state_gdn_conv.shape[2:], F32)
    z_gdn = jnp.zeros((n_b, bp) + state_gdn.shape[2:], F32)
    y_p, hgrn_p, conv_p, gdn_p, k_p, v_p = _trunk(x_prompt, mods[:, :bp], z_hgrn, z_conv, z_gdn, None, W)

    y_s, hgrn_s, conv_s, gdn_s, k_s, v_s = _trunk(x_sample, mods[:, bp:bp + bs], state_hgrn, state_gdn_conv,
                                                  state_gdn, (cache_k, cache_v, page_table), W)
    return (y_p, y_s, hgrn_p, hgrn_s, conv_p, conv_s, gdn_p, gdn_s, k_p, k_s, v_p, v_s)
```

```python
import functools
import math

import jax
import jax.numpy as jnp
from jax import lax
from jax.experimental import pallas as pl
from jax.experimental.pallas import tpu as pltpu

F32 = jnp.float32
BF16 = jnp.bfloat16
HI = lax.Precision.HIGHEST

DEPTH = 4
N_MIXERS = 3
HEADS = 8
HEAD_DIM = 128
CONV_W = 4
CHUNK = 64
MOBA_BLOCK = 256
MOBA_TOPK = 3
REL_BUCKETS = 32
REL_MAX_DIST = 128
PAGE_SIZE = 128
DN_ALPHA = (2 * DEPTH) ** 0.25
LN_EPS = 1e-5
NEG = -1e30

LANES = 128
SUBLANES = 8
MXU_WIDTH = 256
VMEM_LIMIT_BYTES = 56 * 1024 * 1024


def _cparams(*sem):
    return pltpu.CompilerParams(dimension_semantics=sem, vmem_limit_bytes=VMEM_LIMIT_BYTES)


def _round_up(n, m):
    return (n + m - 1) // m * m


def _bdot(a, b):
    return jnp.dot(a.astype(BF16), b.astype(BF16), preferred_element_type=F32)


def _bdot_nt(a, b):
    return lax.dot_general(a.astype(BF16), b.astype(BF16), (((1,), (1,)), ((), ())),
                           preferred_element_type=F32)


def _bdot_tn(a, b):
    return lax.dot_general(a.astype(BF16), b.astype(BF16), (((0,), (0,)), ((), ())),
                           preferred_element_type=F32)


def _fdot(a, b):
    return jnp.dot(a, b, preferred_element_type=F32, precision=HI)


def _block_tril3(r, c):
    row = lax.broadcasted_iota(jnp.int32, (r, 3 * r), 0)
    col = lax.broadcasted_iota(jnp.int32, (r, 3 * r), 1) % r
    return jnp.where(((row // c) == (col // c)) & (row >= col), 1.0, 0.0).astype(BF16)


def _prefix_dot(sel3, x):
    hi = x.astype(BF16)
    rest = x - hi.astype(F32)
    mid = rest.astype(BF16)
    lo = (rest - mid.astype(F32)).astype(BF16)
    return jnp.dot(sel3, jnp.concatenate([hi, mid, lo], axis=0), preferred_element_type=F32)


def _sigmoid(x):
    return 1.0 / (1.0 + jnp.exp(-x))


def _silu(x):
    return x * _sigmoid(x)


def _layer_norm_rows(z, g, b):
    mu = jnp.mean(z, axis=-1, keepdims=True)
    zc = z - mu
    var = jnp.mean(zc * zc, axis=-1, keepdims=True)
    return zc * lax.rsqrt(var + LN_EPS) * g + b


def _ada_kernel(c_ref, w_ref, b_ref, o_ref):
    o_ref[...] = _bdot(_silu(c_ref[...]), w_ref[...]) + b_ref[...]


def _ada_mods(c_all, w_ada, b_ada):
    depth, d, n = w_ada.shape
    nb = c_all.shape[0]
    tn = n // 8
    assert n % 8 == 0 and tn % LANES == 0
    return pl.pallas_call(
        _ada_kernel,
        grid=(depth, n // tn),
        in_specs=[pl.BlockSpec((nb, d), lambda l, j: (0, 0)),
                  pl.BlockSpec((None, d, tn), lambda l, j: (l, 0, j)),
                  pl.BlockSpec((None, 1, tn), lambda l, j: (l, 0, j))],
        out_specs=pl.BlockSpec((None, nb, tn), lambda l, j: (l, 0, j)),
        out_shape=jax.ShapeDtypeStruct((depth, nb, n), F32),
        name="ada_mods",
        compiler_params=_cparams("parallel", "parallel"),
    )(c_all, w_ada, b_ada.reshape(depth, 1, n))


MAX_ROW_TILE = 1024


def _row_tile(batch, length):
    if length >= MXU_WIDTH:
        tm = MAX_ROW_TILE
        while length % tm:
            tm //= 2
        return tm, length // tm, True
    rows = batch * length
    assert rows % SUBLANES == 0 and rows <= MAX_ROW_TILE
    return rows, 1, False


def _mod_layout(m, batch, length):
    tm, _, per_batch = _row_tile(batch, length)
    mt = jnp.transpose(m, (1, 0, 2))
    if per_batch:
        return mt[:, :, None, :]
    return jnp.repeat(mt, length, axis=1)[:, None, :, :]


def _mod_spec(mod, tiles_per_block):
    _, _, rm, d = mod.shape
    return pl.BlockSpec((9, None, rm, d), lambda i, *_: (0, i // tiles_per_block, 0, 0))


FF_CHUNK = MXU_WIDTH


def _ffn_kernel(x_ref, mod_ref, wa_ref, wu_ref, wd_ref, g_ref, b_ref, o_ref, h_ref, *, j, n_chunks):
    x = x_ref[...]
    h_ref[...] = (x * (1.0 + mod_ref[3 * j + 1]) + mod_ref[3 * j]).astype(BF16)
    o_ref[...] = jnp.zeros_like(o_ref)

    def body(c, carry):
        c0 = pl.multiple_of(c * FF_CHUNK, FF_CHUNK)
        h = h_ref[...]
        a = jnp.dot(h, wa_ref[:, pl.ds(c0, FF_CHUNK)], preferred_element_type=F32)
        u = jnp.dot(h, wu_ref[:, pl.ds(c0, FF_CHUNK)], preferred_element_type=F32)
        act = (_silu(a) * u).astype(BF16)
        o_ref[...] += jnp.dot(act, wd_ref[pl.ds(c0, FF_CHUNK), :], preferred_element_type=F32)
        return carry

    lax.fori_loop(0, n_chunks, body, 0)
    z = DN_ALPHA * x + (1.0 + mod_ref[3 * j + 2]) * (0.5 * o_ref[...])
    o_ref[...] = _layer_norm_rows(z, g_ref[...], b_ref[...])


def _ffn_block(x2d, mod, tiling, j, weights, layer, ln_g, ln_b):
    tm, tpb, _ = tiling
    rows, d = x2d.shape
    w_a, w_u, w_down = weights
    fp = w_a.shape[-1]
    kern = functools.partial(_ffn_kernel, j=j, n_chunks=fp // FF_CHUNK)
    return pl.pallas_call(
        kern,
        grid=(rows // tm,),
        in_specs=[pl.BlockSpec((tm, d), lambda i: (i, 0)),
                  _mod_spec(mod, tpb),
                  pl.BlockSpec((None, d, fp), lambda i: (layer, 0, 0), pipeline_mode=pl.Buffered(1)),
                  pl.BlockSpec((None, d, fp), lambda i: (layer, 0, 0), pipeline_mode=pl.Buffered(1)),
                  pl.BlockSpec((None, fp, d), lambda i: (layer, 0, 0), pipeline_mode=pl.Buffered(1)),
                  pl.BlockSpec((1, d), lambda i: (0, 0)),
                  pl.BlockSpec((1, d), lambda i: (0, 0))],
        out_specs=pl.BlockSpec((tm, d), lambda i: (i, 0)),
        out_shape=jax.ShapeDtypeStruct((rows, d), F32),
        scratch_shapes=[pltpu.VMEM((tm, d), BF16)],
        name="ffn_block",
        compiler_params=_cparams("parallel"),
    )(x2d, mod, w_a, w_u, w_down, ln_g.reshape(1, d), ln_b.reshape(1, d))


def _prep_ffn_weights(w_up, w_down):
    d_ff = w_up.shape[2] // 2
    fp = _round_up(d_ff, FF_CHUNK)
    pad_cols = lambda w: jnp.pad(w.astype(BF16), ((0, 0), (0, 0), (0, fp - d_ff)))
    down = jnp.pad(w_down.astype(BF16), ((0, 0), (0, fp - d_ff), (0, 0)))
    return pad_cols(w_up[:, :, :d_ff]), pad_cols(w_up[:, :, d_ff:]), down


PROJ_ROW_TILE = 512


def _proj_in_kernel(x_ref, mod_ref, w_ref, o_ref, *, j, tn):
    h = (x_ref[...] * (1.0 + mod_ref[3 * j + 1]) + mod_ref[3 * j]).astype(BF16)
    for k in range(w_ref.shape[1] // tn):
        cols = slice(k * tn, (k + 1) * tn)
        o_ref[:, cols] = jnp.dot(h, w_ref[:, cols], preferred_element_type=F32)


def _proj_in(x2d, mod, tiling, j, w, tn):
    tm, tpb, per_batch = tiling
    if per_batch and tm % PROJ_ROW_TILE == 0:
        tpb = tpb * (tm // PROJ_ROW_TILE)
        tm = PROJ_ROW_TILE
    rows, d = x2d.shape
    n = w.shape[1]
    assert n % tn == 0
    return pl.pallas_call(
        functools.partial(_proj_in_kernel, j=j, tn=tn),
        grid=(rows // tm,),
        in_specs=[pl.BlockSpec((tm, d), lambda i: (i, 0)),
                  _mod_spec(mod, tpb),
                  pl.BlockSpec((d, n), lambda i: (0, 0), pipeline_mode=pl.Buffered(1))],
        out_specs=pl.BlockSpec((tm, n), lambda i: (i, 0)),
        out_shape=jax.ShapeDtypeStruct((rows, n), F32),
        name="proj_in",
        compiler_params=_cparams("parallel"),
    )(x2d, mod, w)


def _proj_out_kernel(o_ref, x_ref, mod_ref, w_ref, g_ref, b_ref, y_ref, *, j):
    y = jnp.dot(o_ref[...].astype(BF16), w_ref[...], preferred_element_type=F32)
    z = DN_ALPHA * x_ref[...] + (1.0 + mod_ref[3 * j + 2]) * y
    y_ref[...] = _layer_norm_rows(z, g_ref[...], b_ref[...])


def _proj_out(o2d, x2d, mod, tiling, j, w, ln_g, ln_b):
    tm, tpb, _ = tiling
    rows, d = x2d.shape
    k = o2d.shape[1]
    return pl.pallas_call(
        functools.partial(_proj_out_kernel, j=j),
        grid=(rows // tm,),
        in_specs=[pl.BlockSpec((tm, k), lambda i: (i, 0)),
                  pl.BlockSpec((tm, d), lambda i: (i, 0)),
                  _mod_spec(mod, tpb),
                  pl.BlockSpec((k, d), lambda i: (0, 0)),
                  pl.BlockSpec((1, d), lambda i: (0, 0)),
                  pl.BlockSpec((1, d), lambda i: (0, 0))],
        out_specs=pl.BlockSpec((tm, d), lambda i: (i, 0)),
        out_shape=jax.ShapeDtypeStruct((rows, d), F32),
        name="proj_out",
        compiler_params=_cparams("parallel"),
    )(o2d, x2d, mod, w, ln_g.reshape(1, d), ln_b.reshape(1, d))


def _tri(c, strict=False):
    r = lax.broadcasted_iota(jnp.int32, (c, c), 0)
    s = lax.broadcasted_iota(jnp.int32, (c, c), 1)
    return (r > s) if strict else (r >= s)


def _hgrn_diag_masks(c):
    row = lax.broadcasted_iota(jnp.int32, (c, c), 0)
    col = lax.broadcasted_iota(jnp.int32, (c, c), 1)
    return [(col == row - d) & ((row % SUBLANES) >= d) for d in range(SUBLANES)]


def _hgrn_chunk(q, k, cum, v, st_prev, c, diag_masks):
    hs = range(len(q))
    nb = c // SUBLANES
    last_row = [cum[h][c - 1:c, :] for h in hs]
    o_state = [_bdot_nt(q[h] * jnp.exp(cum[h]), st_prev[h]) for h in hs]
    q3 = [q[h].reshape(nb, SUBLANES, LANES) for h in hs]
    k3 = [k[h].reshape(nb, SUBLANES, LANES) for h in hs]
    cum3 = [cum[h].reshape(nb, SUBLANES, LANES) for h in hs]

    att_off = None
    if nb > 1:
        blk = lax.broadcasted_iota(jnp.int32, (nb, SUBLANES, LANES), 0)
        q_cat, k_cat = [], []
        for h in hs:
            e = cum3[h][:, SUBLANES - 1:SUBLANES, :]
            k_t = k3[h] * jnp.exp(e - cum3[h])
            q_parts, k_parts = [], []
            for jb in range(nb - 1):
                qj = q3[h][jb + 1:] * jnp.exp(cum3[h][jb + 1:] - e[jb:jb + 1])
                qj = jnp.concatenate([jnp.zeros((jb + 1, SUBLANES, LANES), F32), qj], axis=0)
                q_parts.append(qj.reshape(c, LANES).astype(BF16))
                k_parts.append(jnp.where(blk == jb, k_t, 0.0).reshape(c, LANES).astype(BF16))
            q_cat.append(jnp.concatenate(q_parts, axis=1))
            k_cat.append(jnp.concatenate(k_parts, axis=1))
        att_off = [_bdot_nt(q_cat[h], k_cat[h]) for h in hs]

    att = [jnp.zeros((c, c), F32) for _ in hs]
    for d in reversed(range(SUBLANES)):
        for h in hs:
            if d == 0:
                p = q3[h] * k3[h]
            else:
                p = q3[h] * pltpu.roll(k3[h], d, 1) * jnp.exp(cum3[h] - pltpu.roll(cum3[h], d, 1))
            a_d = jnp.sum(p.reshape(c, LANES), axis=-1, keepdims=True)
            att[h] = jnp.where(diag_masks[d], a_d, att[h])
    if att_off is not None:
        att = [att[h] + att_off[h] for h in hs]

    o = [o_state[h] + _bdot(att[h], v[h]) for h in hs]
    st_new = [st_prev[h] * jnp.exp(last_row[h]) + _bdot_tn(v[h], k[h] * jnp.exp(last_row[h] - cum[h])) for h in hs]
    return o, st_new


REC_L_TILE = 512
HGRN_HEADS_PER_STEP = 8
GDN_HEADS_PER_STEP = 8
GDN_STACK = 4


def _head_lanes(hh):
    return slice(hh * HEAD_DIM, (hh + 1) * HEAD_DIM)


def _rec_tiling(l):
    c = min(CHUNK, l)
    tl = min(REC_L_TILE, l)
    assert l % tl == 0 and tl % c == 0 and c % SUBLANES == 0
    return c, tl


def _hgrn_kernel(q_ref, f_ref, v_ref, g_ref, lb_ref, ng_ref, s0_ref, o_ref, s_ref, *, c, n_chunks, hb):
    @pl.when(pl.program_id(2) == 0)
    def _():
        for hh in range(hb):
            s_ref[hh] = s0_ref[hh].T

    ng = ng_ref[...]
    lb = lb_ref[...]
    tril3 = _block_tril3(c, c)
    diag_masks = _hgrn_diag_masks(c)

    def body(i, carry):
        rows = pl.ds(pl.multiple_of(i * c, c), c)
        fr = f_ref[rows, :]
        q_all = _silu(q_ref[rows, :]) * (HEAD_DIM ** -0.5)
        k_all = (1.0 - lb) * _sigmoid(-fr)
        cum_all = _prefix_dot(tril3, jnp.log(lb + (1.0 - lb) * _sigmoid(fr)))
        lanes = [_head_lanes(hh) for hh in range(hb)]
        o, st_new = _hgrn_chunk([q_all[:, ln] for ln in lanes], [k_all[:, ln] for ln in lanes],
                                [cum_all[:, ln] for ln in lanes], [v_ref[rows, ln] for ln in lanes],
                                [s_ref[hh] for hh in range(hb)], c, diag_masks)
        for hh, ln in enumerate(lanes):
            s_ref[hh] = st_new[hh]
            ms = jnp.mean(o[hh] * o[hh], axis=-1, keepdims=True)
            o_ref[rows, ln] = o[hh] * lax.rsqrt(ms + 1e-6) * ng * _sigmoid(g_ref[rows, ln])
        return carry

    lax.fori_loop(0, n_chunks, body, 0)

    @pl.when(pl.program_id(2) == pl.num_programs(2) - 1)
    def _():
        for hh in range(hb):
            s_ref[hh] = s_ref[hh].T


def _hgrn_recurrence(proj, s0, lb, norm_g):
    b, l, _ = proj.shape
    c, tl = _rec_tiling(l)
    hb = HGRN_HEADS_PER_STEP
    ng_ = HEADS // hb
    col = lambda off: pl.BlockSpec((None, tl, hb * HEAD_DIM), lambda bi, hg, t: (bi, t, off * ng_ + hg))
    state = pl.BlockSpec((None, hb, HEAD_DIM, HEAD_DIM), lambda bi, hg, t: (bi, hg, 0, 0))
    return pl.pallas_call(
        functools.partial(_hgrn_kernel, c=c, n_chunks=tl // c, hb=hb),
        grid=(b, ng_, l // tl),
        in_specs=[col(0), col(1), col(2), col(3),
                  pl.BlockSpec((1, hb * HEAD_DIM), lambda bi, hg, t: (0, hg)),
                  pl.BlockSpec((1, HEAD_DIM), lambda bi, hg, t: (0, 0)),
                  state],
        out_specs=[col(0), state],
        out_shape=[jax.ShapeDtypeStruct((b, l, HEADS * HEAD_DIM), F32),
                   jax.ShapeDtypeStruct((b, HEADS, HEAD_DIM, HEAD_DIM), F32)],
        name="hgrn_recurrence",
        compiler_params=_cparams("parallel", "parallel", "arbitrary"),
    )(proj, proj, proj, proj, lb.reshape(1, -1), norm_g.reshape(1, -1), s0)


HIST = SUBLANES


def _gdn_conv(x_ext, cw, c):
    acc = x_ext * cw[CONV_W - 1:CONV_W, :]
    for w in range(CONV_W - 1):
        acc = acc + pltpu.roll(x_ext, CONV_W - 1 - w, 0) * cw[w:w + 1, :]
    return _silu(acc[HIST:, :])


def _l2norm_rows(x):
    return x * lax.rsqrt(jnp.sum(x * x, axis=-1, keepdims=True) + 1e-6)


def _gdn_masks(r, c):
    row = lax.broadcasted_iota(jnp.int32, (r, r), 0)
    col = lax.broadcasted_iota(jnp.int32, (r, r), 1)
    same = (row // c) == (col // c)
    low = same & (row >= col)
    return low, same & (row > col), jnp.where(row == col, 1.0, 0.0), _block_tril3(r, c)


def _gdn_chunk(groups, c, masks):
    low, strict, eye, tril3 = masks
    gr = range(len(groups))
    qs, ks, vs, betas, gs, s_prev = zip(*groups)
    n = len(s_prev[0])
    r = n * c
    pre = [_prefix_dot(tril3, jnp.concatenate(
        [jnp.where(strict, jnp.broadcast_to(gs[g], (r, r)), 0.0), jnp.broadcast_to(gs[g], (r, LANES))], axis=1))
        for g in gr]
    cum = [pre[g][:, r:] for g in gr]
    decay = [jnp.where(low, jnp.exp(jnp.where(low, pre[g][:, :r], 0.0)), 0.0) for g in gr]
    ecum = [jnp.exp(cum[g]) for g in gr]
    kb = [ks[g] * betas[g] for g in gr]
    kk = [_bdot_nt(jnp.concatenate([kb[g], qs[g]], axis=0), ks[g]) for g in gr]
    p = [-jnp.where(strict, kk[g][:r] * decay[g], 0.0) for g in gr]
    att = [kk[g][r:] * decay[g] for g in gr]
    t_inv = [eye + p[g] for g in gr]
    x = p
    for _ in range(int(math.log2(c)) - 1):
        x = [_bdot(x[g], x[g]) for g in gr]
        t_inv = [t_inv[g] + _bdot(x[g], t_inv[g]) for g in gr]
    uw = [_bdot(t_inv[g], jnp.concatenate([vs[g] * betas[g], kb[g] * ecum[g]], axis=1)) for g in gr]
    qe = [qs[g] * ecum[g] for g in gr]
    head_rows = [slice(h * c, (h + 1) * c) for h in range(n)]
    ws = [[_bdot(jnp.concatenate([uw[g][rows, HEAD_DIM:], qe[g][rows]], axis=0), s_prev[g][h])
           for h, rows in enumerate(head_rows)] for g in gr]
    v_new = [jnp.concatenate([uw[g][rows, :HEAD_DIM] - ws[g][h][:c] for h, rows in enumerate(head_rows)], axis=0)
             for g in gr]
    o = [jnp.concatenate([ws[g][h][c:] for h in range(n)], axis=0) + _bdot(att[g], v_new[g]) for g in gr]
    lasts = [[cum[g][(h + 1) * c - 1:(h + 1) * c, :] for h in range(n)] for g in gr]
    kt = [ks[g] * jnp.exp(jnp.concatenate([jnp.broadcast_to(l, (c, LANES)) for l in lasts[g]], axis=0) - cum[g])
          for g in gr]
    s_new = [[s_prev[g][h] * jnp.exp(lasts[g][h]) + _bdot_tn(kt[g][rows], v_new[g][rows])
              for h, rows in enumerate(head_rows)] for g in gr]
    return [(o[g], s_new[g]) for g in gr]


def _gdn_kernel(q_ref, k_ref, v_ref, z_ref, ba_ref, c0q_ref, c0k_ref, c0v_ref, cwq_ref, cwk_ref, cwv_ref,
                an_ref, dt_ref, ng_ref, s0_ref, o_ref, s_ref, qx_ref, kx_ref, vx_ref, *, c, n_chunks, tl, hb):
    hg = pl.program_id(1)
    t = pl.program_id(2)
    staged = ((q_ref, c0q_ref, qx_ref), (k_ref, c0k_ref, kx_ref), (v_ref, c0v_ref, vx_ref))

    @pl.when(t == 0)
    def _():
        s_ref[...] = s0_ref[...]
        for _, c0, dst in staged:
            dst[0:HIST, :] = jnp.zeros((HIST, hb * HEAD_DIM), F32)
            dst[HIST - (CONV_W - 1):HIST, :] = c0[...]

    @pl.when(t > 0)
    def _():
        for _, _, dst in staged:
            dst[0:HIST, :] = dst[tl:tl + HIST, :]

    for src, _, dst in staged:
        dst[HIST:HIST + tl, :] = src[...]
    ng = ng_ref[...]
    lane = lax.broadcasted_iota(jnp.int32, (c, LANES), 1)
    masks = _gdn_masks(GDN_STACK * c, c)

    def body(i, carry):
        r0 = pl.multiple_of(i * c, c)
        ext = pl.ds(r0, c + HIST)
        rows = pl.ds(r0, c)
        ba = ba_ref[rows, :]
        stack = lambda parts: jnp.concatenate(parts, axis=0)
        groups = []
        for h0 in range(0, hb, GDN_STACK):
            qs, ks, vs, betas, gs = [], [], [], [], []
            for hh in range(h0, h0 + GDN_STACK):
                ln = _head_lanes(hh)
                head = hg * hb + hh
                qs.append(_l2norm_rows(_gdn_conv(qx_ref[ext, ln], cwq_ref[:, ln], c)) * (HEAD_DIM ** -0.5))
                ks.append(_l2norm_rows(_gdn_conv(kx_ref[ext, ln], cwk_ref[:, ln], c)))
                vs.append(_gdn_conv(vx_ref[ext, ln], cwv_ref[:, ln], c))
                betas.append(_sigmoid(jnp.sum(jnp.where(lane == head, ba, 0.0), axis=-1, keepdims=True)))
                a_raw = jnp.sum(jnp.where(lane == HEADS + head, ba, 0.0), axis=-1, keepdims=True)
                xa = a_raw + dt_ref[:, ln][:, 0:1]
                softplus = jnp.maximum(xa, 0.0) + jnp.log(1.0 + jnp.exp(-jnp.abs(xa)))
                gs.append(an_ref[:, ln][:, 0:1] * softplus)
            groups.append((stack(qs), stack(ks), stack(vs), stack(betas), stack(gs),
                           [s_ref[hh] for hh in range(h0, h0 + GDN_STACK)]))
        for gi, (o, s_new) in enumerate(_gdn_chunk(groups, c, masks)):
            for j in range(GDN_STACK):
                hh = gi * GDN_STACK + j
                ln = _head_lanes(hh)
                s_ref[hh] = s_new[j]
                oh = o[j * c:(j + 1) * c]
                ms = jnp.mean(oh * oh, axis=-1, keepdims=True)
                o_ref[rows, ln] = oh * lax.rsqrt(ms + 1e-6) * ng * _silu(z_ref[rows, ln])
        return carry

    lax.fori_loop(0, n_chunks, body, 0)


def _gdn_recurrence(proj, conv0, s0, conv_w, a_log, dt_bias, norm_g):
    b, l, _ = proj.shape
    c, tl = _rec_tiling(l)
    hb = GDN_HEADS_PER_STEP
    ng_ = HEADS // hb
    w = hb * HEAD_DIM
    col = lambda off: pl.BlockSpec((None, tl, w), lambda bi, hg, t: (bi, t, off * ng_ + hg))
    c0 = lambda off: pl.BlockSpec((None, CONV_W - 1, w), lambda bi, hg, t: (bi, 0, off * ng_ + hg))
    cw = lambda off: pl.BlockSpec((CONV_W, w), lambda bi, hg, t: (0, off * ng_ + hg))
    per_head = pl.BlockSpec((1, w), lambda bi, hg, t: (0, hg))
    state = pl.BlockSpec((None, hb, HEAD_DIM, HEAD_DIM), lambda bi, hg, t: (bi, hg, 0, 0))
    a_neg = jnp.repeat(-jnp.exp(a_log.astype(F32)), HEAD_DIM)[None, :]
    dt_b = jnp.repeat(dt_bias.astype(F32), HEAD_DIM)[None, :]
    xbuf = pltpu.VMEM((HIST + tl, w), F32)
    return pl.pallas_call(
        functools.partial(_gdn_kernel, c=c, n_chunks=tl // c, tl=tl, hb=hb),
        grid=(b, ng_, l // tl),
        in_specs=[col(0), col(1), col(2), col(3),
                  pl.BlockSpec((None, tl, LANES), lambda bi, hg, t: (bi, t, 4 * HEADS)),
                  c0(0), c0(1), c0(2), cw(0), cw(1), cw(2),
                  per_head, per_head,
                  pl.BlockSpec((1, HEAD_DIM), lambda bi, hg, t: (0, 0)),
                  state],
        out_specs=[col(0), state],
        out_shape=[jax.ShapeDtypeStruct((b, l, HEADS * HEAD_DIM), F32),
                   jax.ShapeDtypeStruct((b, HEADS, HEAD_DIM, HEAD_DIM), F32)],
        scratch_shapes=[xbuf, xbuf, xbuf],
        name="gdn_recurrence",
        compiler_params=_cparams("parallel", "parallel", "arbitrary"),
    )(proj, proj, proj, proj, proj, conv0, conv0, conv0, conv_w, conv_w, conv_w,
      a_neg, dt_b, norm_g.reshape(1, -1), s0)


def _rel_bucket(dist):
    exact = REL_BUCKETS // 2
    d = jnp.maximum(dist, exact).astype(F32)
    large = exact + (jnp.log(d / exact) / math.log(REL_MAX_DIST / exact) * (REL_BUCKETS - exact)).astype(jnp.int32)
    return jnp.where(dist < exact, dist, jnp.minimum(large, REL_BUCKETS - 1))


def _top_mask(gate, n_valid, k, axis):
    idx = lax.broadcasted_iota(jnp.int32, gate.shape, axis)
    big = jnp.int32(2 ** 30)
    work = jnp.where(idx < n_valid, gate, NEG)
    sel = jnp.zeros(gate.shape, F32)
    for r in range(k):
        mx = jnp.max(work, axis=axis, keepdims=True)
        first = jnp.min(jnp.where(work == mx, idx, big), axis=axis, keepdims=True)
        pick = (idx == first) & (idx < n_valid)
        sel = jnp.where(pick, 1.0, sel)
        work = jnp.where(idx == first, -jnp.inf, work)
    return sel


MOBA_HEADS_PER_STEP = 4


def _moba_prompt_kernel(q_ref, k_ref, v_ref, bd_ref, bp_ref, bf_ref, o_ref, means_ref, sel_ref, *, nb, hb):
    i = pl.program_id(2)
    blk = MOBA_BLOCK

    @pl.when(i == 0)
    def _():
        means_ref[...] = jnp.zeros_like(means_ref)
        for hh in range(hb):
            for n in range(nb):
                means_ref[hh, n:n + 1, :] = jnp.mean(k_ref[n * blk:(n + 1) * blk, _head_lanes(hh)],
                                                     axis=0, keepdims=True)

    causal = ~_tri(blk, strict=True)
    r0 = pl.multiple_of(i * blk, blk)
    heads = range(hb)
    qs = [q_ref[:, _head_lanes(hh)] * (HEAD_DIM ** -0.5) for hh in heads]
    qbs = [q.astype(BF16) for q in qs]
    gates = [lax.dot_general(means_ref[hh], qs[hh], (((1,), (1,)), ((), ())),
                             preferred_element_type=F32, precision=HI) for hh in heads]
    qk = [_bdot_nt(k_ref[pl.ds(r0, blk), _head_lanes(hh)], qbs[hh]) for hh in heads]
    for hh in heads:
        sel_ref[hh] = _top_mask(gates[hh], i, MOBA_TOPK, 0)
    s0 = [jnp.where(causal, qk[hh] + bd_ref[hh], NEG) for hh in heads]
    m0 = [jnp.max(s, axis=0, keepdims=True) for s in s0]
    p0 = [jnp.exp(s0[hh] - m0[hh]) for hh in heads]
    init = [(m0[hh], jnp.sum(p0[hh], axis=0, keepdims=True),
             _bdot_tn(v_ref[pl.ds(r0, blk), _head_lanes(hh)], p0[hh])) for hh in heads]

    def past_block(n, carry, bias_of_head):
        rn = pl.multiple_of(n * blk, blk)
        heads = range(hb)
        qk = [_bdot_nt(k_ref[pl.ds(rn, blk), _head_lanes(hh)], qbs[hh]) for hh in heads]
        sn = [jnp.where(sel_ref[hh, pl.ds(n, 1), :] > 0.0, qk[hh] + bias_of_head(hh), NEG) for hh in heads]
        m_new = [jnp.maximum(carry[hh][0], jnp.max(sn[hh], axis=0, keepdims=True)) for hh in heads]
        pn = [jnp.exp(sn[hh] - m_new[hh]) for hh in heads]
        pv = [_bdot_tn(v_ref[pl.ds(rn, blk), _head_lanes(hh)], pn[hh]) for hh in heads]
        out = []
        for hh in heads:
            m, l, acc = carry[hh]
            alpha = jnp.exp(m - m_new[hh])
            out.append((m_new[hh], l * alpha + jnp.sum(pn[hh], axis=0, keepdims=True), acc * alpha + pv[hh]))
        return tuple(out)

    carry = lax.fori_loop(0, jnp.maximum(i - 1, 0),
                          lambda n, cr: past_block(n, cr, lambda hh: bf_ref[:, _head_lanes(hh)][:, 0:1]),
                          tuple(init))
    final = past_block(jnp.maximum(i - 1, 0), carry, lambda hh: bp_ref[hh])
    for hh in range(hb):
        _, l, acc = final[hh]
        o_ref[:, _head_lanes(hh)] = (acc / l).T


def _bias_of_distance(rel_bias, dist):
    bucket = _rel_bucket(jnp.maximum(dist, 0))[None]
    out = jnp.zeros((HEADS,) + dist.shape, F32)
    for b in range(REL_BUCKETS):
        out = jnp.where(bucket == b, rel_bias[b].astype(F32).reshape((HEADS,) + (1,) * dist.ndim), out)
    return out


def _moba_bias_tables(rel_bias, blk):
    t = jnp.arange(blk, dtype=jnp.int32)
    d0 = t[None, :] - t[:, None]
    bias_diag = _bias_of_distance(rel_bias, d0)
    bias_prev = _bias_of_distance(rel_bias, d0 + blk)
    far = jnp.repeat(rel_bias[REL_BUCKETS - 1].astype(F32), HEAD_DIM)[None, :]
    return bias_diag, bias_prev, far


def _moba_prompt(proj, rel_bias):
    b, l, _ = proj.shape
    blk = MOBA_BLOCK
    assert l % blk == 0 and l // blk <= LANES
    assert 2 * blk > REL_MAX_DIST
    nb = l // blk
    bias_diag, bias_prev, far = _moba_bias_tables(rel_bias, blk)
    hb = MOBA_HEADS_PER_STEP
    ng_ = HEADS // hb
    w = hb * HEAD_DIM
    tile = pl.BlockSpec((hb, blk, blk), lambda bi, hg, i: (hg, 0, 0))
    return pl.pallas_call(
        functools.partial(_moba_prompt_kernel, nb=nb, hb=hb),
        grid=(b, ng_, nb),
        in_specs=[pl.BlockSpec((None, blk, w), lambda bi, hg, i: (bi, i, hg)),
                  pl.BlockSpec((None, l, w), lambda bi, hg, i: (bi, 0, ng_ + hg)),
                  pl.BlockSpec((None, l, w), lambda bi, hg, i: (bi, 0, 2 * ng_ + hg)),
                  tile, tile,
                  pl.BlockSpec((1, w), lambda bi, hg, i: (0, hg))],
        out_specs=pl.BlockSpec((None, blk, w), lambda bi, hg, i: (bi, i, hg)),
        out_shape=jax.ShapeDtypeStruct((b, l, HEADS * HEAD_DIM), F32),
        scratch_shapes=[pltpu.VMEM((hb, _round_up(nb, SUBLANES), HEAD_DIM), F32),
                        pltpu.VMEM((hb, _round_up(nb, SUBLANES), blk), F32)],
        name="moba_prompt",
        compiler_params=_cparams("parallel", "parallel", "arbitrary"),
    )(proj, proj, proj, bias_diag, bias_prev, far)


def _block_diag_rows(x, lq):
    return jnp.concatenate([x[h * lq:(h + 1) * lq, h * HEAD_DIM:(h + 1) * HEAD_DIM] for h in range(HEADS)], axis=0)


MOBA_SAMPLE_BLOCKS_PER_STEP = 4


def _moba_sample_kernel(pt_ref, *refs, nb, lq, bps):
    n_pages = bps * (MOBA_BLOCK // PAGE_SIZE)
    k_refs, v_refs = refs[:n_pages], refs[n_pages:2 * n_pages]
    qkv_ref, bl_ref, bc_ref, bf_ref, o_ref, qbd_ref, m_ref, l_ref, acc_ref, means_ref = refs[2 * n_pages:]
    step = pl.program_id(1)
    d_model = HEADS * HEAD_DIM
    hq = HEADS * lq

    @pl.when(step == 0)
    def _():
        q = qkv_ref[:, 0:d_model] * (HEAD_DIM ** -0.5)
        rows = lax.broadcasted_iota(jnp.int32, (d_model, hq), 0) // HEAD_DIM
        cols = lax.broadcasted_iota(jnp.int32, (d_model, hq), 1)
        spread = (lax.broadcasted_iota(jnp.int32, (lq, hq), 1) % lq
                  == lax.broadcasted_iota(jnp.int32, (lq, hq), 0)).astype(F32)
        qt = lax.dot_general(q, spread, (((0,), (0,)), ((), ())),
                             preferred_element_type=F32, precision=HI)
        qbd_ref[...] = jnp.where(rows == cols // lq, qt, 0.0)
        means_ref[...] = jnp.zeros_like(means_ref)
        m_ref[...] = jnp.zeros_like(m_ref)
        l_ref[...] = jnp.zeros_like(l_ref)

    qbd = qbd_ref[...]

    def rows_by_lanes(*pages):
        return jnp.concatenate(
            [jnp.concatenate([pg[pl.ds(h, PAGE_SIZE, stride=HEADS), :] for h in range(HEADS)], axis=1)
             for pg in pages], axis=0)

    js = range(bps)
    ns = [step * bps + j for j in js]
    kblk = [rows_by_lanes(k_refs[2 * j], k_refs[2 * j + 1]) for j in js]
    vblk = [rows_by_lanes(v_refs[2 * j], v_refs[2 * j + 1]) for j in js]
    s = [_bdot(kblk[j], qbd) + jnp.where(ns[j] == nb - 1, bl_ref[...], bf_ref[...]) for j in js]
    for j in js:
        means_ref[pl.ds(ns[j], 1), :] = jnp.mean(kblk[j], axis=0, keepdims=True)
    m = [jnp.max(s[j], axis=0, keepdims=True) for j in js]
    p = [jnp.exp(s[j] - m[j]) for j in js]
    pv = [_bdot_tn(p[j], vblk[j]) for j in js]
    for j in js:
        m_ref[pl.ds(ns[j], 1), :] = m[j]
        l_ref[pl.ds(ns[j], 1), :] = jnp.sum(p[j], axis=0, keepdims=True)
        acc_ref[ns[j]] = _block_diag_rows(pv[j], lq)

    @pl.when(step == pl.num_programs(1) - 1)
    def _():
        k_new = qkv_ref[:, d_model:2 * d_model]
        v_new = qkv_ref[:, 2 * d_model:3 * d_model]
        jq = lax.broadcasted_iota(jnp.int32, (lq, hq), 1) % lq
        jk = lax.broadcasted_iota(jnp.int32, (lq, hq), 0)
        sc = jnp.where(jk <= jq, _bdot(k_new, qbd) + bc_ref[...], NEG)
        mc = jnp.max(sc, axis=0, keepdims=True)
        pc = jnp.exp(sc - mc)
        lc = jnp.sum(pc, axis=0, keepdims=True)
        oc = _block_diag_rows(_bdot_tn(pc, v_new), lq)

        gate = _fdot(means_ref[...], qbd)
        sel = _top_mask(gate, nb, MOBA_TOPK, 0)
        m_all = m_ref[...]
        m_tot = jnp.maximum(jnp.max(jnp.where(sel > 0.0, m_all, NEG), axis=0, keepdims=True), mc)
        w = jnp.where(sel > 0.0, jnp.exp(jnp.where(sel > 0.0, m_all - m_tot, 0.0)), 0.0)
        wc = jnp.exp(mc - m_tot)
        denom = jnp.sum(w * l_ref[...], axis=0, keepdims=True) + wc * lc
        nbp = w.shape[0]
        w_all = jnp.concatenate([w, wc, jnp.zeros((SUBLANES - 1, hq), F32)], axis=0) / denom
        ne = nbp + SUBLANES
        eye = (lax.broadcasted_iota(jnp.int32, (ne, ne), 0)
               == lax.broadcasted_iota(jnp.int32, (ne, ne), 1)).astype(F32)
        wt = lax.dot_general(w_all, eye, (((0,), (0,)), ((), ())),
                             preferred_element_type=F32, precision=HI)
        out = wt[:, nbp:nbp + 1] * oc
        for b_i in range(nb):
            out = out + wt[:, b_i:b_i + 1] * acc_ref[b_i]
        o_ref[...] = out


def _moba_sample(proj, cache_k, cache_v, layer, page_table, rel_bias):
    b, lq, _ = proj.shape
    n_pages = page_table.shape[1]
    d_model = HEADS * HEAD_DIM
    pages_per_block = MOBA_BLOCK // PAGE_SIZE
    assert pages_per_block == 2 and n_pages % pages_per_block == 0 and lq <= MOBA_BLOCK
    assert cache_k.shape[2:] == (PAGE_SIZE, HEADS, HEAD_DIM)
    nb = n_pages // pages_per_block
    assert nb >= 1 and MOBA_BLOCK >= REL_MAX_DIST
    nbp = _round_up(nb, SUBLANES)
    hq = HEADS * lq
    jq = jnp.arange(lq, dtype=jnp.int32)
    t = jnp.arange(MOBA_BLOCK, dtype=jnp.int32)
    by_cols = lambda tab: jnp.transpose(tab, (1, 0, 2)).reshape(tab.shape[1], hq)
    bias_last = by_cols(_bias_of_distance(rel_bias, (MOBA_BLOCK + jq)[None, :] - t[:, None]))
    bias_cur = by_cols(_bias_of_distance(rel_bias, jq[None, :] - jq[:, None]))
    bias_far = jnp.repeat(rel_bias[REL_BUCKETS - 1].astype(F32), lq)[None, :]

    n_c, n_pool = cache_k.shape[:2]
    cache_k = cache_k.reshape(n_c, n_pool, PAGE_SIZE * HEADS, HEAD_DIM)
    cache_v = cache_v.reshape(n_c, n_pool, PAGE_SIZE * HEADS, HEAD_DIM)
    bps = math.gcd(MOBA_SAMPLE_BLOCKS_PER_STEP, nb)
    pages_per_step = bps * pages_per_block
    page = lambda off: pl.BlockSpec((None, None, PAGE_SIZE * HEADS, HEAD_DIM),
                                    lambda bi, n, pt: (layer, pt[bi, pages_per_step * n + off], 0, 0))
    pages = [page(off) for off in range(pages_per_step)]
    full = lambda shape: pl.BlockSpec(shape, lambda bi, n, pt: (0,) * len(shape))
    grid_spec = pltpu.PrefetchScalarGridSpec(
        num_scalar_prefetch=1,
        grid=(b, nb // bps),
        in_specs=pages + pages + [
                  pl.BlockSpec((None, lq, 3 * d_model), lambda bi, n, pt: (bi, 0, 0)),
                  full((MOBA_BLOCK, hq)), full((lq, hq)), full((1, hq))],
        out_specs=pl.BlockSpec((None, hq, HEAD_DIM), lambda bi, n, pt: (bi, 0, 0)),
        scratch_shapes=[pltpu.VMEM((d_model, hq), F32),
                        pltpu.VMEM((nbp, hq), F32),
                        pltpu.VMEM((nbp, hq), F32),
                        pltpu.VMEM((nb, hq, HEAD_DIM), F32),
                        pltpu.VMEM((nbp, d_model), F32)])
    return pl.pallas_call(
        functools.partial(_moba_sample_kernel, nb=nb, lq=lq, bps=bps),
        grid_spec=grid_spec,
        out_shape=jax.ShapeDtypeStruct((b, hq, HEAD_DIM), F32),
        name="moba_sample",
        compiler_params=_cparams("parallel", "arbitrary"),
    )(page_table, *([cache_k] * pages_per_step), *([cache_v] * pages_per_step), proj,
      bias_last, bias_cur, bias_far)


def _trunk(x, mods, s_hgrn, s_conv, s_gdn, paged, W):
    b, l, d = x.shape
    tiling = _row_tile(b, l)
    x2 = x.reshape(b * l, d)
    new_h, new_c, new_g, new_k, new_v = [], [], [], [], []
    for i in range(DEPTH):
        mod = _mod_layout(mods[i], b, l)
        x2 = _ffn_block(x2, mod, tiling, 0, W['ffn1'], i, W['ln_g'][i, 0], W['ln_b'][i, 0])
        kind, li = i % N_MIXERS, i // N_MIXERS
        if kind == 0:
            proj = _proj_in(x2, mod, tiling, 1, W['hgrn_w_in'][li], 1024)
            o, s = _hgrn_recurrence(proj.reshape(b, l, -1), s_hgrn[li], W['hgrn_lb'][li], W['hgrn_norm_g'][li])
            new_h.append(s)
            w_o = W['hgrn_w_o'][li]
            o2 = o.reshape(b * l, d)
        elif kind == 1:
            proj = _proj_in(x2, mod, tiling, 1, W['gdn_w_in'][li], W['gdn_w_in'][li].shape[1] // 3)
            proj = proj.reshape(b, l, -1)
            o, s = _gdn_recurrence(proj, s_conv[li], s_gdn[li], W['gdn_conv_w'][li], W['gdn_a_log'][li],
                                   W['gdn_dt_bias'][li], W['gdn_norm_g'][li])
            n_qkv = 3 * HEADS * HEAD_DIM
            assert l >= CONV_W - 1
            new_c.append(proj[:, l - (CONV_W - 1):, :n_qkv])
            new_g.append(s)
            w_o = W['gdn_w_o'][li]
            o2 = o.reshape(b * l, d)
        else:
            proj = _proj_in(x2, mod, tiling, 1, W['moba_w_qkv'][li], 1024).reshape(b, l, -1)
            if paged is None:
                o2 = _moba_prompt(proj, W['rel_bias']).reshape(b * l, d)
            else:
                ck, cv, pt = paged
                o = _moba_sample(proj, ck, cv, li, pt, W['rel_bias'])
                o2 = jnp.transpose(o.reshape(b, HEADS, l, HEAD_DIM), (0, 2, 1, 3)).reshape(b * l, d)
            new_k.append(proj[:, :, d:2 * d].reshape(b, l, HEADS, HEAD_DIM))
            new_v.append(proj[:, :, 2 * d:3 * d].reshape(b, l, HEADS, HEAD_DIM))
            w_o = W['moba_w_o'][li]
        x2 = _proj_out(o2, x2, mod, tiling, 1, w_o, W['ln_g'][i, 1], W['ln_b'][i, 1])
        x2 = _ffn_block(x2, mod, tiling, 2, W['ffn2'], i, W['ln_g'][i, 2], W['ln_b'][i, 2])
    return (x2.reshape(b, l, d), jnp.stack(new_h), jnp.stack(new_c), jnp.stack(new_g),
            jnp.stack(new_k), jnp.stack(new_v))


def kernel(x_prompt, x_sample, state_hgrn, state_gdn_conv, state_gdn, cache_k, cache_v, page_table, c_prompt, c_sample, w_ada, b_ada, ln_g, ln_b, w_ffn1_up, w_ffn1_down, w_ffn2_up, w_ffn2_down, hgrn_w_in, hgrn_lb_logits, hgrn_norm_g, hgrn_w_o, gdn_w_in, gdn_conv_w, gdn_a_log, gdn_dt_bias, gdn_norm_g, gdn_w_o, moba_w_qkv, moba_w_o, rel_bias):
    bp, _, d = x_prompt.shape
    bs = x_sample.shape[0]
    assert d == HEADS * HEAD_DIM

    nc = _round_up(bp + bs, SUBLANES)
    c_all = jnp.pad(jnp.concatenate([c_prompt, c_sample], axis=0), ((0, nc - bp - bs), (0, 0)))
    mods = _ada_mods(c_all, w_ada, b_ada).reshape(DEPTH, nc, 9, d)

    gdn_n = _round_up(gdn_w_in.shape[2], 3 * LANES)
    lbp = jax.nn.softmax(hgrn_lb_logits.astype(F32), axis=0)
    W = {
        'ffn1': _prep_ffn_weights(w_ffn1_up, w_ffn1_down), 'ffn2': _prep_ffn_weights(w_ffn2_up, w_ffn2_down),
        'ln_g': ln_g, 'ln_b': ln_b,
        'hgrn_w_in': hgrn_w_in.astype(BF16), 'hgrn_lb': jnp.cumsum(lbp, axis=0) - lbp[0],
        'hgrn_norm_g': hgrn_norm_g, 'hgrn_w_o': hgrn_w_o.astype(BF16),
        'gdn_w_in': jnp.pad(gdn_w_in, ((0, 0), (0, 0), (0, gdn_n - gdn_w_in.shape[2]))).astype(BF16),
        'gdn_conv_w': gdn_conv_w, 'gdn_a_log': gdn_a_log, 'gdn_dt_bias': gdn_dt_bias,
        'gdn_norm_g': gdn_norm_g, 'gdn_w_o': gdn_w_o.astype(BF16),
        'moba_w_qkv': moba_w_qkv.astype(BF16), 'moba_w_o': moba_w_o.astype(BF16), 'rel_bias': rel_bias,
    }

    n_a, n_b = state_hgrn.shape[0], state_gdn.shape[0]
    z_hgrn = jnp.zeros((n_a, bp) + state_hgrn.shape[2:], F32)
    z_conv = jnp.zeros((n_b, bp) + state_gdn_conv.shape[2:], F32)
    z_gdn = jnp.zeros((n_b, bp) + state_gdn.shape[2:], F32)
    y_p, hgrn_p, conv_p, gdn_p, k_p, v_p = _trunk(x_prompt, mods[:, :bp], z_hgrn, z_conv, z_gdn, None, W)

    y_s, hgrn_s, conv_s, gdn_s, k_s, v_s = _trunk(x_sample, mods[:, bp:bp + bs], state_hgrn, state_gdn_conv,
                                                  state_gdn, (cache_k, cache_v, page_table), W)
    return (y_p, y_s, hgrn_p, hgrn_s, conv_p, conv_s, gdn_p, gdn_s, k_p, k_s, v_p, v_s)
```

```python
import functools
import math

import jax
import jax.numpy as jnp
from jax import lax
from jax.experimental import pallas as pl
from jax.experimental.pallas import tpu as pltpu

F32 = jnp.float32
BF16 = jnp.bfloat16
HI = lax.Precision.HIGHEST

DEPTH = 4
N_MIXERS = 3
HEADS = 8
HEAD_DIM = 128
CONV_W = 4
CHUNK = 64
MOBA_BLOCK = 256
MOBA_TOPK = 3
REL_BUCKETS = 32
REL_MAX_DIST = 128
PAGE_SIZE = 128
DN_ALPHA = (2 * DEPTH) ** 0.25
LN_EPS = 1e-5
NEG = -1e30

LANES = 128
SUBLANES = 8
MXU_WIDTH = 256
VMEM_LIMIT_BYTES = 56 * 1024 * 1024


def _cparams(*sem):
    return pltpu.CompilerParams(dimension_semantics=sem, vmem_limit_bytes=VMEM_LIMIT_BYTES)


def _round_up(n, m):
    return (n + m - 1) // m * m


def _bdot(a, b):
    return jnp.dot(a.astype(BF16), b.astype(BF16), preferred_element_type=F32)


def _bdot_nt(a, b):
    return lax.dot_general(a.astype(BF16), b.astype(BF16), (((1,), (1,)), ((), ())),
                           preferred_element_type=F32)


def _bdot_tn(a, b):
    return lax.dot_general(a.astype(BF16), b.astype(BF16), (((0,), (0,)), ((), ())),
                           preferred_element_type=F32)


def _fdot(a, b):
    return jnp.dot(a, b, preferred_element_type=F32, precision=HI)


def _block_tril3(r, c):
    row = lax.broadcasted_iota(jnp.int32, (r, 3 * r), 0)
    col = lax.broadcasted_iota(jnp.int32, (r, 3 * r), 1) % r
    return jnp.where(((row // c) == (col // c)) & (row >= col), 1.0, 0.0).astype(BF16)


def _prefix_dot(sel3, x):
    hi = x.astype(BF16)
    rest = x - hi.astype(F32)
    mid = rest.astype(BF16)
    lo = (rest - mid.astype(F32)).astype(BF16)
    return jnp.dot(sel3, jnp.concatenate([hi, mid, lo], axis=0), preferred_element_type=F32)


def _sigmoid(x):
    return 1.0 / (1.0 + jnp.exp(-x))


def _silu(x):
    return x * _sigmoid(x)


def _layer_norm_rows(z, g, b):
    mu = jnp.mean(z, axis=-1, keepdims=True)
    zc = z - mu
    var = jnp.mean(zc * zc, axis=-1, keepdims=True)
    return zc * lax.rsqrt(var + LN_EPS) * g + b


def _ada_kernel(c_ref, w_ref, b_ref, o_ref):
    o_ref[...] = _bdot(_silu(c_ref[...]), w_ref[...]) + b_ref[...]


def _ada_mods(c_all, w_ada, b_ada):
    depth, d, n = w_ada.shape
    nb = c_all.shape[0]
    tn = n // 8
    assert n % 8 == 0 and tn % LANES == 0
    return pl.pallas_call(
        _ada_kernel,
        grid=(depth, n // tn),
        in_specs=[pl.BlockSpec((nb, d), lambda l, j: (0, 0)),
                  pl.BlockSpec((None, d, tn), lambda l, j: (l, 0, j)),
                  pl.BlockSpec((None, 1, tn), lambda l, j: (l, 0, j))],
        out_specs=pl.BlockSpec((None, nb, tn), lambda l, j: (l, 0, j)),
        out_shape=jax.ShapeDtypeStruct((depth, nb, n), F32),
        name="ada_mods",
        compiler_params=_cparams("parallel", "parallel"),
    )(c_all, w_ada, b_ada.reshape(depth, 1, n))


MAX_ROW_TILE = 1024


def _row_tile(batch, length):
    if length >= MXU_WIDTH:
        tm = MAX_ROW_TILE
        while length % tm:
            tm //= 2
        return tm, length // tm, True
    rows = batch * length
    assert rows % SUBLANES == 0 and rows <= MAX_ROW_TILE
    return rows, 1, False


def _mod_layout(m, batch, length):
    tm, _, per_batch = _row_tile(batch, length)
    mt = jnp.transpose(m, (1, 0, 2))
    if per_batch:
        return mt[:, :, None, :]
    return jnp.repeat(mt, length, axis=1)[:, None, :, :]


def _mod_spec(mod, tiles_per_block):
    _, _, rm, d = mod.shape
    return pl.BlockSpec((9, None, rm, d), lambda i, *_: (0, i // tiles_per_block, 0, 0))


FF_CHUNK = MXU_WIDTH


def _ffn_kernel(x_ref, mod_ref, wa_ref, wu_ref, wd_ref, g_ref, b_ref, o_ref, h_ref, *, j, n_chunks):
    x = x_ref[...]
    h_ref[...] = (x * (1.0 + mod_ref[3 * j + 1]) + mod_ref[3 * j]).astype(BF16)

    for c in range(n_chunks):
        cols = slice(c * FF_CHUNK, (c + 1) * FF_CHUNK)
        h = h_ref[...]
        a = jnp.dot(h, wa_ref[:, cols], preferred_element_type=F32)
        u = jnp.dot(h, wu_ref[:, cols], preferred_element_type=F32)
        act = (_silu(a) * u).astype(BF16)
        part = jnp.dot(act, wd_ref[cols, :], preferred_element_type=F32)
        if c == 0:
            o_ref[...] = part
        else:
            o_ref[...] += part
    z = DN_ALPHA * x_ref[...] + (1.0 + mod_ref[3 * j + 2]) * (0.5 * o_ref[...])
    o_ref[...] = _layer_norm_rows(z, g_ref[...], b_ref[...])


def _ffn_block(x2d, mod, tiling, j, weights, layer, ln_g, ln_b):
    tm, tpb, _ = tiling
    rows, d = x2d.shape
    w_a, w_u, w_down = weights
    fp = w_a.shape[-1]
    kern = functools.partial(_ffn_kernel, j=j, n_chunks=fp // FF_CHUNK)
    return pl.pallas_call(
        kern,
        grid=(rows // tm,),
        in_specs=[pl.BlockSpec((tm, d), lambda i: (i, 0)),
                  _mod_spec(mod, tpb),
                  pl.BlockSpec((None, d, fp), lambda i: (layer, 0, 0), pipeline_mode=pl.Buffered(1)),
                  pl.BlockSpec((None, d, fp), lambda i: (layer, 0, 0), pipeline_mode=pl.Buffered(1)),
                  pl.BlockSpec((None, fp, d), lambda i: (layer, 0, 0), pipeline_mode=pl.Buffered(1)),
                  pl.BlockSpec((1, d), lambda i: (0, 0)),
                  pl.BlockSpec((1, d), lambda i: (0, 0))],
        out_specs=pl.BlockSpec((tm, d), lambda i: (i, 0)),
        out_shape=jax.ShapeDtypeStruct((rows, d), F32),
        scratch_shapes=[pltpu.VMEM((tm, d), BF16)],
        name="ffn_block",
        compiler_params=_cparams("parallel"),
    )(x2d, mod, w_a, w_u, w_down, ln_g.reshape(1, d), ln_b.reshape(1, d))


def _prep_ffn_weights(w_up, w_down):
    d_ff = w_up.shape[2] // 2
    fp = _round_up(d_ff, FF_CHUNK)
    pad_cols = lambda w: jnp.pad(w.astype(BF16), ((0, 0), (0, 0), (0, fp - d_ff)))
    down = jnp.pad(w_down.astype(BF16), ((0, 0), (0, fp - d_ff), (0, 0)))
    return pad_cols(w_up[:, :, :d_ff]), pad_cols(w_up[:, :, d_ff:]), down


PROJ_ROW_TILE = 512


def _proj_in_kernel(x_ref, mod_ref, w_ref, *o_refs, j, tn):
    h = (x_ref[...] * (1.0 + mod_ref[3 * j + 1]) + mod_ref[3 * j]).astype(BF16)
    per_out = o_refs[0].shape[1] // tn
    for k in range(w_ref.shape[1] // tn):
        cols = slice((k % per_out) * tn, (k % per_out + 1) * tn)
        o_refs[k // per_out][:, cols] = jnp.dot(h, w_ref[:, k * tn:(k + 1) * tn], preferred_element_type=F32)


def _proj_in(x2d, mod, tiling, j, w, tn, n_out=1):
    tm, tpb, per_batch = tiling
    if per_batch and tm % PROJ_ROW_TILE == 0:
        tpb = tpb * (tm // PROJ_ROW_TILE)
        tm = PROJ_ROW_TILE
    rows, d = x2d.shape
    n = w.shape[1]
    assert n % (n_out * tn) == 0
    outs = pl.pallas_call(
        functools.partial(_proj_in_kernel, j=j, tn=tn),
        grid=(rows // tm,),
        in_specs=[pl.BlockSpec((tm, d), lambda i: (i, 0)),
                  _mod_spec(mod, tpb),
                  pl.BlockSpec((d, n), lambda i: (0, 0), pipeline_mode=pl.Buffered(1))],
        out_specs=[pl.BlockSpec((tm, n // n_out), lambda i: (i, 0))] * n_out,
        out_shape=[jax.ShapeDtypeStruct((rows, n // n_out), F32)] * n_out,
        name="proj_in",
        compiler_params=_cparams("parallel"),
    )(x2d, mod, w)
    return outs[0] if n_out == 1 else outs


def _proj_out_kernel(o_ref, x_ref, mod_ref, w_ref, g_ref, b_ref, y_ref, *, j):
    y = jnp.dot(o_ref[...].astype(BF16), w_ref[...], preferred_element_type=F32)
    z = DN_ALPHA * x_ref[...] + (1.0 + mod_ref[3 * j + 2]) * y
    y_ref[...] = _layer_norm_rows(z, g_ref[...], b_ref[...])


def _proj_out(o2d, x2d, mod, tiling, j, w, ln_g, ln_b):
    tm, tpb, _ = tiling
    rows, d = x2d.shape
    k = o2d.shape[1]
    return pl.pallas_call(
        functools.partial(_proj_out_kernel, j=j),
        grid=(rows // tm,),
        in_specs=[pl.BlockSpec((tm, k), lambda i: (i, 0)),
                  pl.BlockSpec((tm, d), lambda i: (i, 0)),
                  _mod_spec(mod, tpb),
                  pl.BlockSpec((k, d), lambda i: (0, 0)),
                  pl.BlockSpec((1, d), lambda i: (0, 0)),
                  pl.BlockSpec((1, d), lambda i: (0, 0))],
        out_specs=pl.BlockSpec((tm, d), lambda i: (i, 0)),
        out_shape=jax.ShapeDtypeStruct((rows, d), F32),
        name="proj_out",
        compiler_params=_cparams("parallel"),
    )(o2d, x2d, mod, w, ln_g.reshape(1, d), ln_b.reshape(1, d))


def _tri(c, strict=False):
    r = lax.broadcasted_iota(jnp.int32, (c, c), 0)
    s = lax.broadcasted_iota(jnp.int32, (c, c), 1)
    return (r > s) if strict else (r >= s)


def _hgrn_diag_masks(c):
    row = lax.broadcasted_iota(jnp.int32, (c, c), 0)
    col = lax.broadcasted_iota(jnp.int32, (c, c), 1)
    return [(col == row - d) & ((row % SUBLANES) >= d) for d in range(SUBLANES)]


def _hgrn_chunk(q, k, cum, v, st_prev, c, diag_masks):
    hs = range(len(q))
    nb = c // SUBLANES
    last_row = [cum[h][c - 1:c, :] for h in hs]
    o_state = [_bdot_nt(q[h] * jnp.exp(cum[h]), st_prev[h]) for h in hs]
    q3 = [q[h].reshape(nb, SUBLANES, LANES) for h in hs]
    k3 = [k[h].reshape(nb, SUBLANES, LANES) for h in hs]
    cum3 = [cum[h].reshape(nb, SUBLANES, LANES) for h in hs]

    att_off = None
    if nb > 1:
        blk = lax.broadcasted_iota(jnp.int32, (nb, SUBLANES, LANES), 0)
        q_cat, k_cat = [], []
        for h in hs:
            e = cum3[h][:, SUBLANES - 1:SUBLANES, :]
            k_t = k3[h] * jnp.exp(e - cum3[h])
            q_parts, k_parts = [], []
            for jb in range(nb - 1):
                qj = q3[h][jb + 1:] * jnp.exp(cum3[h][jb + 1:] - e[jb:jb + 1])
                qj = jnp.concatenate([jnp.zeros((jb + 1, SUBLANES, LANES), F32), qj], axis=0)
                q_parts.append(qj.reshape(c, LANES).astype(BF16))
                k_parts.append(jnp.where(blk == jb, k_t, 0.0).reshape(c, LANES).astype(BF16))
            q_cat.append(jnp.concatenate(q_parts, axis=1))
            k_cat.append(jnp.concatenate(k_parts, axis=1))
        att_off = [_bdot_nt(q_cat[h], k_cat[h]) for h in hs]

    att = [jnp.zeros((c, c), F32) for _ in hs]
    for d in reversed(range(SUBLANES)):
        for h in hs:
            if d == 0:
                p = q3[h] * k3[h]
            else:
                p = q3[h] * pltpu.roll(k3[h], d, 1) * jnp.exp(cum3[h] - pltpu.roll(cum3[h], d, 1))
            a_d = jnp.sum(p.reshape(c, LANES), axis=-1, keepdims=True)
            att[h] = jnp.where(diag_masks[d], a_d, att[h])
    if att_off is not None:
        att = [att[h] + att_off[h] for h in hs]

    o = [o_state[h] + _bdot(att[h], v[h]) for h in hs]
    st_new = [st_prev[h] * jnp.exp(last_row[h]) + _bdot_tn(v[h], k[h] * jnp.exp(last_row[h] - cum[h])) for h in hs]
    return o, st_new


REC_L_TILE = 512
HGRN_HEADS_PER_STEP = 8
GDN_HEADS_PER_STEP = 8
GDN_STACK = 4


def _head_lanes(hh):
    return slice(hh * HEAD_DIM, (hh + 1) * HEAD_DIM)


def _rec_tiling(l):
    c = min(CHUNK, l)
    tl = min(REC_L_TILE, l)
    assert l % tl == 0 and tl % c == 0 and c % SUBLANES == 0
    return c, tl


def _hgrn_kernel(q_ref, f_ref, v_ref, g_ref, lb_ref, ng_ref, s0_ref, o_ref, s_ref, *, c, n_chunks, hb):
    @pl.when(pl.program_id(2) == 0)
    def _():
        for hh in range(hb):
            s_ref[hh] = s0_ref[hh].T

    ng = ng_ref[...]
    lb = lb_ref[...]
    tril3 = _block_tril3(c, c)
    diag_masks = _hgrn_diag_masks(c)

    def body(i, carry):
        rows = pl.ds(pl.multiple_of(i * c, c), c)
        fr = f_ref[rows, :]
        q_all = _silu(q_ref[rows, :]) * (HEAD_DIM ** -0.5)
        sg = _sigmoid(fr)
        k_all = (1.0 - lb) * (1.0 - sg)
        cum_all = _prefix_dot(tril3, jnp.log(lb + (1.0 - lb) * sg))
        lanes = [_head_lanes(hh) for hh in range(hb)]
        o, st_new = _hgrn_chunk([q_all[:, ln] for ln in lanes], [k_all[:, ln] for ln in lanes],
                                [cum_all[:, ln] for ln in lanes], [v_ref[rows, ln] for ln in lanes],
                                [s_ref[hh] for hh in range(hb)], c, diag_masks)
        for hh, ln in enumerate(lanes):
            s_ref[hh] = st_new[hh]
            ms = jnp.mean(o[hh] * o[hh], axis=-1, keepdims=True)
            o_ref[rows, ln] = o[hh] * lax.rsqrt(ms + 1e-6) * ng * _sigmoid(g_ref[rows, ln])
        return carry

    lax.fori_loop(0, n_chunks, body, 0)

    @pl.when(pl.program_id(2) == pl.num_programs(2) - 1)
    def _():
        for hh in range(hb):
            s_ref[hh] = s_ref[hh].T


def _hgrn_recurrence(proj, s0, layer, lb, norm_g):
    b, l, _ = proj.shape
    c, tl = _rec_tiling(l)
    hb = HGRN_HEADS_PER_STEP
    ng_ = HEADS // hb
    col = lambda off: pl.BlockSpec((None, tl, hb * HEAD_DIM), lambda bi, hg, t: (bi, t, off * ng_ + hg))
    state = pl.BlockSpec((None, hb, HEAD_DIM, HEAD_DIM), lambda bi, hg, t: (bi, hg, 0, 0))
    return pl.pallas_call(
        functools.partial(_hgrn_kernel, c=c, n_chunks=tl // c, hb=hb),
        grid=(b, ng_, l // tl),
        in_specs=[col(0), col(1), col(2), col(3),
                  pl.BlockSpec((1, hb * HEAD_DIM), lambda bi, hg, t: (0, hg)),
                  pl.BlockSpec((1, HEAD_DIM), lambda bi, hg, t: (0, 0)),
                  pl.BlockSpec((None, None, hb, HEAD_DIM, HEAD_DIM), lambda bi, hg, t: (layer, bi, hg, 0, 0))],
        out_specs=[col(0), state],
        out_shape=[jax.ShapeDtypeStruct((b, l, HEADS * HEAD_DIM), F32),
                   jax.ShapeDtypeStruct((b, HEADS, HEAD_DIM, HEAD_DIM), F32)],
        name="hgrn_recurrence",
        compiler_params=_cparams("parallel", "parallel", "arbitrary"),
    )(proj, proj, proj, proj, lb.reshape(1, -1), norm_g.reshape(1, -1), s0)


HIST = SUBLANES


def _gdn_conv(x_ext, cw, c):
    acc = x_ext * cw[CONV_W - 1:CONV_W, :]
    for w in range(CONV_W - 1):
        acc = acc + pltpu.roll(x_ext, CONV_W - 1 - w, 0) * cw[w:w + 1, :]
    return _silu(acc[HIST:, :])


def _l2norm_rows(x):
    return x * lax.rsqrt(jnp.sum(x * x, axis=-1, keepdims=True) + 1e-6)


def _gdn_masks(r, c):
    row = lax.broadcasted_iota(jnp.int32, (r, r), 0)
    col = lax.broadcasted_iota(jnp.int32, (r, r), 1)
    same = (row // c) == (col // c)
    low = same & (row >= col)
    return low, same & (row > col), jnp.where(row == col, 1.0, 0.0), _block_tril3(r, c)


def _gdn_chunk(groups, c, masks):
    low, strict, eye, tril3 = masks
    gr = range(len(groups))
    qs, ks, vs, betas, gs, s_prev = zip(*groups)
    n = len(s_prev[0])
    r = n * c
    pre = [_prefix_dot(tril3, jnp.concatenate(
        [jnp.where(strict, jnp.broadcast_to(gs[g], (r, r)), 0.0), jnp.broadcast_to(gs[g], (r, LANES))], axis=1))
        for g in gr]
    cum = [pre[g][:, r:] for g in gr]
    decay = [jnp.where(low, jnp.exp(jnp.where(low, pre[g][:, :r], 0.0)), 0.0) for g in gr]
    ecum = [jnp.exp(cum[g]) for g in gr]
    kb = [ks[g] * betas[g] for g in gr]
    kk = [_bdot_nt(jnp.concatenate([kb[g], qs[g]], axis=0), ks[g]) for g in gr]
    p = [-jnp.where(strict, kk[g][:r] * decay[g], 0.0) for g in gr]
    att = [kk[g][r:] * decay[g] for g in gr]
    t_inv = [eye + p[g] for g in gr]
    x = p
    for _ in range(int(math.log2(c)) - 1):
        x = [_bdot(x[g], x[g]) for g in gr]
        t_inv = [t_inv[g] + _bdot(x[g], t_inv[g]) for g in gr]
    uw = [_bdot(t_inv[g], jnp.concatenate([vs[g] * betas[g], kb[g] * ecum[g]], axis=1)) for g in gr]
    qe = [qs[g] * ecum[g] for g in gr]
    head_rows = [slice(h * c, (h + 1) * c) for h in range(n)]
    ws = [[_bdot(jnp.concatenate([uw[g][rows, HEAD_DIM:], qe[g][rows]], axis=0), s_prev[g][h])
           for h, rows in enumerate(head_rows)] for g in gr]
    v_new = [jnp.concatenate([uw[g][rows, :HEAD_DIM] - ws[g][h][:c] for h, rows in enumerate(head_rows)], axis=0)
             for g in gr]
    o = [jnp.concatenate([ws[g][h][c:] for h in range(n)], axis=0) + _bdot(att[g], v_new[g]) for g in gr]
    lasts = [[cum[g][(h + 1) * c - 1:(h + 1) * c, :] for h in range(n)] for g in gr]
    kt = [ks[g] * jnp.exp(jnp.concatenate([jnp.broadcast_to(l, (c, LANES)) for l in lasts[g]], axis=0) - cum[g])
          for g in gr]
    s_new = [[s_prev[g][h] * jnp.exp(lasts[g][h]) + _bdot_tn(kt[g][rows], v_new[g][rows])
              for h, rows in enumerate(head_rows)] for g in gr]
    return [(o[g], s_new[g]) for g in gr]


def _gdn_kernel(q_ref, k_ref, v_ref, z_ref, ba_ref, c0q_ref, c0k_ref, c0v_ref, cwq_ref, cwk_ref, cwv_ref,
                an_ref, dt_ref, ng_ref, s0_ref, o_ref, s_ref, qx_ref, kx_ref, vx_ref, *, c, n_chunks, tl, hb):
    hg = pl.program_id(1)
    t = pl.program_id(2)
    staged = ((q_ref, c0q_ref, qx_ref), (k_ref, c0k_ref, kx_ref), (v_ref, c0v_ref, vx_ref))

    @pl.when(t == 0)
    def _():
        s_ref[...] = s0_ref[...]
        for _, c0, dst in staged:
            dst[0:HIST, :] = jnp.zeros((HIST, hb * HEAD_DIM), F32)
            dst[HIST - (CONV_W - 1):HIST, :] = c0[...]

    @pl.when(t > 0)
    def _():
        for _, _, dst in staged:
            dst[0:HIST, :] = dst[tl:tl + HIST, :]

    for src, _, dst in staged:
        dst[HIST:HIST + tl, :] = src[...]
    ng = ng_ref[...]
    lane = lax.broadcasted_iota(jnp.int32, (c, LANES), 1)
    masks = _gdn_masks(GDN_STACK * c, c)

    def body(i, carry):
        r0 = pl.multiple_of(i * c, c)
        ext = pl.ds(r0, c + HIST)
        rows = pl.ds(r0, c)
        ba = ba_ref[rows, :]
        stack = lambda parts: jnp.concatenate(parts, axis=0)
        groups = []
        for h0 in range(0, hb, GDN_STACK):
            qs, ks, vs, betas, gs = [], [], [], [], []
            for hh in range(h0, h0 + GDN_STACK):
                ln = _head_lanes(hh)
                head = hg * hb + hh
                qs.append(_l2norm_rows(_gdn_conv(qx_ref[ext, ln], cwq_ref[:, ln], c)) * (HEAD_DIM ** -0.5))
                ks.append(_l2norm_rows(_gdn_conv(kx_ref[ext, ln], cwk_ref[:, ln], c)))
                vs.append(_gdn_conv(vx_ref[ext, ln], cwv_ref[:, ln], c))
                betas.append(_sigmoid(jnp.sum(jnp.where(lane == head, ba, 0.0), axis=-1, keepdims=True)))
                a_raw = jnp.sum(jnp.where(lane == HEADS + head, ba, 0.0), axis=-1, keepdims=True)
                xa = a_raw + dt_ref[:, ln][:, 0:1]
                softplus = jnp.maximum(xa, 0.0) + jnp.log(1.0 + jnp.exp(-jnp.abs(xa)))
                gs.append(an_ref[:, ln][:, 0:1] * softplus)
            groups.append((stack(qs), stack(ks), stack(vs), stack(betas), stack(gs),
                           [s_ref[hh] for hh in range(h0, h0 + GDN_STACK)]))
        for gi, (o, s_new) in enumerate(_gdn_chunk(groups, c, masks)):
            for j in range(GDN_STACK):
                hh = gi * GDN_STACK + j
                ln = _head_lanes(hh)
                s_ref[hh] = s_new[j]
                oh = o[j * c:(j + 1) * c]
                ms = jnp.mean(oh * oh, axis=-1, keepdims=True)
                o_ref[rows, ln] = oh * lax.rsqrt(ms + 1e-6) * ng * _silu(z_ref[rows, ln])
        return carry

    lax.fori_loop(0, n_chunks, body, 0)


def _gdn_recurrence(proj, conv0, s0, layer, conv_w, a_log, dt_bias, norm_g):
    b, l, _ = proj.shape
    c, tl = _rec_tiling(l)
    hb = GDN_HEADS_PER_STEP
    ng_ = HEADS // hb
    w = hb * HEAD_DIM
    col = lambda off: pl.BlockSpec((None, tl, w), lambda bi, hg, t: (bi, t, off * ng_ + hg))
    c0 = lambda off: pl.BlockSpec((None, None, CONV_W - 1, w), lambda bi, hg, t: (layer, bi, 0, off * ng_ + hg))
    cw = lambda off: pl.BlockSpec((CONV_W, w), lambda bi, hg, t: (0, off * ng_ + hg))
    per_head = pl.BlockSpec((1, w), lambda bi, hg, t: (0, hg))
    state = pl.BlockSpec((None, hb, HEAD_DIM, HEAD_DIM), lambda bi, hg, t: (bi, hg, 0, 0))
    a_neg = jnp.repeat(-jnp.exp(a_log.astype(F32)), HEAD_DIM)[None, :]
    dt_b = jnp.repeat(dt_bias.astype(F32), HEAD_DIM)[None, :]
    xbuf = pltpu.VMEM((HIST + tl, w), F32)
    return pl.pallas_call(
        functools.partial(_gdn_kernel, c=c, n_chunks=tl // c, tl=tl, hb=hb),
        grid=(b, ng_, l // tl),
        in_specs=[col(0), col(1), col(2), col(3),
                  pl.BlockSpec((None, tl, LANES), lambda bi, hg, t: (bi, t, 4 * HEADS)),
                  c0(0), c0(1), c0(2), cw(0), cw(1), cw(2),
                  per_head, per_head,
                  pl.BlockSpec((1, HEAD_DIM), lambda bi, hg, t: (0, 0)),
                  pl.BlockSpec((None, None, hb, HEAD_DIM, HEAD_DIM), lambda bi, hg, t: (layer, bi, hg, 0, 0))],
        out_specs=[col(0), state],
        out_shape=[jax.ShapeDtypeStruct((b, l, HEADS * HEAD_DIM), F32),
                   jax.ShapeDtypeStruct((b, HEADS, HEAD_DIM, HEAD_DIM), F32)],
        scratch_shapes=[xbuf, xbuf, xbuf],
        name="gdn_recurrence",
        compiler_params=_cparams("parallel", "parallel", "arbitrary"),
    )(proj, proj, proj, proj, proj, conv0, conv0, conv0, conv_w, conv_w, conv_w,
      a_neg, dt_b, norm_g.reshape(1, -1), s0)


def _rel_bucket(dist):
    exact = REL_BUCKETS // 2
    d = jnp.maximum(dist, exact).astype(F32)
    large = exact + (jnp.log(d / exact) / math.log(REL_MAX_DIST / exact) * (REL_BUCKETS - exact)).astype(jnp.int32)
    return jnp.where(dist < exact, dist, jnp.minimum(large, REL_BUCKETS - 1))


def _top_mask(gate, n_valid, k, axis):
    idx = lax.broadcasted_iota(jnp.int32, gate.shape, axis)
    big = jnp.int32(2 ** 30)
    work = jnp.where(idx < n_valid, gate, NEG)
    sel = jnp.zeros(gate.shape, F32)
    for r in range(k):
        mx = jnp.max(work, axis=axis, keepdims=True)
        first = jnp.min(jnp.where(work == mx, idx, big), axis=axis, keepdims=True)
        pick = (idx == first) & (idx < n_valid)
        sel = jnp.where(pick, 1.0, sel)
        work = jnp.where(idx == first, -jnp.inf, work)
    return sel


MOBA_HEADS_PER_STEP = 4


def _moba_prompt_kernel(q_ref, k_ref, v_ref, bd_ref, bp_ref, bf_ref, o_ref, means_ref, sel_ref, *, nb, hb):
    i = pl.program_id(2)
    blk = MOBA_BLOCK

    @pl.when(i == 0)
    def _():
        means_ref[...] = jnp.zeros_like(means_ref)
        for hh in range(hb):
            for n in range(nb):
                means_ref[hh, n:n + 1, :] = jnp.mean(k_ref[n * blk:(n + 1) * blk, _head_lanes(hh)],
                                                     axis=0, keepdims=True)

    causal = ~_tri(blk, strict=True)
    r0 = pl.multiple_of(i * blk, blk)
    heads = range(hb)
    qs = [q_ref[:, _head_lanes(hh)] * (HEAD_DIM ** -0.5) for hh in heads]
    qbs = [q.astype(BF16) for q in qs]
    gates = [lax.dot_general(means_ref[hh], qs[hh], (((1,), (1,)), ((), ())),
                             preferred_element_type=F32, precision=HI) for hh in heads]
    qk = [_bdot_nt(k_ref[pl.ds(r0, blk), _head_lanes(hh)], qbs[hh]) for hh in heads]
    for hh in heads:
        sel_ref[hh] = _top_mask(gates[hh], i, MOBA_TOPK, 0)
    s0 = [jnp.where(causal, qk[hh] + bd_ref[hh], NEG) for hh in heads]
    m0 = [jnp.max(s, axis=0, keepdims=True) for s in s0]
    p0 = [jnp.exp(s0[hh] - m0[hh]) for hh in heads]
    init = [(m0[hh], jnp.sum(p0[hh], axis=0, keepdims=True),
             _bdot_tn(v_ref[pl.ds(r0, blk), _head_lanes(hh)], p0[hh])) for hh in heads]

    def past_block(n, carry, bias_of_head):
        rn = pl.multiple_of(n * blk, blk)
        heads = range(hb)
        qk = [_bdot_nt(k_ref[pl.ds(rn, blk), _head_lanes(hh)], qbs[hh]) for hh in heads]
        sn = [jnp.where(sel_ref[hh, pl.ds(n, 1), :] > 0.0, qk[hh] + bias_of_head(hh), NEG) for hh in heads]
        m_new = [jnp.maximum(carry[hh][0], jnp.max(sn[hh], axis=0, keepdims=True)) for hh in heads]
        pn = [jnp.exp(sn[hh] - m_new[hh]) for hh in heads]
        pv = [_bdot_tn(v_ref[pl.ds(rn, blk), _head_lanes(hh)], pn[hh]) for hh in heads]
        out = []
        for hh in heads:
            m, l, acc = carry[hh]
            alpha = jnp.exp(m - m_new[hh])
            out.append((m_new[hh], l * alpha + jnp.sum(pn[hh], axis=0, keepdims=True), acc * alpha + pv[hh]))
        return tuple(out)

    carry = lax.fori_loop(0, jnp.maximum(i - 1, 0),
                          lambda n, cr: past_block(n, cr, lambda hh: bf_ref[:, _head_lanes(hh)][:, 0:1]),
                          tuple(init))
    final = past_block(jnp.maximum(i - 1, 0), carry, lambda hh: bp_ref[hh])
    for hh in range(hb):
        _, l, acc = final[hh]
        o_ref[:, _head_lanes(hh)] = (acc / l).T


def _bias_of_distance(rel_bias, dist):
    bucket = _rel_bucket(jnp.maximum(dist, 0))[None]
    out = jnp.zeros((HEADS,) + dist.shape, F32)
    for b in range(REL_BUCKETS):
        out = jnp.where(bucket == b, rel_bias[b].astype(F32).reshape((HEADS,) + (1,) * dist.ndim), out)
    return out


def _moba_bias_tables(rel_bias, blk):
    t = jnp.arange(blk, dtype=jnp.int32)
    d0 = t[None, :] - t[:, None]
    bias_diag = _bias_of_distance(rel_bias, d0)
    bias_prev = _bias_of_distance(rel_bias, d0 + blk)
    far = jnp.repeat(rel_bias[REL_BUCKETS - 1].astype(F32), HEAD_DIM)[None, :]
    return bias_diag, bias_prev, far


def _moba_prompt(q, k, v, rel_bias):
    b, l, _ = q.shape
    blk = MOBA_BLOCK
    assert l % blk == 0 and l // blk <= LANES
    assert 2 * blk > REL_MAX_DIST
    nb = l // blk
    bias_diag, bias_prev, far = _moba_bias_tables(rel_bias, blk)
    hb = MOBA_HEADS_PER_STEP
    ng_ = HEADS // hb
    w = hb * HEAD_DIM
    tile = pl.BlockSpec((hb, blk, blk), lambda bi, hg, i: (hg, 0, 0))
    return pl.pallas_call(
        functools.partial(_moba_prompt_kernel, nb=nb, hb=hb),
        grid=(b, ng_, nb),
        in_specs=[pl.BlockSpec((None, blk, w), lambda bi, hg, i: (bi, i, hg)),
                  pl.BlockSpec((None, l, w), lambda bi, hg, i: (bi, 0, hg)),
                  pl.BlockSpec((None, l, w), lambda bi, hg, i: (bi, 0, hg)),
                  tile, tile,
                  pl.BlockSpec((1, w), lambda bi, hg, i: (0, hg))],
        out_specs=pl.BlockSpec((None, blk, w), lambda bi, hg, i: (bi, i, hg)),
        out_shape=jax.ShapeDtypeStruct((b, l, HEADS * HEAD_DIM), F32),
        scratch_shapes=[pltpu.VMEM((hb, _round_up(nb, SUBLANES), HEAD_DIM), F32),
                        pltpu.VMEM((hb, _round_up(nb, SUBLANES), blk), F32)],
        name="moba_prompt",
        compiler_params=_cparams("parallel", "parallel", "arbitrary"),
    )(q, k, v, bias_diag, bias_prev, far)


def _block_diag_rows(x, lq):
    return jnp.concatenate([x[h * lq:(h + 1) * lq, h * HEAD_DIM:(h + 1) * HEAD_DIM] for h in range(HEADS)], axis=0)


MOBA_SAMPLE_BLOCKS_PER_STEP = 4


def _moba_sample_kernel(pt_ref, *refs, nb, lq, bps):
    n_pages = bps * (MOBA_BLOCK // PAGE_SIZE)
    k_refs, v_refs = refs[:n_pages], refs[n_pages:2 * n_pages]
    (q_ref, kn_ref, vn_ref, bl_ref, bc_ref, bf_ref, o_ref,
     qbd_ref, m_ref, l_ref, acc_ref, means_ref) = refs[2 * n_pages:]
    step = pl.program_id(1)
    d_model = HEADS * HEAD_DIM
    hq = HEADS * lq

    @pl.when(step == 0)
    def _():
        q = q_ref[...] * (HEAD_DIM ** -0.5)
        rows = lax.broadcasted_iota(jnp.int32, (d_model, hq), 0) // HEAD_DIM
        cols = lax.broadcasted_iota(jnp.int32, (d_model, hq), 1)
        spread = (lax.broadcasted_iota(jnp.int32, (lq, hq), 1) % lq
                  == lax.broadcasted_iota(jnp.int32, (lq, hq), 0)).astype(F32)
        qt = lax.dot_general(q, spread, (((0,), (0,)), ((), ())),
                             preferred_element_type=F32, precision=HI)
        qbd_ref[...] = jnp.where(rows == cols // lq, qt, 0.0)
        means_ref[...] = jnp.zeros_like(means_ref)
        m_ref[...] = jnp.zeros_like(m_ref)
        l_ref[...] = jnp.zeros_like(l_ref)

    qbd = qbd_ref[...]

    def rows_by_lanes(*pages):
        return jnp.concatenate(
            [jnp.concatenate([pg[pl.ds(h, PAGE_SIZE, stride=HEADS), :] for h in range(HEADS)], axis=1)
             for pg in pages], axis=0)

    js = range(bps)
    ns = [step * bps + j for j in js]
    kblk = [rows_by_lanes(k_refs[2 * j], k_refs[2 * j + 1]) for j in js]
    vblk = [rows_by_lanes(v_refs[2 * j], v_refs[2 * j + 1]) for j in js]
    s = [_bdot(kblk[j], qbd) + jnp.where(ns[j] == nb - 1, bl_ref[...], bf_ref[...]) for j in js]
    for j in js:
        means_ref[pl.ds(ns[j], 1), :] = jnp.mean(kblk[j], axis=0, keepdims=True)
    m = [jnp.max(s[j], axis=0, keepdims=True) for j in js]
    p = [jnp.exp(s[j] - m[j]) for j in js]
    pv = [_bdot_tn(p[j], vblk[j]) for j in js]
    for j in js:
        m_ref[pl.ds(ns[j], 1), :] = m[j]
        l_ref[pl.ds(ns[j], 1), :] = jnp.sum(p[j], axis=0, keepdims=True)
        acc_ref[ns[j]] = _block_diag_rows(pv[j], lq)

    @pl.when(step == pl.num_programs(1) - 1)
    def _():
        k_new = kn_ref[...]
        v_new = vn_ref[...]
        jq = lax.broadcasted_iota(jnp.int32, (lq, hq), 1) % lq
        jk = lax.broadcasted_iota(jnp.int32, (lq, hq), 0)
        sc = jnp.where(jk <= jq, _bdot(k_new, qbd) + bc_ref[...], NEG)
        mc = jnp.max(sc, axis=0, keepdims=True)
        pc = jnp.exp(sc - mc)
        lc = jnp.sum(pc, axis=0, keepdims=True)
        oc = _block_diag_rows(_bdot_tn(pc, v_new), lq)

        gate = _fdot(means_ref[...], qbd)
        sel = _top_mask(gate, nb, MOBA_TOPK, 0)
        m_all = m_ref[...]
        m_tot = jnp.maximum(jnp.max(jnp.where(sel > 0.0, m_all, NEG), axis=0, keepdims=True), mc)
        w = jnp.where(sel > 0.0, jnp.exp(jnp.where(sel > 0.0, m_all - m_tot, 0.0)), 0.0)
        wc = jnp.exp(mc - m_tot)
        denom = jnp.sum(w * l_ref[...], axis=0, keepdims=True) + wc * lc
        nbp = w.shape[0]
        w_all = jnp.concatenate([w, wc, jnp.zeros((SUBLANES - 1, hq), F32)], axis=0) / denom
        ne = nbp + SUBLANES
        eye = (lax.broadcasted_iota(jnp.int32, (ne, ne), 0)
               == lax.broadcasted_iota(jnp.int32, (ne, ne), 1)).astype(F32)
        wt = lax.dot_general(w_all, eye, (((0,), (0,)), ((), ())),
                             preferred_element_type=F32, precision=HI)
        out = wt[:, nbp:nbp + 1] * oc
        for b_i in range(nb):
            out = out + wt[:, b_i:b_i + 1] * acc_ref[b_i]
        o_ref[...] = out


def _moba_sample(q, k, v, cache_k, cache_v, layer, page_table, rel_bias):
    b, lq, _ = q.shape
    n_pages = page_table.shape[1]
    d_model = HEADS * HEAD_DIM
    pages_per_block = MOBA_BLOCK // PAGE_SIZE
    assert pages_per_block == 2 and n_pages % pages_per_block == 0 and lq <= MOBA_BLOCK
    assert cache_k.shape[2:] == (PAGE_SIZE, HEADS, HEAD_DIM)
    nb = n_pages // pages_per_block
    assert nb >= 1 and MOBA_BLOCK >= REL_MAX_DIST
    nbp = _round_up(nb, SUBLANES)
    hq = HEADS * lq
    jq = jnp.arange(lq, dtype=jnp.int32)
    t = jnp.arange(MOBA_BLOCK, dtype=jnp.int32)
    by_cols = lambda tab: jnp.transpose(tab, (1, 0, 2)).reshape(tab.shape[1], hq)
    bias_last = by_cols(_bias_of_distance(rel_bias, (MOBA_BLOCK + jq)[None, :] - t[:, None]))
    bias_cur = by_cols(_bias_of_distance(rel_bias, jq[None, :] - jq[:, None]))
    bias_far = jnp.repeat(rel_bias[REL_BUCKETS - 1].astype(F32), lq)[None, :]

    n_c, n_pool = cache_k.shape[:2]
    cache_k = cache_k.reshape(n_c, n_pool, PAGE_SIZE * HEADS, HEAD_DIM)
    cache_v = cache_v.reshape(n_c, n_pool, PAGE_SIZE * HEADS, HEAD_DIM)
    bps = math.gcd(MOBA_SAMPLE_BLOCKS_PER_STEP, nb)
    pages_per_step = bps * pages_per_block
    page = lambda off: pl.BlockSpec((None, None, PAGE_SIZE * HEADS, HEAD_DIM),
                                    lambda bi, n, pt: (layer, pt[bi, pages_per_step * n + off], 0, 0))
    pages = [page(off) for off in range(pages_per_step)]
    full = lambda shape: pl.BlockSpec(shape, lambda bi, n, pt: (0,) * len(shape))
    grid_spec = pltpu.PrefetchScalarGridSpec(
        num_scalar_prefetch=1,
        grid=(b, nb // bps),
        in_specs=pages + pages + [
                  *([pl.BlockSpec((None, lq, d_model), lambda bi, n, pt: (bi, 0, 0))] * 3),
                  full((MOBA_BLOCK, hq)), full((lq, hq)), full((1, hq))],
        out_specs=pl.BlockSpec((None, hq, HEAD_DIM), lambda bi, n, pt: (bi, 0, 0)),
        scratch_shapes=[pltpu.VMEM((d_model, hq), F32),
                        pltpu.VMEM((nbp, hq), F32),
                        pltpu.VMEM((nbp, hq), F32),
                        pltpu.VMEM((nb, hq, HEAD_DIM), F32),
                        pltpu.VMEM((nbp, d_model), F32)])
    return pl.pallas_call(
        functools.partial(_moba_sample_kernel, nb=nb, lq=lq, bps=bps),
        grid_spec=grid_spec,
        out_shape=jax.ShapeDtypeStruct((b, hq, HEAD_DIM), F32),
        name="moba_sample",
        compiler_params=_cparams("parallel", "arbitrary"),
    )(page_table, *([cache_k] * pages_per_step), *([cache_v] * pages_per_step), q, k, v,
      bias_last, bias_cur, bias_far)


def _trunk(x, mods, s_hgrn, s_conv, s_gdn, paged, W):
    b, l, d = x.shape
    tiling = _row_tile(b, l)
    x2 = x.reshape(b * l, d)
    new_h, new_c, new_g, new_k, new_v = [], [], [], [], []
    for i in range(DEPTH):
        mod = _mod_layout(mods[i], b, l)
        x2 = _ffn_block(x2, mod, tiling, 0, W['ffn1'], i, W['ln_g'][i, 0], W['ln_b'][i, 0])
        kind, li = i % N_MIXERS, i // N_MIXERS
        if kind == 0:
            proj = _proj_in(x2, mod, tiling, 1, W['hgrn_w_in'][li], 1024)
            o, s = _hgrn_recurrence(proj.reshape(b, l, -1), s_hgrn, li, W['hgrn_lb'][li], W['hgrn_norm_g'][li])
            new_h.append(s)
            w_o = W['hgrn_w_o'][li]
            o2 = o.reshape(b * l, d)
        elif kind == 1:
            proj = _proj_in(x2, mod, tiling, 1, W['gdn_w_in'][li], W['gdn_w_in'][li].shape[1] // 3)
            proj = proj.reshape(b, l, -1)
            o, s = _gdn_recurrence(proj, s_conv, s_gdn, li, W['gdn_conv_w'][li], W['gdn_a_log'][li],
                                   W['gdn_dt_bias'][li], W['gdn_norm_g'][li])
            n_qkv = 3 * HEADS * HEAD_DIM
            assert l >= CONV_W - 1
            new_c.append(proj[:, l - (CONV_W - 1):, :n_qkv])
            new_g.append(s)
            w_o = W['gdn_w_o'][li]
            o2 = o.reshape(b * l, d)
        else:
            q, k, v = (t.reshape(b, l, d) for t in _proj_in(x2, mod, tiling, 1, W['moba_w_qkv'][li], 1024, n_out=3))
            if paged is None:
                o2 = _moba_prompt(q, k, v, W['rel_bias']).reshape(b * l, d)
            else:
                ck, cv, pt = paged
                o = _moba_sample(q, k, v, ck, cv, li, pt, W['rel_bias'])
                o2 = jnp.transpose(o.reshape(b, HEADS, l, HEAD_DIM), (0, 2, 1, 3)).reshape(b * l, d)
            new_k.append(k.reshape(b, l, HEADS, HEAD_DIM))
            new_v.append(v.reshape(b, l, HEADS, HEAD_DIM))
            w_o = W['moba_w_o'][li]
        x2 = _proj_out(o2, x2, mod, tiling, 1, w_o, W['ln_g'][i, 1], W['ln_b'][i, 1])
        x2 = _ffn_block(x2, mod, tiling, 2, W['ffn2'], i, W['ln_g'][i, 2], W['ln_b'][i, 2])
    return (x2.reshape(b, l, d), jnp.stack(new_h), jnp.stack(new_c), jnp.stack(new_g),
            jnp.stack(new_k), jnp.stack(new_v))


def kernel(x_prompt, x_sample, state_hgrn, state_gdn_conv, state_gdn, cache_k, cache_v, page_table, c_prompt, c_sample, w_ada, b_ada, ln_g, ln_b, w_ffn1_up, w_ffn1_down, w_ffn2_up, w_ffn2_down, hgrn_w_in, hgrn_lb_logits, hgrn_norm_g, hgrn_w_o, gdn_w_in, gdn_conv_w, gdn_a_log, gdn_dt_bias, gdn_norm_g, gdn_w_o, moba_w_qkv, moba_w_o, rel_bias):
    bp, _, d = x_prompt.shape
    bs = x_sample.shape[0]
    assert d == HEADS * HEAD_DIM

    nc = _round_up(bp + bs, SUBLANES)
    c_all = jnp.pad(jnp.concatenate([c_prompt, c_sample], axis=0), ((0, nc - bp - bs), (0, 0)))
    mods = _ada_mods(c_all, w_ada, b_ada).reshape(DEPTH, nc, 9, d)

    gdn_n = _round_up(gdn_w_in.shape[2], 3 * LANES)
    lbp = jax.nn.softmax(hgrn_lb_logits.astype(F32), axis=0)
    W = {
        'ffn1': _prep_ffn_weights(w_ffn1_up, w_ffn1_down), 'ffn2': _prep_ffn_weights(w_ffn2_up, w_ffn2_down),
        'ln_g': ln_g, 'ln_b': ln_b,
        'hgrn_w_in': hgrn_w_in.astype(BF16), 'hgrn_lb': jnp.cumsum(lbp, axis=0) - lbp[0],
        'hgrn_norm_g': hgrn_norm_g, 'hgrn_w_o': hgrn_w_o.astype(BF16),
        'gdn_w_in': jnp.pad(gdn_w_in, ((0, 0), (0, 0), (0, gdn_n - gdn_w_in.shape[2]))).astype(BF16),
        'gdn_conv_w': gdn_conv_w, 'gdn_a_log': gdn_a_log, 'gdn_dt_bias': gdn_dt_bias,
        'gdn_norm_g': gdn_norm_g, 'gdn_w_o': gdn_w_o.astype(BF16),
        'moba_w_qkv': moba_w_qkv.astype(BF16), 'moba_w_o': moba_w_o.astype(BF16), 'rel_bias': rel_bias,
    }

    n_a, n_b = state_hgrn.shape[0], state_gdn.shape[0]
    z_hgrn = jnp.zeros((n_a, bp) + state_hgrn.shape[2:], F32)
    z_conv = jnp.zeros((n_b, bp) + state_gdn_conv.shape[2:], F32)
    z_gdn = jnp.zeros((n_b, bp) + state_gdn.shape[2:], F32)
    y_p, hgrn_p, conv_p, gdn_p, k_p, v_p = _trunk(x_prompt, mods[:, :bp], z_hgrn, z_conv, z_gdn, None, W)

    y_s, hgrn_s, conv_s, gdn_s, k_s, v_s = _trunk(x_sample, mods[:, bp:bp + bs], state_hgrn, state_gdn_conv,
                                                  state_gdn, (cache_k, cache_v, page_table), W)
    return (y_p, y_s, hgrn_p, hgrn_s, conv_p, conv_s, gdn_p, gdn_s, k_p, k_s, v_p, v_s)
```

```python
import functools
import math

import jax
import jax.numpy as jnp
from jax import lax
from jax.experimental import pallas as pl
from jax.experimental.pallas import tpu as pltpu

F32 = jnp.float32
BF16 = jnp.bfloat16
HI = lax.Precision.HIGHEST

DEPTH = 4
N_MIXERS = 3
HEADS = 8
HEAD_DIM = 128
CONV_W = 4
CHUNK = 64
MOBA_BLOCK = 256
MOBA_TOPK = 3
REL_BUCKETS = 32
REL_MAX_DIST = 128
PAGE_SIZE = 128
DN_ALPHA = (2 * DEPTH) ** 0.25
LN_EPS = 1e-5
NEG = -1e30

LANES = 128
SUBLANES = 8
MXU_WIDTH = 256
VMEM_LIMIT_BYTES = 56 * 1024 * 1024


def _cparams(*sem):
    return pltpu.CompilerParams(dimension_semantics=sem, vmem_limit_bytes=VMEM_LIMIT_BYTES)


def _round_up(n, m):
    return (n + m - 1) // m * m


def _bdot(a, b):
    return jnp.dot(a.astype(BF16), b.astype(BF16), preferred_element_type=F32)


def _bdot_nt(a, b):
    return lax.dot_general(a.astype(BF16), b.astype(BF16), (((1,), (1,)), ((), ())),
                           preferred_element_type=F32)


def _bdot_tn(a, b):
    return lax.dot_general(a.astype(BF16), b.astype(BF16), (((0,), (0,)), ((), ())),
                           preferred_element_type=F32)


def _fdot(a, b):
    return jnp.dot(a, b, preferred_element_type=F32, precision=HI)


def _block_tril3(r, c):
    row = lax.broadcasted_iota(jnp.int32, (r, 3 * r), 0)
    col = lax.broadcasted_iota(jnp.int32, (r, 3 * r), 1) % r
    return jnp.where(((row // c) == (col // c)) & (row >= col), 1.0, 0.0).astype(BF16)


def _prefix_dot(sel3, x):
    hi = x.astype(BF16)
    rest = x - hi.astype(F32)
    mid = rest.astype(BF16)
    lo = (rest - mid.astype(F32)).astype(BF16)
    return jnp.dot(sel3, jnp.concatenate([hi, mid, lo], axis=0), preferred_element_type=F32)


def _sigmoid(x):
    return 1.0 / (1.0 + jnp.exp(-x))


def _silu(x):
    return x * _sigmoid(x)


def _layer_norm_rows(z, g, b):
    mu = jnp.mean(z, axis=-1, keepdims=True)
    zc = z - mu
    var = jnp.mean(zc * zc, axis=-1, keepdims=True)
    return zc * lax.rsqrt(var + LN_EPS) * g + b


def _ada_kernel(c_ref, w_ref, b_ref, o_ref):
    o_ref[...] = _bdot(_silu(c_ref[...]), w_ref[...]) + b_ref[...]


def _ada_mods(c_all, w_ada, b_ada):
    depth, d, n = w_ada.shape
    nb = c_all.shape[0]
    tn = n // 8
    assert n % 8 == 0 and tn % LANES == 0
    return pl.pallas_call(
        _ada_kernel,
        grid=(depth, n // tn),
        in_specs=[pl.BlockSpec((nb, d), lambda l, j: (0, 0)),
                  pl.BlockSpec((None, d, tn), lambda l, j: (l, 0, j)),
                  pl.BlockSpec((None, 1, tn), lambda l, j: (l, 0, j))],
        out_specs=pl.BlockSpec((None, nb, tn), lambda l, j: (l, 0, j)),
        out_shape=jax.ShapeDtypeStruct((depth, nb, n), F32),
        name="ada_mods",
        compiler_params=_cparams("parallel", "parallel"),
    )(c_all, w_ada, b_ada.reshape(depth, 1, n))


MAX_ROW_TILE = 1024


def _row_tile(batch, length):
    if length >= MXU_WIDTH:
        tm = MAX_ROW_TILE
        while length % tm:
            tm //= 2
        return tm, length // tm, True
    rows = batch * length
    assert rows % SUBLANES == 0 and rows <= MAX_ROW_TILE
    return rows, 1, False


def _mod_layout(m, batch, length):
    tm, _, per_batch = _row_tile(batch, length)
    mt = jnp.transpose(m, (1, 0, 2))
    if per_batch:
        return mt[:, :, None, :]
    return jnp.repeat(mt, length, axis=1)[:, None, :, :]


def _mod_spec(mod, tiles_per_block):
    _, _, rm, d = mod.shape
    return pl.BlockSpec((9, None, rm, d), lambda i, *_: (0, i // tiles_per_block, 0, 0))


FF_CHUNK = MXU_WIDTH


def _ffn_kernel(x_ref, mod_ref, wa_ref, wu_ref, wd_ref, g_ref, b_ref, o_ref, h_ref, *, j, n_chunks):
    x = x_ref[...]
    h_ref[...] = (x * (1.0 + mod_ref[3 * j + 1]) + mod_ref[3 * j]).astype(BF16)

    for c in range(n_chunks):
        cols = slice(c * FF_CHUNK, (c + 1) * FF_CHUNK)
        h = h_ref[...]
        a = jnp.dot(h, wa_ref[:, cols], preferred_element_type=F32)
        u = jnp.dot(h, wu_ref[:, cols], preferred_element_type=F32)
        act = (_silu(a) * u).astype(BF16)
        part = jnp.dot(act, wd_ref[cols, :], preferred_element_type=F32)
        if c == 0:
            o_ref[...] = part
        else:
            o_ref[...] += part
    z = DN_ALPHA * x_ref[...] + (1.0 + mod_ref[3 * j + 2]) * (0.5 * o_ref[...])
    o_ref[...] = _layer_norm_rows(z, g_ref[...], b_ref[...])


def _ffn_block(x2d, mod, tiling, j, weights, layer, ln_g, ln_b):
    tm, tpb, _ = tiling
    rows, d = x2d.shape
    w_a, w_u, w_down = weights
    fp = w_a.shape[-1]
    kern = functools.partial(_ffn_kernel, j=j, n_chunks=fp // FF_CHUNK)
    return pl.pallas_call(
        kern,
        grid=(rows // tm,),
        in_specs=[pl.BlockSpec((tm, d), lambda i: (i, 0)),
                  _mod_spec(mod, tpb),
                  pl.BlockSpec((None, d, fp), lambda i: (layer, 0, 0), pipeline_mode=pl.Buffered(1)),
                  pl.BlockSpec((None, d, fp), lambda i: (layer, 0, 0), pipeline_mode=pl.Buffered(1)),
                  pl.BlockSpec((None, fp, d), lambda i: (layer, 0, 0), pipeline_mode=pl.Buffered(1)),
                  pl.BlockSpec((1, d), lambda i: (0, 0)),
                  pl.BlockSpec((1, d), lambda i: (0, 0))],
        out_specs=pl.BlockSpec((tm, d), lambda i: (i, 0)),
        out_shape=jax.ShapeDtypeStruct((rows, d), F32),
        scratch_shapes=[pltpu.VMEM((tm, d), BF16)],
        name="ffn_block",
        compiler_params=_cparams("parallel"),
    )(x2d, mod, w_a, w_u, w_down, ln_g.reshape(1, d), ln_b.reshape(1, d))


def _prep_ffn_weights(w_up, w_down):
    d_ff = w_up.shape[2] // 2
    fp = _round_up(d_ff, FF_CHUNK)
    pad_cols = lambda w: jnp.pad(w.astype(BF16), ((0, 0), (0, 0), (0, fp - d_ff)))
    down = jnp.pad(w_down.astype(BF16), ((0, 0), (0, fp - d_ff), (0, 0)))
    return pad_cols(w_up[:, :, :d_ff]), pad_cols(w_up[:, :, d_ff:]), down


PROJ_ROW_TILE = 512


def _proj_in_kernel(x_ref, mod_ref, w_ref, *o_refs, j, tn):
    h = (x_ref[...] * (1.0 + mod_ref[3 * j + 1]) + mod_ref[3 * j]).astype(BF16)
    per_out = o_refs[0].shape[1] // tn
    for k in range(w_ref.shape[1] // tn):
        cols = slice((k % per_out) * tn, (k % per_out + 1) * tn)
        o_refs[k // per_out][:, cols] = jnp.dot(h, w_ref[:, k * tn:(k + 1) * tn], preferred_element_type=F32)


def _proj_in(x2d, mod, tiling, j, w, tn, n_out=1):
    tm, tpb, per_batch = tiling
    if per_batch and tm % PROJ_ROW_TILE == 0:
        tpb = tpb * (tm // PROJ_ROW_TILE)
        tm = PROJ_ROW_TILE
    rows, d = x2d.shape
    n = w.shape[1]
    assert n % (n_out * tn) == 0
    outs = pl.pallas_call(
        functools.partial(_proj_in_kernel, j=j, tn=tn),
        grid=(rows // tm,),
        in_specs=[pl.BlockSpec((tm, d), lambda i: (i, 0)),
                  _mod_spec(mod, tpb),
                  pl.BlockSpec((d, n), lambda i: (0, 0), pipeline_mode=pl.Buffered(1))],
        out_specs=[pl.BlockSpec((tm, n // n_out), lambda i: (i, 0))] * n_out,
        out_shape=[jax.ShapeDtypeStruct((rows, n // n_out), F32)] * n_out,
        name="proj_in",
        compiler_params=_cparams("parallel"),
    )(x2d, mod, w)
    return outs[0] if n_out == 1 else outs


def _proj_out_kernel(o_ref, x_ref, mod_ref, w_ref, g_ref, b_ref, y_ref, *, j):
    y = jnp.dot(o_ref[...].astype(BF16), w_ref[...], preferred_element_type=F32)
    z = DN_ALPHA * x_ref[...] + (1.0 + mod_ref[3 * j + 2]) * y
    y_ref[...] = _layer_norm_rows(z, g_ref[...], b_ref[...])


def _proj_out(o2d, x2d, mod, tiling, j, w, ln_g, ln_b):
    tm, tpb, _ = tiling
    rows, d = x2d.shape
    k = o2d.shape[1]
    return pl.pallas_call(
        functools.partial(_proj_out_kernel, j=j),
        grid=(rows // tm,),
        in_specs=[pl.BlockSpec((tm, k), lambda i: (i, 0)),
                  pl.BlockSpec((tm, d), lambda i: (i, 0)),
                  _mod_spec(mod, tpb),
                  pl.BlockSpec((k, d), lambda i: (0, 0)),
                  pl.BlockSpec((1, d), lambda i: (0, 0)),
                  pl.BlockSpec((1, d), lambda i: (0, 0))],
        out_specs=pl.BlockSpec((tm, d), lambda i: (i, 0)),
        out_shape=jax.ShapeDtypeStruct((rows, d), F32),
        name="proj_out",
        compiler_params=_cparams("parallel"),
    )(o2d, x2d, mod, w, ln_g.reshape(1, d), ln_b.reshape(1, d))


def _tri(c, strict=False):
    r = lax.broadcasted_iota(jnp.int32, (c, c), 0)
    s = lax.broadcasted_iota(jnp.int32, (c, c), 1)
    return (r > s) if strict else (r >= s)


def _hgrn_diag_masks(c):
    row = lax.broadcasted_iota(jnp.int32, (c, c), 0)
    col = lax.broadcasted_iota(jnp.int32, (c, c), 1)
    return [(col == row - d) & ((row % SUBLANES) >= d) for d in range(SUBLANES)]


def _hgrn_chunk(q, k, cum, v, st_prev, c, diag_masks):
    hs = range(len(q))
    nb = c // SUBLANES
    last_row = [cum[h][c - 1:c, :] for h in hs]
    o_state = [_bdot_nt(q[h] * jnp.exp(cum[h]), st_prev[h]) for h in hs]
    q3 = [q[h].reshape(nb, SUBLANES, LANES) for h in hs]
    k3 = [k[h].reshape(nb, SUBLANES, LANES) for h in hs]
    cum3 = [cum[h].reshape(nb, SUBLANES, LANES) for h in hs]

    att_off = None
    if nb > 1:
        blk = lax.broadcasted_iota(jnp.int32, (nb, SUBLANES, LANES), 0)
        q_cat, k_cat = [], []
        for h in hs:
            e = cum3[h][:, SUBLANES - 1:SUBLANES, :]
            k_t = k3[h] * jnp.exp(e - cum3[h])
            q_parts, k_parts = [], []
            for jb in range(nb - 1):
                qj = q3[h][jb + 1:] * jnp.exp(cum3[h][jb + 1:] - e[jb:jb + 1])
                qj = jnp.concatenate([jnp.zeros((jb + 1, SUBLANES, LANES), F32), qj], axis=0)
                q_parts.append(qj.reshape(c, LANES).astype(BF16))
                k_parts.append(jnp.where(blk == jb, k_t, 0.0).reshape(c, LANES).astype(BF16))
            q_cat.append(jnp.concatenate(q_parts, axis=1))
            k_cat.append(jnp.concatenate(k_parts, axis=1))
        att_off = [_bdot_nt(q_cat[h], k_cat[h]) for h in hs]

    att = [jnp.zeros((c, c), F32) for _ in hs]
    for d in reversed(range(SUBLANES)):
        for h in hs:
            if d == 0:
                p = q3[h] * k3[h]
            else:
                p = q3[h] * pltpu.roll(k3[h], d, 1) * jnp.exp(cum3[h] - pltpu.roll(cum3[h], d, 1))
            a_d = jnp.sum(p.reshape(c, LANES), axis=-1, keepdims=True)
            att[h] = jnp.where(diag_masks[d], a_d, att[h])
    if att_off is not None:
        att = [att[h] + att_off[h] for h in hs]

    o = [o_state[h] + _bdot(att[h], v[h]) for h in hs]
    st_new = [st_prev[h] * jnp.exp(last_row[h]) + _bdot_tn(v[h], k[h] * jnp.exp(last_row[h] - cum[h])) for h in hs]
    return o, st_new


REC_L_TILE = 512
HGRN_HEADS_PER_STEP = 8
GDN_HEADS_PER_STEP = 8
GDN_STACK = 4


def _head_lanes(hh):
    return slice(hh * HEAD_DIM, (hh + 1) * HEAD_DIM)


def _rec_tiling(l):
    c = min(CHUNK, l)
    tl = min(REC_L_TILE, l)
    assert l % tl == 0 and tl % c == 0 and c % SUBLANES == 0
    return c, tl


def _hgrn_kernel(q_ref, f_ref, v_ref, g_ref, lb_ref, ng_ref, s0_ref, o_ref, s_ref, *, c, n_chunks, hb):
    @pl.when(pl.program_id(2) == 0)
    def _():
        for hh in range(hb):
            s_ref[hh] = s0_ref[hh].T

    ng = ng_ref[...]
    lb = lb_ref[...]
    tril3 = _block_tril3(c, c)
    diag_masks = _hgrn_diag_masks(c)

    def body(i, carry):
        rows = pl.ds(pl.multiple_of(i * c, c), c)
        fr = f_ref[rows, :]
        q_all = _silu(q_ref[rows, :]) * (HEAD_DIM ** -0.5)
        sg = _sigmoid(fr)
        k_all = (1.0 - lb) * (1.0 - sg)
        cum_all = _prefix_dot(tril3, jnp.log(lb + (1.0 - lb) * sg))
        lanes = [_head_lanes(hh) for hh in range(hb)]
        o, st_new = _hgrn_chunk([q_all[:, ln] for ln in lanes], [k_all[:, ln] for ln in lanes],
                                [cum_all[:, ln] for ln in lanes], [v_ref[rows, ln] for ln in lanes],
                                [s_ref[hh] for hh in range(hb)], c, diag_masks)
        for hh, ln in enumerate(lanes):
            s_ref[hh] = st_new[hh]
            ms = jnp.mean(o[hh] * o[hh], axis=-1, keepdims=True)
            o_ref[rows, ln] = o[hh] * lax.rsqrt(ms + 1e-6) * ng * _sigmoid(g_ref[rows, ln])
        return carry

    lax.fori_loop(0, n_chunks, body, 0)

    @pl.when(pl.program_id(2) == pl.num_programs(2) - 1)
    def _():
        for hh in range(hb):
            s_ref[hh] = s_ref[hh].T


def _hgrn_recurrence(proj, s0, layer, lb, norm_g):
    b, l, _ = proj.shape
    c, tl = _rec_tiling(l)
    hb = HGRN_HEADS_PER_STEP
    ng_ = HEADS // hb
    col = lambda off: pl.BlockSpec((None, tl, hb * HEAD_DIM), lambda bi, hg, t: (bi, t, off * ng_ + hg))
    state = pl.BlockSpec((None, hb, HEAD_DIM, HEAD_DIM), lambda bi, hg, t: (bi, hg, 0, 0))
    return pl.pallas_call(
        functools.partial(_hgrn_kernel, c=c, n_chunks=tl // c, hb=hb),
        grid=(b, ng_, l // tl),
        in_specs=[col(0), col(1), col(2), col(3),
                  pl.BlockSpec((1, hb * HEAD_DIM), lambda bi, hg, t: (0, hg)),
                  pl.BlockSpec((1, HEAD_DIM), lambda bi, hg, t: (0, 0)),
                  pl.BlockSpec((None, None, hb, HEAD_DIM, HEAD_DIM), lambda bi, hg, t: (layer, bi, hg, 0, 0))],
        out_specs=[col(0), state],
        out_shape=[jax.ShapeDtypeStruct((b, l, HEADS * HEAD_DIM), F32),
                   jax.ShapeDtypeStruct((b, HEADS, HEAD_DIM, HEAD_DIM), F32)],
        name="hgrn_recurrence",
        compiler_params=_cparams("parallel", "parallel", "arbitrary"),
    )(proj, proj, proj, proj, lb.reshape(1, -1), norm_g.reshape(1, -1), s0)


HIST = SUBLANES


def _gdn_conv(x_ref, r0, lanes, cw, c):
    x_ext = x_ref[pl.ds(r0, c + HIST), lanes]
    acc = x_ext * cw[CONV_W - 1:CONV_W, :]
    for w in range(CONV_W - 1):
        acc = acc + pltpu.roll(x_ext, CONV_W - 1 - w, 0) * cw[w:w + 1, :]
    return _silu(acc[HIST:, :])


def _l2norm_rows(x):
    return x * lax.rsqrt(jnp.sum(x * x, axis=-1, keepdims=True) + 1e-6)


def _gdn_masks(r, c):
    row = lax.broadcasted_iota(jnp.int32, (r, r), 0)
    col = lax.broadcasted_iota(jnp.int32, (r, r), 1)
    same = (row // c) == (col // c)
    low = same & (row >= col)
    return low, same & (row > col), jnp.where(row == col, 1.0, 0.0), _block_tril3(r, c)


def _gdn_wy(groups, c, masks):
    low, strict, eye, tril3 = masks
    gr = range(len(groups))
    qs, ks, vs, betas, gs = zip(*groups)
    r = qs[0].shape[0]
    n = r // c
    pre = [_prefix_dot(tril3, jnp.concatenate(
        [jnp.where(strict, jnp.broadcast_to(gs[g], (r, r)), 0.0), jnp.broadcast_to(gs[g], (r, LANES))], axis=1))
        for g in gr]
    cum = [pre[g][:, r:] for g in gr]
    decay = [jnp.where(low, jnp.exp(jnp.where(low, pre[g][:, :r], 0.0)), 0.0) for g in gr]
    ecum = [jnp.exp(cum[g]) for g in gr]
    kb = [ks[g] * betas[g] for g in gr]
    kk = [_bdot_nt(jnp.concatenate([kb[g], qs[g]], axis=0), ks[g]) for g in gr]
    p = [-jnp.where(strict, kk[g][:r] * decay[g], 0.0) for g in gr]
    att = [kk[g][r:] * decay[g] for g in gr]
    t_inv = [eye + p[g] for g in gr]
    x = p
    for _ in range(int(math.log2(c)) - 1):
        x = [_bdot(x[g], x[g]) for g in gr]
        t_inv = [t_inv[g] + _bdot(x[g], t_inv[g]) for g in gr]
    uw = [_bdot(t_inv[g], jnp.concatenate([vs[g] * betas[g], kb[g] * ecum[g]], axis=1)) for g in gr]
    qe = [qs[g] * ecum[g] for g in gr]
    last = [jnp.concatenate([jnp.broadcast_to(cum[g][(h + 1) * c - 1:(h + 1) * c, :], (c, LANES))
                             for h in range(n)], axis=0) for g in gr]
    kt = [ks[g] * jnp.exp(last[g] - cum[g]) for g in gr]
    return [(uw[g], att[g], qe[g], kt[g], jnp.exp(last[g])) for g in gr]


def _gdn_state_step(wy, s_prev, c):
    gr = range(len(wy))
    uw, att, qe, kt, ecl = zip(*wy)
    n = len(s_prev[0])
    head_rows = [slice(h * c, (h + 1) * c) for h in range(n)]
    ws = [[_bdot(jnp.concatenate([uw[g][rows, HEAD_DIM:], qe[g][rows]], axis=0), s_prev[g][h])
           for h, rows in enumerate(head_rows)] for g in gr]
    v_new = [jnp.concatenate([uw[g][rows, :HEAD_DIM] - ws[g][h][:c] for h, rows in enumerate(head_rows)], axis=0)
             for g in gr]
    o = [jnp.concatenate([ws[g][h][c:] for h in range(n)], axis=0) + _bdot(att[g], v_new[g]) for g in gr]
    s_new = [[s_prev[g][h] * ecl[g][h * c:h * c + 1, :] + _bdot_tn(kt[g][rows], v_new[g][rows])
              for h, rows in enumerate(head_rows)] for g in gr]
    return [(o[g], s_new[g]) for g in gr]


def _gdn_kernel(q_ref, k_ref, v_ref, z_ref, ba_ref, c0q_ref, c0k_ref, c0v_ref, cwq_ref, cwk_ref, cwv_ref,
                an_ref, dt_ref, ng_ref, s0_ref, o_ref, s_ref, qx_ref, kx_ref, vx_ref, *wy_refs,
                c, n_chunks, tl, hb):
    hg = pl.program_id(1)
    t = pl.program_id(2)
    staged = ((q_ref, c0q_ref, qx_ref), (k_ref, c0k_ref, kx_ref), (v_ref, c0v_ref, vx_ref))

    @pl.when(t == 0)
    def _():
        s_ref[...] = s0_ref[...]
        for _, c0, dst in staged:
            dst[0:HIST, :] = jnp.zeros((HIST, hb * HEAD_DIM), F32)
            dst[HIST - (CONV_W - 1):HIST, :] = c0[...]

    @pl.when(t > 0)
    def _():
        for _, _, dst in staged:
            dst[0:HIST, :] = dst[tl:tl + HIST, :]

    for src, _, dst in staged:
        dst[HIST:HIST + tl, :] = src[...]
    ng = ng_ref[...]
    lane = lax.broadcasted_iota(jnp.int32, (c, LANES), 1)
    masks = _gdn_masks(GDN_STACK * c, c)
    n_groups = hb // GDN_STACK
    stack = lambda parts: jnp.concatenate(parts, axis=0)

    def chunk_groups(i):
        r0 = pl.multiple_of(i * c, c)
        ba = ba_ref[pl.ds(r0, c), :]
        groups = []
        for h0 in range(0, hb, GDN_STACK):
            qs, ks, vs, betas, gs = [], [], [], [], []
            for hh in range(h0, h0 + GDN_STACK):
                ln = _head_lanes(hh)
                head = hg * hb + hh
                qs.append(_l2norm_rows(_gdn_conv(qx_ref, r0, ln, cwq_ref[:, ln], c)) * (HEAD_DIM ** -0.5))
                ks.append(_l2norm_rows(_gdn_conv(kx_ref, r0, ln, cwk_ref[:, ln], c)))
                vs.append(_gdn_conv(vx_ref, r0, ln, cwv_ref[:, ln], c))
                betas.append(_sigmoid(jnp.sum(jnp.where(lane == head, ba, 0.0), axis=-1, keepdims=True)))
                a_raw = jnp.sum(jnp.where(lane == HEADS + head, ba, 0.0), axis=-1, keepdims=True)
                xa = a_raw + dt_ref[:, ln][:, 0:1]
                softplus = jnp.maximum(xa, 0.0) + jnp.log(1.0 + jnp.exp(-jnp.abs(xa)))
                gs.append(an_ref[:, ln][:, 0:1] * softplus)
            groups.append((stack(qs), stack(ks), stack(vs), stack(betas), stack(gs)))
        return groups

    per_iter = 2 if n_chunks % 2 == 0 else 1

    def wy_body(ip, carry):
        chunks = [ip * per_iter + j for j in range(per_iter)]
        wy = _gdn_wy([g for i in chunks for g in chunk_groups(i)], c, masks)
        for j, i in enumerate(chunks):
            for gi in range(n_groups):
                for ref, val in zip(wy_refs, wy[j * n_groups + gi]):
                    ref[i, gi] = val
        return carry

    lax.fori_loop(0, n_chunks // per_iter, wy_body, 0)

    def state_body(i, carry):
        rows = pl.ds(pl.multiple_of(i * c, c), c)
        wy = [tuple(ref[i, gi] for ref in wy_refs) for gi in range(n_groups)]
        s_prev = [[s_ref[gi * GDN_STACK + j] for j in range(GDN_STACK)] for gi in range(n_groups)]
        for gi, (o, s_new) in enumerate(_gdn_state_step(wy, s_prev, c)):
            for j in range(GDN_STACK):
                hh = gi * GDN_STACK + j
                ln = _head_lanes(hh)
                s_ref[hh] = s_new[j]
                oh = o[j * c:(j + 1) * c]
                ms = jnp.mean(oh * oh, axis=-1, keepdims=True)
                o_ref[rows, ln] = oh * lax.rsqrt(ms + 1e-6) * ng * _silu(z_ref[rows, ln])
        return carry

    lax.fori_loop(0, n_chunks, state_body, 0)


def _gdn_recurrence(proj, conv0, s0, layer, conv_w, a_log, dt_bias, norm_g):
    b, l, _ = proj.shape
    c, tl = _rec_tiling(l)
    hb = GDN_HEADS_PER_STEP
    ng_ = HEADS // hb
    w = hb * HEAD_DIM
    col = lambda off: pl.BlockSpec((None, tl, w), lambda bi, hg, t: (bi, t, off * ng_ + hg))
    c0 = lambda off: pl.BlockSpec((None, None, CONV_W - 1, w), lambda bi, hg, t: (layer, bi, 0, off * ng_ + hg))
    cw = lambda off: pl.BlockSpec((CONV_W, w), lambda bi, hg, t: (0, off * ng_ + hg))
    per_head = pl.BlockSpec((1, w), lambda bi, hg, t: (0, hg))
    state = pl.BlockSpec((None, hb, HEAD_DIM, HEAD_DIM), lambda bi, hg, t: (bi, hg, 0, 0))
    a_neg = jnp.repeat(-jnp.exp(a_log.astype(F32)), HEAD_DIM)[None, :]
    dt_b = jnp.repeat(dt_bias.astype(F32), HEAD_DIM)[None, :]
    xbuf = pltpu.VMEM((HIST + tl, w), F32)
    return pl.pallas_call(
        functools.partial(_gdn_kernel, c=c, n_chunks=tl // c, tl=tl, hb=hb),
        grid=(b, ng_, l // tl),
        in_specs=[col(0), col(1), col(2), col(3),
                  pl.BlockSpec((None, tl, LANES), lambda bi, hg, t: (bi, t, 4 * HEADS)),
                  c0(0), c0(1), c0(2), cw(0), cw(1), cw(2),
                  per_head, per_head,
                  pl.BlockSpec((1, HEAD_DIM), lambda bi, hg, t: (0, 0)),
                  pl.BlockSpec((None, None, hb, HEAD_DIM, HEAD_DIM), lambda bi, hg, t: (layer, bi, hg, 0, 0))],
        out_specs=[col(0), state],
        out_shape=[jax.ShapeDtypeStruct((b, l, HEADS * HEAD_DIM), F32),
                   jax.ShapeDtypeStruct((b, HEADS, HEAD_DIM, HEAD_DIM), F32)],
        scratch_shapes=[xbuf, xbuf, xbuf] + [
            pltpu.VMEM((tl // c, hb // GDN_STACK, GDN_STACK * c, width), F32)
            for width in (2 * HEAD_DIM, GDN_STACK * c, HEAD_DIM, HEAD_DIM, HEAD_DIM)],
        name="gdn_recurrence",
        compiler_params=_cparams("parallel", "parallel", "arbitrary"),
    )(proj, proj, proj, proj, proj, conv0, conv0, conv0, conv_w, conv_w, conv_w,
      a_neg, dt_b, norm_g.reshape(1, -1), s0)


def _rel_bucket(dist):
    exact = REL_BUCKETS // 2
    d = jnp.maximum(dist, exact).astype(F32)
    large = exact + (jnp.log(d / exact) / math.log(REL_MAX_DIST / exact) * (REL_BUCKETS - exact)).astype(jnp.int32)
    return jnp.where(dist < exact, dist, jnp.minimum(large, REL_BUCKETS - 1))


def _top_mask(gate, n_valid, k, axis):
    idx = lax.broadcasted_iota(jnp.int32, gate.shape, axis)
    big = jnp.int32(2 ** 30)
    work = jnp.where(idx < n_valid, gate, NEG)
    sel = jnp.zeros(gate.shape, F32)
    for r in range(k):
        mx = jnp.max(work, axis=axis, keepdims=True)
        first = jnp.min(jnp.where(work == mx, idx, big), axis=axis, keepdims=True)
        pick = (idx == first) & (idx < n_valid)
        sel = jnp.where(pick, 1.0, sel)
        work = jnp.where(idx == first, -jnp.inf, work)
    return sel


MOBA_HEADS_PER_STEP = 4


def _moba_prompt_kernel(q_ref, k_ref, v_ref, bd_ref, bp_ref, bf_ref, o_ref, means_ref, sel_ref, *, nb, hb):
    i = pl.program_id(2)
    blk = MOBA_BLOCK

    @pl.when(i == 0)
    def _():
        means_ref[...] = jnp.zeros_like(means_ref)
        for hh in range(hb):
            for n in range(nb):
                means_ref[hh, n:n + 1, :] = jnp.mean(k_ref[n * blk:(n + 1) * blk, _head_lanes(hh)],
                                                     axis=0, keepdims=True)

    causal = ~_tri(blk, strict=True)
    r0 = pl.multiple_of(i * blk, blk)
    heads = range(hb)
    qs = [q_ref[:, _head_lanes(hh)] * (HEAD_DIM ** -0.5) for hh in heads]
    qbs = [q.astype(BF16) for q in qs]
    gates = [lax.dot_general(means_ref[hh], qs[hh], (((1,), (1,)), ((), ())),
                             preferred_element_type=F32, precision=HI) for hh in heads]
    qk = [_bdot_nt(k_ref[pl.ds(r0, blk), _head_lanes(hh)], qbs[hh]) for hh in heads]
    for hh in heads:
        sel_ref[hh] = _top_mask(gates[hh], i, MOBA_TOPK, 0)
    s0 = [jnp.where(causal, qk[hh] + bd_ref[hh], NEG) for hh in heads]
    m0 = [jnp.max(s, axis=0, keepdims=True) for s in s0]
    p0 = [jnp.exp(s0[hh] - m0[hh]) for hh in heads]
    init = [(m0[hh], jnp.sum(p0[hh], axis=0, keepdims=True),
             _bdot_tn(v_ref[pl.ds(r0, blk), _head_lanes(hh)], p0[hh])) for hh in heads]

    def past_blocks(n, count, carry, bias_of_head, limit):
        rn = pl.multiple_of(n * blk, blk)
        rows = pl.ds(rn, count * blk)
        heads = range(hb)
        qk = [_bdot_nt(k_ref[rows, _head_lanes(hh)], qbs[hh]) for hh in heads]
        sn = []
        for hh in heads:
            parts = []
            for j in range(count):
                picked = sel_ref[hh, pl.ds(n + j, 1), :] * (n + j < limit).astype(F32) > 0.0
                parts.append(jnp.where(picked, qk[hh][j * blk:(j + 1) * blk] + bias_of_head(hh), NEG))
            sn.append(parts[0] if count == 1 else jnp.concatenate(parts, axis=0))
        m_new = [jnp.maximum(carry[hh][0], jnp.max(sn[hh], axis=0, keepdims=True)) for hh in heads]
        pn = [jnp.exp(sn[hh] - m_new[hh]) for hh in heads]
        pv = [_bdot_tn(v_ref[rows, _head_lanes(hh)], pn[hh]) for hh in heads]
        out = []
        for hh in heads:
            m, l, acc = carry[hh]
            alpha = jnp.exp(m - m_new[hh])
            out.append((m_new[hh], l * alpha + jnp.sum(pn[hh], axis=0, keepdims=True), acc * alpha + pv[hh]))
        return tuple(out)

    n_far = jnp.maximum(i - 1, 0)
    carry = lax.fori_loop(0, (n_far + 1) // 2,
                          lambda n2, cr: past_blocks(2 * n2, 2, cr, lambda hh: bf_ref[:, _head_lanes(hh)][:, 0:1],
                                                     n_far),
                          tuple(init))
    final = past_blocks(n_far, 1, carry, lambda hh: bp_ref[hh], i)
    for hh in range(hb):
        _, l, acc = final[hh]
        o_ref[:, _head_lanes(hh)] = (acc / l).T


def _bias_of_distance(rel_bias, dist):
    bucket = _rel_bucket(jnp.maximum(dist, 0))[None]
    out = jnp.zeros((HEADS,) + dist.shape, F32)
    for b in range(REL_BUCKETS):
        out = jnp.where(bucket == b, rel_bias[b].astype(F32).reshape((HEADS,) + (1,) * dist.ndim), out)
    return out


def _moba_bias_tables(rel_bias, blk):
    t = jnp.arange(blk, dtype=jnp.int32)
    d0 = t[None, :] - t[:, None]
    bias_diag = _bias_of_distance(rel_bias, d0)
    bias_prev = _bias_of_distance(rel_bias, d0 + blk)
    far = jnp.repeat(rel_bias[REL_BUCKETS - 1].astype(F32), HEAD_DIM)[None, :]
    return bias_diag, bias_prev, far


def _moba_prompt(q, k, v, rel_bias):
    b, l, _ = q.shape
    blk = MOBA_BLOCK
    assert l % blk == 0 and l // blk <= LANES
    assert 2 * blk > REL_MAX_DIST
    nb = l // blk
    bias_diag, bias_prev, far = _moba_bias_tables(rel_bias, blk)
    hb = MOBA_HEADS_PER_STEP
    ng_ = HEADS // hb
    w = hb * HEAD_DIM
    tile = pl.BlockSpec((hb, blk, blk), lambda bi, hg, i: (hg, 0, 0))
    return pl.pallas_call(
        functools.partial(_moba_prompt_kernel, nb=nb, hb=hb),
        grid=(b, ng_, nb),
        in_specs=[pl.BlockSpec((None, blk, w), lambda bi, hg, i: (bi, i, hg)),
                  pl.BlockSpec((None, l, w), lambda bi, hg, i: (bi, 0, hg)),
                  pl.BlockSpec((None, l, w), lambda bi, hg, i: (bi, 0, hg)),
                  tile, tile,
                  pl.BlockSpec((1, w), lambda bi, hg, i: (0, hg))],
        out_specs=pl.BlockSpec((None, blk, w), lambda bi, hg, i: (bi, i, hg)),
        out_shape=jax.ShapeDtypeStruct((b, l, HEADS * HEAD_DIM), F32),
        scratch_shapes=[pltpu.VMEM((hb, _round_up(nb, SUBLANES), HEAD_DIM), F32),
                        pltpu.VMEM((hb, _round_up(nb, SUBLANES), blk), F32)],
        name="moba_prompt",
        compiler_params=_cparams("parallel", "parallel", "arbitrary"),
    )(q, k, v, bias_diag, bias_prev, far)


def _block_diag_rows(x, lq):
    return jnp.concatenate([x[h * lq:(h + 1) * lq, h * HEAD_DIM:(h + 1) * HEAD_DIM] for h in range(HEADS)], axis=0)


MOBA_SAMPLE_BLOCKS_PER_STEP = 4


def _moba_sample_kernel(pt_ref, *refs, nb, lq, bps):
    n_pages = bps * (MOBA_BLOCK // PAGE_SIZE)
    k_refs, v_refs = refs[:n_pages], refs[n_pages:2 * n_pages]
    (q_ref, kn_ref, vn_ref, bl_ref, bc_ref, bf_ref, o_ref,
     qbd_ref, m_ref, l_ref, acc_ref, means_ref) = refs[2 * n_pages:]
    step = pl.program_id(1)
    d_model = HEADS * HEAD_DIM
    hq = HEADS * lq

    @pl.when(step == 0)
    def _():
        q = q_ref[...] * (HEAD_DIM ** -0.5)
        rows = lax.broadcasted_iota(jnp.int32, (d_model, hq), 0) // HEAD_DIM
        cols = lax.broadcasted_iota(jnp.int32, (d_model, hq), 1)
        spread = (lax.broadcasted_iota(jnp.int32, (lq, hq), 1) % lq
                  == lax.broadcasted_iota(jnp.int32, (lq, hq), 0)).astype(F32)
        qt = lax.dot_general(q, spread, (((0,), (0,)), ((), ())),
                             preferred_element_type=F32, precision=HI)
        qbd_ref[...] = jnp.where(rows == cols // lq, qt, 0.0)
        means_ref[...] = jnp.zeros_like(means_ref)
        m_ref[...] = jnp.zeros_like(m_ref)
        l_ref[...] = jnp.zeros_like(l_ref)

    qbd = qbd_ref[...]

    def rows_by_lanes(*pages):
        return jnp.concatenate(
            [jnp.concatenate([pg[pl.ds(h, PAGE_SIZE, stride=HEADS), :] for h in range(HEADS)], axis=1)
             for pg in pages], axis=0)

    js = range(bps)
    ns = [step * bps + j for j in js]
    kblk = [rows_by_lanes(k_refs[2 * j], k_refs[2 * j + 1]) for j in js]
    vblk = [rows_by_lanes(v_refs[2 * j], v_refs[2 * j + 1]) for j in js]
    s = [_bdot(kblk[j], qbd) + jnp.where(ns[j] == nb - 1, bl_ref[...], bf_ref[...]) for j in js]
    for j in js:
        means_ref[pl.ds(ns[j], 1), :] = jnp.mean(kblk[j], axis=0, keepdims=True)
    m = [jnp.max(s[j], axis=0, keepdims=True) for j in js]
    p = [jnp.exp(s[j] - m[j]) for j in js]
    pv = [_bdot_tn(p[j], vblk[j]) for j in js]
    for j in js:
        m_ref[pl.ds(ns[j], 1), :] = m[j]
        l_ref[pl.ds(ns[j], 1), :] = jnp.sum(p[j], axis=0, keepdims=True)
        acc_ref[ns[j]] = _block_diag_rows(pv[j], lq)

    @pl.when(step == pl.num_programs(1) - 1)
    def _():
        k_new = kn_ref[...]
        v_new = vn_ref[...]
        jq = lax.broadcasted_iota(jnp.int32, (lq, hq), 1) % lq
        jk = lax.broadcasted_iota(jnp.int32, (lq, hq), 0)
        sc = jnp.where(jk <= jq, _bdot(k_new, qbd) + bc_ref[...], NEG)
        mc = jnp.max(sc, axis=0, keepdims=True)
        pc = jnp.exp(sc - mc)
        lc = jnp.sum(pc, axis=0, keepdims=True)
        oc = _block_diag_rows(_bdot_tn(pc, v_new), lq)

        gate = _fdot(means_ref[...], qbd)
        sel = _top_mask(gate, nb, MOBA_TOPK, 0)
        m_all = m_ref[...]
        m_tot = jnp.maximum(jnp.max(jnp.where(sel > 0.0, m_all, NEG), axis=0, keepdims=True), mc)
        w = jnp.where(sel > 0.0, jnp.exp(jnp.where(sel > 0.0, m_all - m_tot, 0.0)), 0.0)
        wc = jnp.exp(mc - m_tot)
        denom = jnp.sum(w * l_ref[...], axis=0, keepdims=True) + wc * lc
        nbp = w.shape[0]
        w_all = jnp.concatenate([w, wc, jnp.zeros((SUBLANES - 1, hq), F32)], axis=0) / denom
        ne = nbp + SUBLANES
        eye = (lax.broadcasted_iota(jnp.int32, (ne, ne), 0)
               == lax.broadcasted_iota(jnp.int32, (ne, ne), 1)).astype(F32)
        wt = lax.dot_general(w_all, eye, (((0,), (0,)), ((), ())),
                             preferred_element_type=F32, precision=HI)
        out = wt[:, nbp:nbp + 1] * oc
        for b_i in range(nb):
            out = out + wt[:, b_i:b_i + 1] * acc_ref[b_i]
        o_ref[...] = out


def _moba_sample(q, k, v, cache_k, cache_v, layer, page_table, rel_bias):
    b, lq, _ = q.shape
    n_pages = page_table.shape[1]
    d_model = HEADS * HEAD_DIM
    pages_per_block = MOBA_BLOCK // PAGE_SIZE
    assert pages_per_block == 2 and n_pages % pages_per_block == 0 and lq <= MOBA_BLOCK
    assert cache_k.shape[2:] == (PAGE_SIZE, HEADS, HEAD_DIM)
    nb = n_pages // pages_per_block
    assert nb >= 1 and MOBA_BLOCK >= REL_MAX_DIST
    nbp = _round_up(nb, SUBLANES)
    hq = HEADS * lq
    jq = jnp.arange(lq, dtype=jnp.int32)
    t = jnp.arange(MOBA_BLOCK, dtype=jnp.int32)
    by_cols = lambda tab: jnp.transpose(tab, (1, 0, 2)).reshape(tab.shape[1], hq)
    bias_last = by_cols(_bias_of_distance(rel_bias, (MOBA_BLOCK + jq)[None, :] - t[:, None]))
    bias_cur = by_cols(_bias_of_distance(rel_bias, jq[None, :] - jq[:, None]))
    bias_far = jnp.repeat(rel_bias[REL_BUCKETS - 1].astype(F32), lq)[None, :]

    n_c, n_pool = cache_k.shape[:2]
    cache_k = cache_k.reshape(n_c, n_pool, PAGE_SIZE * HEADS, HEAD_DIM)
    cache_v = cache_v.reshape(n_c, n_pool, PAGE_SIZE * HEADS, HEAD_DIM)
    bps = math.gcd(MOBA_SAMPLE_BLOCKS_PER_STEP, nb)
    pages_per_step = bps * pages_per_block
    page = lambda off: pl.BlockSpec((None, None, PAGE_SIZE * HEADS, HEAD_DIM),
                                    lambda bi, n, pt: (layer, pt[bi, pages_per_step * n + off], 0, 0))
    pages = [page(off) for off in range(pages_per_step)]
    full = lambda shape: pl.BlockSpec(shape, lambda bi, n, pt: (0,) * len(shape))
    grid_spec = pltpu.PrefetchScalarGridSpec(
        num_scalar_prefetch=1,
        grid=(b, nb // bps),
        in_specs=pages + pages + [
                  *([pl.BlockSpec((None, lq, d_model), lambda bi, n, pt: (bi, 0, 0))] * 3),
                  full((MOBA_BLOCK, hq)), full((lq, hq)), full((1, hq))],
        out_specs=pl.BlockSpec((None, hq, HEAD_DIM), lambda bi, n, pt: (bi, 0, 0)),
        scratch_shapes=[pltpu.VMEM((d_model, hq), F32),
                        pltpu.VMEM((nbp, hq), F32),
                        pltpu.VMEM((nbp, hq), F32),
                        pltpu.VMEM((nb, hq, HEAD_DIM), F32),
                        pltpu.VMEM((nbp, d_model), F32)])
    return pl.pallas_call(
        functools.partial(_moba_sample_kernel, nb=nb, lq=lq, bps=bps),
        grid_spec=grid_spec,
        out_shape=jax.ShapeDtypeStruct((b, hq, HEAD_DIM), F32),
        name="moba_sample",
        compiler_params=_cparams("parallel", "arbitrary"),
    )(page_table, *([cache_k] * pages_per_step), *([cache_v] * pages_per_step), q, k, v,
      bias_last, bias_cur, bias_far)


def _trunk(x, mods, s_hgrn, s_conv, s_gdn, paged, W):
    b, l, d = x.shape
    tiling = _row_tile(b, l)
    x2 = x.reshape(b * l, d)
    new_h, new_c, new_g, new_k, new_v = [], [], [], [], []
    for i in range(DEPTH):
        mod = _mod_layout(mods[i], b, l)
        x2 = _ffn_block(x2, mod, tiling, 0, W['ffn1'], i, W['ln_g'][i, 0], W['ln_b'][i, 0])
        kind, li = i % N_MIXERS, i // N_MIXERS
        if kind == 0:
            proj = _proj_in(x2, mod, tiling, 1, W['hgrn_w_in'][li], 1024)
            o, s = _hgrn_recurrence(proj.reshape(b, l, -1), s_hgrn, li, W['hgrn_lb'][li], W['hgrn_norm_g'][li])
            new_h.append(s)
            w_o = W['hgrn_w_o'][li]
            o2 = o.reshape(b * l, d)
        elif kind == 1:
            proj = _proj_in(x2, mod, tiling, 1, W['gdn_w_in'][li], W['gdn_w_in'][li].shape[1] // 3)
            proj = proj.reshape(b, l, -1)
            o, s = _gdn_recurrence(proj, s_conv, s_gdn, li, W['gdn_conv_w'][li], W['gdn_a_log'][li],
                                   W['gdn_dt_bias'][li], W['gdn_norm_g'][li])
            n_qkv = 3 * HEADS * HEAD_DIM
            assert l >= CONV_W - 1
            new_c.append(proj[:, l - (CONV_W - 1):, :n_qkv])
            new_g.append(s)
            w_o = W['gdn_w_o'][li]
            o2 = o.reshape(b * l, d)
        else:
            q, k, v = (t.reshape(b, l, d) for t in _proj_in(x2, mod, tiling, 1, W['moba_w_qkv'][li], 1024, n_out=3))
            if paged is None:
                o2 = _moba_prompt(q, k, v, W['rel_bias']).reshape(b * l, d)
            else:
                ck, cv, pt = paged
                o = _moba_sample(q, k, v, ck, cv, li, pt, W['rel_bias'])
                o2 = jnp.transpose(o.reshape(b, HEADS, l, HEAD_DIM), (0, 2, 1, 3)).reshape(b * l, d)
            new_k.append(k.reshape(b, l, HEADS, HEAD_DIM))
            new_v.append(v.reshape(b, l, HEADS, HEAD_DIM))
            w_o = W['moba_w_o'][li]
        x2 = _proj_out(o2, x2, mod, tiling, 1, w_o, W['ln_g'][i, 1], W['ln_b'][i, 1])
        x2 = _ffn_block(x2, mod, tiling, 2, W['ffn2'], i, W['ln_g'][i, 2], W['ln_b'][i, 2])
    return (x2.reshape(b, l, d), jnp.stack(new_h), jnp.stack(new_c), jnp.stack(new_g),
            jnp.stack(new_k), jnp.stack(new_v))


def kernel(x_prompt, x_sample, state_hgrn, state_gdn_conv, state_gdn, cache_k, cache_v, page_table, c_prompt, c_sample, w_ada, b_ada, ln_g, ln_b, w_ffn1_up, w_ffn1_down, w_ffn2_up, w_ffn2_down, hgrn_w_in, hgrn_lb_logits, hgrn_norm_g, hgrn_w_o, gdn_w_in, gdn_conv_w, gdn_a_log, gdn_dt_bias, gdn_norm_g, gdn_w_o, moba_w_qkv, moba_w_o, rel_bias):
    bp, _, d = x_prompt.shape
    bs = x_sample.shape[0]
    assert d == HEADS * HEAD_DIM

    nc = _round_up(bp + bs, SUBLANES)
    c_all = jnp.pad(jnp.concatenate([c_prompt, c_sample], axis=0), ((0, nc - bp - bs), (0, 0)))
    mods = _ada_mods(c_all, w_ada, b_ada).reshape(DEPTH, nc, 9, d)

    gdn_n = _round_up(gdn_w_in.shape[2], 3 * LANES)
    lbp = jax.nn.softmax(hgrn_lb_logits.astype(F32), axis=0)
    W = {
        'ffn1': _prep_ffn_weights(w_ffn1_up, w_ffn1_down), 'ffn2': _prep_ffn_weights(w_ffn2_up, w_ffn2_down),
        'ln_g': ln_g, 'ln_b': ln_b,
        'hgrn_w_in': hgrn_w_in.astype(BF16), 'hgrn_lb': jnp.cumsum(lbp, axis=0) - lbp[0],
        'hgrn_norm_g': hgrn_norm_g, 'hgrn_w_o': hgrn_w_o.astype(BF16),
        'gdn_w_in': jnp.pad(gdn_w_in, ((0, 0), (0, 0), (0, gdn_n - gdn_w_in.shape[2]))).astype(BF16),
        'gdn_conv_w': gdn_conv_w, 'gdn_a_log': gdn_a_log, 'gdn_dt_bias': gdn_dt_bias,
        'gdn_norm_g': gdn_norm_g, 'gdn_w_o': gdn_w_o.astype(BF16),
        'moba_w_qkv': moba_w_qkv.astype(BF16), 'moba_w_o': moba_w_o.astype(BF16), 'rel_bias': rel_bias,
    }

    n_a, n_b = state_hgrn.shape[0], state_gdn.shape[0]
    z_hgrn = jnp.zeros((n_a, bp) + state_hgrn.shape[2:], F32)
    z_conv = jnp.zeros((n_b, bp) + state_gdn_conv.shape[2:], F32)
    z_gdn = jnp.zeros((n_b, bp) + state_gdn.shape[2:], F32)
    y_p, hgrn_p, conv_p, gdn_p, k_p, v_p = _trunk(x_prompt, mods[:, :bp], z_hgrn, z_conv, z_gdn, None, W)

    y_s, hgrn_s, conv_s, gdn_s, k_s, v_s = _trunk(x_sample, mods[:, bp:bp + bs], state_hgrn, state_gdn_conv,
                                                  state_gdn, (cache_k, cache_v, page_table), W)
    return (y_p, y_s, hgrn_p, hgrn_s, conv_p, conv_s, gdn_p, gdn_s, k_p, k_s, v_p, v_s)
```

```python
import functools
import math

import jax
import jax.numpy as jnp
from jax import lax
from jax.experimental import pallas as pl
from jax.experimental.pallas import tpu as pltpu

F32 = jnp.float32
BF16 = jnp.bfloat16
HI = lax.Precision.HIGHEST

DEPTH = 4
N_MIXERS = 3
HEADS = 8
HEAD_DIM = 128
CONV_W = 4
CHUNK = 64
MOBA_BLOCK = 256
MOBA_TOPK = 3
REL_BUCKETS = 32
REL_MAX_DIST = 128
PAGE_SIZE = 128
DN_ALPHA = (2 * DEPTH) ** 0.25
LN_EPS = 1e-5
NEG = -1e30

LANES = 128
SUBLANES = 8
MXU_WIDTH = 256
VMEM_LIMIT_BYTES = 56 * 1024 * 1024


def _cparams(*sem):
    return pltpu.CompilerParams(dimension_semantics=sem, vmem_limit_bytes=VMEM_LIMIT_BYTES)


def _round_up(n, m):
    return (n + m - 1) // m * m


def _bdot(a, b):
    return jnp.dot(a.astype(BF16), b.astype(BF16), preferred_element_type=F32)


def _bdot_nt(a, b):
    return lax.dot_general(a.astype(BF16), b.astype(BF16), (((1,), (1,)), ((), ())),
                           preferred_element_type=F32)


def _bdot_tn(a, b):
    return lax.dot_general(a.astype(BF16), b.astype(BF16), (((0,), (0,)), ((), ())),
                           preferred_element_type=F32)


def _fdot(a, b):
    return jnp.dot(a, b, preferred_element_type=F32, precision=HI)


def _block_tril3(r, c):
    row = lax.broadcasted_iota(jnp.int32, (r, 3 * r), 0)
    col = lax.broadcasted_iota(jnp.int32, (r, 3 * r), 1) % r
    return jnp.where(((row // c) == (col // c)) & (row >= col), 1.0, 0.0).astype(BF16)


def _prefix_dot(sel3, x):
    hi = x.astype(BF16)
    rest = x - hi.astype(F32)
    mid = rest.astype(BF16)
    lo = (rest - mid.astype(F32)).astype(BF16)
    return jnp.dot(sel3, jnp.concatenate([hi, mid, lo], axis=0), preferred_element_type=F32)


def _sigmoid(x):
    return 1.0 / (1.0 + jnp.exp(-x))


def _silu(x):
    return x * _sigmoid(x)


def _layer_norm_rows(z, g, b):
    mu = jnp.mean(z, axis=-1, keepdims=True)
    zc = z - mu
    var = jnp.mean(zc * zc, axis=-1, keepdims=True)
    return zc * lax.rsqrt(var + LN_EPS) * g + b


def _ada_kernel(c_ref, w_ref, b_ref, o_ref):
    o_ref[...] = _bdot(_silu(c_ref[...]), w_ref[...]) + b_ref[...]


def _ada_mods(c_all, w_ada, b_ada):
    depth, d, n = w_ada.shape
    nb = c_all.shape[0]
    tn = n // 8
    assert n % 8 == 0 and tn % LANES == 0
    return pl.pallas_call(
        _ada_kernel,
        grid=(depth, n // tn),
        in_specs=[pl.BlockSpec((nb, d), lambda l, j: (0, 0)),
                  pl.BlockSpec((None, d, tn), lambda l, j: (l, 0, j)),
                  pl.BlockSpec((None, 1, tn), lambda l, j: (l, 0, j))],
        out_specs=pl.BlockSpec((None, nb, tn), lambda l, j: (l, 0, j)),
        out_shape=jax.ShapeDtypeStruct((depth, nb, n), F32),
        name="ada_mods",
        compiler_params=_cparams("parallel", "parallel"),
    )(c_all, w_ada, b_ada.reshape(depth, 1, n))


MAX_ROW_TILE = 1024


def _row_tile(batch, length):
    if length >= MXU_WIDTH:
        tm = MAX_ROW_TILE
        while length % tm:
            tm //= 2
        return tm, length // tm, True
    rows = batch * length
    assert rows % SUBLANES == 0 and rows <= MAX_ROW_TILE
    return rows, 1, False


def _mod_layout(m, batch, length):
    tm, _, per_batch = _row_tile(batch, length)
    mt = jnp.transpose(m, (1, 0, 2))
    if per_batch:
        return mt[:, :, None, :]
    return jnp.repeat(mt, length, axis=1)[:, None, :, :]


def _mod_spec(mod, tiles_per_block):
    _, _, rm, d = mod.shape
    return pl.BlockSpec((9, None, rm, d), lambda i, *_: (0, i // tiles_per_block, 0, 0))


FF_CHUNK = MXU_WIDTH


def _ffn_sublayer(x_ref, mod_ref, wa_ref, wu_ref, wd_ref, g_ref, b_ref, o_ref, h_ref, j, n_chunks):
    h_ref[...] = (x_ref[...] * (1.0 + mod_ref[3 * j + 1]) + mod_ref[3 * j]).astype(BF16)

    for c in range(n_chunks):
        cols = slice(c * FF_CHUNK, (c + 1) * FF_CHUNK)
        h = h_ref[...]
        a = jnp.dot(h, wa_ref[:, cols], preferred_element_type=F32)
        u = jnp.dot(h, wu_ref[:, cols], preferred_element_type=F32)
        act = (_silu(a) * u).astype(BF16)
        part = jnp.dot(act, wd_ref[cols, :], preferred_element_type=F32)
        if c == 0:
            o_ref[...] = part
        else:
            o_ref[...] += part
    z = DN_ALPHA * x_ref[...] + (1.0 + mod_ref[3 * j + 2]) * (0.5 * o_ref[...])
    o_ref[...] = _layer_norm_rows(z, g_ref[...], b_ref[...])


def _ffn_kernel(x_ref, mod_ref, wa_ref, wu_ref, wd_ref, g_ref, b_ref, o_ref, h_ref, *, j, n_chunks):
    _ffn_sublayer(x_ref, mod_ref, wa_ref, wu_ref, wd_ref, g_ref, b_ref, o_ref, h_ref, j, n_chunks)


def _mixer_out_ffn_kernel(m_ref, x_ref, mod_ref, wo_ref, g1_ref, b1_ref, wa_ref, wu_ref, wd_ref, g2_ref, b2_ref,
                          o_ref, h_ref, x1_ref, *, n_chunks):
    y = jnp.dot(m_ref[...].astype(BF16), wo_ref[...], preferred_element_type=F32)
    z = DN_ALPHA * x_ref[...] + (1.0 + mod_ref[3 * 1 + 2]) * y
    x1_ref[...] = _layer_norm_rows(z, g1_ref[...], b1_ref[...])
    _ffn_sublayer(x1_ref, mod_ref, wa_ref, wu_ref, wd_ref, g2_ref, b2_ref, o_ref, h_ref, 2, n_chunks)


def _mixer_out_ffn(m2d, x2d, mod, tiling, w_o, weights, layer, ln_g, ln_b):
    tm, tpb, _ = tiling
    rows, d = x2d.shape
    k = m2d.shape[1]
    w_a, w_u, w_down = weights
    fp = w_a.shape[-1]
    row = lambda: pl.BlockSpec((1, d), lambda i: (0, 0))
    resident = lambda shape, idx: pl.BlockSpec(shape, idx, pipeline_mode=pl.Buffered(1))
    return pl.pallas_call(
        functools.partial(_mixer_out_ffn_kernel, n_chunks=fp // FF_CHUNK),
        grid=(rows // tm,),
        in_specs=[pl.BlockSpec((tm, k), lambda i: (i, 0)),
                  pl.BlockSpec((tm, d), lambda i: (i, 0)),
                  _mod_spec(mod, tpb),
                  resident((k, d), lambda i: (0, 0)),
                  row(), row(),
                  resident((None, d, fp), lambda i: (layer, 0, 0)),
                  resident((None, d, fp), lambda i: (layer, 0, 0)),
                  resident((None, fp, d), lambda i: (layer, 0, 0)),
                  row(), row()],
        out_specs=pl.BlockSpec((tm, d), lambda i: (i, 0)),
        out_shape=jax.ShapeDtypeStruct((rows, d), F32),
        scratch_shapes=[pltpu.VMEM((tm, d), BF16), pltpu.VMEM((tm, d), F32)],
        name="mixer_out_ffn",
        compiler_params=_cparams("parallel"),
    )(m2d, x2d, mod, w_o, ln_g[1].reshape(1, d), ln_b[1].reshape(1, d), w_a, w_u, w_down,
      ln_g[2].reshape(1, d), ln_b[2].reshape(1, d))


def _ffn_block(x2d, mod, tiling, j, weights, layer, ln_g, ln_b):
    tm, tpb, _ = tiling
    rows, d = x2d.shape
    w_a, w_u, w_down = weights
    fp = w_a.shape[-1]
    kern = functools.partial(_ffn_kernel, j=j, n_chunks=fp // FF_CHUNK)
    return pl.pallas_call(
        kern,
        grid=(rows // tm,),
        in_specs=[pl.BlockSpec((tm, d), lambda i: (i, 0)),
                  _mod_spec(mod, tpb),
                  pl.BlockSpec((None, d, fp), lambda i: (layer, 0, 0), pipeline_mode=pl.Buffered(1)),
                  pl.BlockSpec((None, d, fp), lambda i: (layer, 0, 0), pipeline_mode=pl.Buffered(1)),
                  pl.BlockSpec((None, fp, d), lambda i: (layer, 0, 0), pipeline_mode=pl.Buffered(1)),
                  pl.BlockSpec((1, d), lambda i: (0, 0)),
                  pl.BlockSpec((1, d), lambda i: (0, 0))],
        out_specs=pl.BlockSpec((tm, d), lambda i: (i, 0)),
        out_shape=jax.ShapeDtypeStruct((rows, d), F32),
        scratch_shapes=[pltpu.VMEM((tm, d), BF16)],
        name="ffn_block",
        compiler_params=_cparams("parallel"),
    )(x2d, mod, w_a, w_u, w_down, ln_g.reshape(1, d), ln_b.reshape(1, d))


def _prep_ffn_weights(w_up, w_down):
    d_ff = w_up.shape[2] // 2
    fp = _round_up(d_ff, FF_CHUNK)
    pad_cols = lambda w: jnp.pad(w.astype(BF16), ((0, 0), (0, 0), (0, fp - d_ff)))
    down = jnp.pad(w_down.astype(BF16), ((0, 0), (0, fp - d_ff), (0, 0)))
    return pad_cols(w_up[:, :, :d_ff]), pad_cols(w_up[:, :, d_ff:]), down


PROJ_ROW_TILE = 512


def _proj_in_kernel(x_ref, mod_ref, w_ref, *o_refs, j, tn):
    h = (x_ref[...] * (1.0 + mod_ref[3 * j + 1]) + mod_ref[3 * j]).astype(BF16)
    per_out = o_refs[0].shape[1] // tn
    for k in range(w_ref.shape[1] // tn):
        cols = slice((k % per_out) * tn, (k % per_out + 1) * tn)
        o_refs[k // per_out][:, cols] = jnp.dot(h, w_ref[:, k * tn:(k + 1) * tn], preferred_element_type=F32)


def _proj_in(x2d, mod, tiling, j, w, tn, n_out=1):
    tm, tpb, per_batch = tiling
    if per_batch and tm % PROJ_ROW_TILE == 0:
        tpb = tpb * (tm // PROJ_ROW_TILE)
        tm = PROJ_ROW_TILE
    rows, d = x2d.shape
    n = w.shape[1]
    assert n % (n_out * tn) == 0
    outs = pl.pallas_call(
        functools.partial(_proj_in_kernel, j=j, tn=tn),
        grid=(rows // tm,),
        in_specs=[pl.BlockSpec((tm, d), lambda i: (i, 0)),
                  _mod_spec(mod, tpb),
                  pl.BlockSpec((d, n), lambda i: (0, 0), pipeline_mode=pl.Buffered(1))],
        out_specs=[pl.BlockSpec((tm, n // n_out), lambda i: (i, 0))] * n_out,
        out_shape=[jax.ShapeDtypeStruct((rows, n // n_out), F32)] * n_out,
        name="proj_in",
        compiler_params=_cparams("parallel"),
    )(x2d, mod, w)
    return outs[0] if n_out == 1 else outs


def _proj_out_kernel(o_ref, x_ref, mod_ref, w_ref, g_ref, b_ref, y_ref, *, j):
    y = jnp.dot(o_ref[...].astype(BF16), w_ref[...], preferred_element_type=F32)
    z = DN_ALPHA * x_ref[...] + (1.0 + mod_ref[3 * j + 2]) * y
    y_ref[...] = _layer_norm_rows(z, g_ref[...], b_ref[...])


def _proj_out(o2d, x2d, mod, tiling, j, w, ln_g, ln_b):
    tm, tpb, _ = tiling
    rows, d = x2d.shape
    k = o2d.shape[1]
    return pl.pallas_call(
        functools.partial(_proj_out_kernel, j=j),
        grid=(rows // tm,),
        in_specs=[pl.BlockSpec((tm, k), lambda i: (i, 0)),
                  pl.BlockSpec((tm, d), lambda i: (i, 0)),
                  _mod_spec(mod, tpb),
                  pl.BlockSpec((k, d), lambda i: (0, 0)),
                  pl.BlockSpec((1, d), lambda i: (0, 0)),
                  pl.BlockSpec((1, d), lambda i: (0, 0))],
        out_specs=pl.BlockSpec((tm, d), lambda i: (i, 0)),
        out_shape=jax.ShapeDtypeStruct((rows, d), F32),
        name="proj_out",
        compiler_params=_cparams("parallel"),
    )(o2d, x2d, mod, w, ln_g.reshape(1, d), ln_b.reshape(1, d))


def _tri(c, strict=False):
    r = lax.broadcasted_iota(jnp.int32, (c, c), 0)
    s = lax.broadcasted_iota(jnp.int32, (c, c), 1)
    return (r > s) if strict else (r >= s)


def _hgrn_diag_masks(c):
    row = lax.broadcasted_iota(jnp.int32, (c, c), 0)
    col = lax.broadcasted_iota(jnp.int32, (c, c), 1)
    return [(col == row - d) & ((row % SUBLANES) >= d) for d in range(SUBLANES)]


def _hgrn_chunk(q, k, cum, v, st_prev, c, diag_masks):
    hs = range(len(q))
    nb = c // SUBLANES
    last_row = [cum[h][c - 1:c, :] for h in hs]
    o_state = [_bdot_nt(q[h] * jnp.exp(cum[h]), st_prev[h]) for h in hs]
    q3 = [q[h].reshape(nb, SUBLANES, LANES) for h in hs]
    k3 = [k[h].reshape(nb, SUBLANES, LANES) for h in hs]
    cum3 = [cum[h].reshape(nb, SUBLANES, LANES) for h in hs]

    att_off = None
    if nb > 1:
        blk = lax.broadcasted_iota(jnp.int32, (nb, SUBLANES, LANES), 0)
        q_cat, k_cat = [], []
        for h in hs:
            e = cum3[h][:, SUBLANES - 1:SUBLANES, :]
            k_t = k3[h] * jnp.exp(e - cum3[h])
            q_parts, k_parts = [], []
            for jb in range(nb - 1):
                qj = q3[h][jb + 1:] * jnp.exp(cum3[h][jb + 1:] - e[jb:jb + 1])
                qj = jnp.concatenate([jnp.zeros((jb + 1, SUBLANES, LANES), F32), qj], axis=0)
                q_parts.append(qj.reshape(c, LANES).astype(BF16))
                k_parts.append(jnp.where(blk == jb, k_t, 0.0).reshape(c, LANES).astype(BF16))
            q_cat.append(jnp.concatenate(q_parts, axis=1))
            k_cat.append(jnp.concatenate(k_parts, axis=1))
        att_off = [_bdot_nt(q_cat[h], k_cat[h]) for h in hs]

    att = [jnp.zeros((c, c), F32) for _ in hs]
    for d in reversed(range(SUBLANES)):
        for h in hs:
            if d == 0:
                p = q3[h] * k3[h]
            else:
                p = q3[h] * pltpu.roll(k3[h], d, 1) * jnp.exp(cum3[h] - pltpu.roll(cum3[h], d, 1))
            a_d = jnp.sum(p.reshape(c, LANES), axis=-1, keepdims=True)
            att[h] = jnp.where(diag_masks[d], a_d, att[h])
    if att_off is not None:
        att = [att[h] + att_off[h] for h in hs]

    o = [o_state[h] + _bdot(att[h], v[h]) for h in hs]
    st_new = [st_prev[h] * jnp.exp(last_row[h]) + _bdot_tn(v[h], k[h] * jnp.exp(last_row[h] - cum[h])) for h in hs]
    return o, st_new


REC_L_TILE = 512
HGRN_HEADS_PER_STEP = 8
GDN_HEADS_PER_STEP = 8
GDN_STACK = 4


def _head_lanes(hh):
    return slice(hh * HEAD_DIM, (hh + 1) * HEAD_DIM)


def _rec_tiling(l):
    c = min(CHUNK, l)
    tl = min(REC_L_TILE, l)
    assert l % tl == 0 and tl % c == 0 and c % SUBLANES == 0
    return c, tl


def _hgrn_kernel(q_ref, f_ref, v_ref, g_ref, lb_ref, ng_ref, s0_ref, o_ref, s_ref, *, c, n_chunks, hb):
    @pl.when(pl.program_id(2) == 0)
    def _():
        for hh in range(hb):
            s_ref[hh] = s0_ref[hh].T

    ng = ng_ref[...]
    lb = lb_ref[...]
    tril3 = _block_tril3(c, c)
    diag_masks = _hgrn_diag_masks(c)

    def body(i, carry):
        rows = pl.ds(pl.multiple_of(i * c, c), c)
        fr = f_ref[rows, :]
        q_all = _silu(q_ref[rows, :]) * (HEAD_DIM ** -0.5)
        sg = _sigmoid(fr)
        k_all = (1.0 - lb) * (1.0 - sg)
        cum_all = _prefix_dot(tril3, jnp.log(lb + (1.0 - lb) * sg))
        lanes = [_head_lanes(hh) for hh in range(hb)]
        o, st_new = _hgrn_chunk([q_all[:, ln] for ln in lanes], [k_all[:, ln] for ln in lanes],
                                [cum_all[:, ln] for ln in lanes], [v_ref[rows, ln] for ln in lanes],
                                [s_ref[hh] for hh in range(hb)], c, diag_masks)
        for hh, ln in enumerate(lanes):
            s_ref[hh] = st_new[hh]
            ms = jnp.mean(o[hh] * o[hh], axis=-1, keepdims=True)
            o_ref[rows, ln] = o[hh] * lax.rsqrt(ms + 1e-6) * ng * _sigmoid(g_ref[rows, ln])
        return carry

    lax.fori_loop(0, n_chunks, body, 0)

    @pl.when(pl.program_id(2) == pl.num_programs(2) - 1)
    def _():
        for hh in range(hb):
            s_ref[hh] = s_ref[hh].T


def _hgrn_recurrence(proj, s0, layer, lb, norm_g):
    b, l, _ = proj.shape
    c, tl = _rec_tiling(l)
    hb = HGRN_HEADS_PER_STEP
    ng_ = HEADS // hb
    col = lambda off: pl.BlockSpec((None, tl, hb * HEAD_DIM), lambda bi, hg, t: (bi, t, off * ng_ + hg))
    state = pl.BlockSpec((None, hb, HEAD_DIM, HEAD_DIM), lambda bi, hg, t: (bi, hg, 0, 0))
    return pl.pallas_call(
        functools.partial(_hgrn_kernel, c=c, n_chunks=tl // c, hb=hb),
        grid=(b, ng_, l // tl),
        in_specs=[col(0), col(1), col(2), col(3),
                  pl.BlockSpec((1, hb * HEAD_DIM), lambda bi, hg, t: (0, hg)),
                  pl.BlockSpec((1, HEAD_DIM), lambda bi, hg, t: (0, 0)),
                  pl.BlockSpec((None, None, hb, HEAD_DIM, HEAD_DIM), lambda bi, hg, t: (layer, bi, hg, 0, 0))],
        out_specs=[col(0), state],
        out_shape=[jax.ShapeDtypeStruct((b, l, HEADS * HEAD_DIM), F32),
                   jax.ShapeDtypeStruct((b, HEADS, HEAD_DIM, HEAD_DIM), F32)],
        name="hgrn_recurrence",
        compiler_params=_cparams("parallel", "parallel", "arbitrary"),
    )(proj, proj, proj, proj, lb.reshape(1, -1), norm_g.reshape(1, -1), s0)


HIST = SUBLANES


def _gdn_conv(x_ref, r0, lanes, cw, c):
    x_ext = x_ref[pl.ds(r0, c + HIST), lanes]
    acc = x_ext * cw[CONV_W - 1:CONV_W, :]
    for w in range(CONV_W - 1):
        acc = acc + pltpu.roll(x_ext, CONV_W - 1 - w, 0) * cw[w:w + 1, :]
    return _silu(acc[HIST:, :])


def _l2norm_rows(x):
    return x * lax.rsqrt(jnp.sum(x * x, axis=-1, keepdims=True) + 1e-6)


def _gdn_masks(r, c):
    row = lax.broadcasted_iota(jnp.int32, (r, r), 0)
    col = lax.broadcasted_iota(jnp.int32, (r, r), 1)
    same = (row // c) == (col // c)
    low = same & (row >= col)
    return low, same & (row > col), jnp.where(row == col, 1.0, 0.0), _block_tril3(r, c)


def _gdn_wy(groups, c, masks):
    low, strict, eye, tril3 = masks
    gr = range(len(groups))
    qs, ks, vs, betas, gs = zip(*groups)
    r = qs[0].shape[0]
    n = r // c
    pre = [_prefix_dot(tril3, jnp.concatenate(
        [jnp.where(strict, jnp.broadcast_to(gs[g], (r, r)), 0.0), jnp.broadcast_to(gs[g], (r, LANES))], axis=1))
        for g in gr]
    cum = [pre[g][:, r:] for g in gr]
    decay = [jnp.where(low, jnp.exp(jnp.where(low, pre[g][:, :r], 0.0)), 0.0) for g in gr]
    ecum = [jnp.exp(cum[g]) for g in gr]
    kb = [ks[g] * betas[g] for g in gr]
    kk = [_bdot_nt(jnp.concatenate([kb[g], qs[g]], axis=0), ks[g]) for g in gr]
    p = [-jnp.where(strict, kk[g][:r] * decay[g], 0.0) for g in gr]
    att = [kk[g][r:] * decay[g] for g in gr]
    t_inv = [eye + p[g] for g in gr]
    x = p
    for _ in range(int(math.log2(c)) - 1):
        x = [_bdot(x[g], x[g]) for g in gr]
        t_inv = [t_inv[g] + _bdot(x[g], t_inv[g]) for g in gr]
    uw = [_bdot(t_inv[g], jnp.concatenate([vs[g] * betas[g], kb[g] * ecum[g]], axis=1)) for g in gr]
    qe = [qs[g] * ecum[g] for g in gr]
    last = [jnp.concatenate([jnp.broadcast_to(cum[g][(h + 1) * c - 1:(h + 1) * c, :], (c, LANES))
                             for h in range(n)], axis=0) for g in gr]
    kt = [ks[g] * jnp.exp(last[g] - cum[g]) for g in gr]
    return [(uw[g], att[g], qe[g], kt[g], jnp.exp(last[g])) for g in gr]


def _gdn_state_step(wy, s_prev, c):
    gr = range(len(wy))
    uw, att, qe, kt, ecl = zip(*wy)
    n = len(s_prev[0])
    head_rows = [slice(h * c, (h + 1) * c) for h in range(n)]
    ws = [[_bdot(jnp.concatenate([uw[g][rows, HEAD_DIM:], qe[g][rows]], axis=0), s_prev[g][h])
           for h, rows in enumerate(head_rows)] for g in gr]
    v_new = [jnp.concatenate([uw[g][rows, :HEAD_DIM] - ws[g][h][:c] for h, rows in enumerate(head_rows)], axis=0)
             for g in gr]
    o = [jnp.concatenate([ws[g][h][c:] for h in range(n)], axis=0) + _bdot(att[g], v_new[g]) for g in gr]
    s_new = [[s_prev[g][h] * ecl[g][h * c:h * c + 1, :] + _bdot_tn(kt[g][rows], v_new[g][rows])
              for h, rows in enumerate(head_rows)] for g in gr]
    return [(o[g], s_new[g]) for g in gr]


def _gdn_kernel(q_ref, k_ref, v_ref, z_ref, ba_ref, c0q_ref, c0k_ref, c0v_ref, cwq_ref, cwk_ref, cwv_ref,
                an_ref, dt_ref, ng_ref, s0_ref, o_ref, s_ref, qx_ref, kx_ref, vx_ref, *wy_refs,
                c, n_chunks, tl, hb):
    hg = pl.program_id(1)
    t = pl.program_id(2)
    staged = ((q_ref, c0q_ref, qx_ref), (k_ref, c0k_ref, kx_ref), (v_ref, c0v_ref, vx_ref))

    @pl.when(t == 0)
    def _():
        s_ref[...] = s0_ref[...]
        for _, c0, dst in staged:
            dst[0:HIST, :] = jnp.zeros((HIST, hb * HEAD_DIM), F32)
            dst[HIST - (CONV_W - 1):HIST, :] = c0[...]

    @pl.when(t > 0)
    def _():
        for _, _, dst in staged:
            dst[0:HIST, :] = dst[tl:tl + HIST, :]

    for src, _, dst in staged:
        dst[HIST:HIST + tl, :] = src[...]
    ng = ng_ref[...]
    lane = lax.broadcasted_iota(jnp.int32, (c, LANES), 1)
    masks = _gdn_masks(GDN_STACK * c, c)
    n_groups = hb // GDN_STACK
    stack = lambda parts: jnp.concatenate(parts, axis=0)

    def chunk_groups(i):
        r0 = pl.multiple_of(i * c, c)
        ba = ba_ref[pl.ds(r0, c), :]
        groups = []
        for h0 in range(0, hb, GDN_STACK):
            qs, ks, vs, betas, gs = [], [], [], [], []
            for hh in range(h0, h0 + GDN_STACK):
                ln = _head_lanes(hh)
                head = hg * hb + hh
                qs.append(_l2norm_rows(_gdn_conv(qx_ref, r0, ln, cwq_ref[:, ln], c)) * (HEAD_DIM ** -0.5))
                ks.append(_l2norm_rows(_gdn_conv(kx_ref, r0, ln, cwk_ref[:, ln], c)))
                vs.append(_gdn_conv(vx_ref, r0, ln, cwv_ref[:, ln], c))
                betas.append(_sigmoid(jnp.sum(jnp.where(lane == head, ba, 0.0), axis=-1, keepdims=True)))
                a_raw = jnp.sum(jnp.where(lane == HEADS + head, ba, 0.0), axis=-1, keepdims=True)
                xa = a_raw + dt_ref[:, ln][:, 0:1]
                softplus = jnp.maximum(xa, 0.0) + jnp.log(1.0 + jnp.exp(-jnp.abs(xa)))
                gs.append(an_ref[:, ln][:, 0:1] * softplus)
            groups.append((stack(qs), stack(ks), stack(vs), stack(betas), stack(gs)))
        return groups

    per_iter = 2 if n_chunks % 2 == 0 else 1

    def wy_body(ip, carry):
        chunks = [ip * per_iter + j for j in range(per_iter)]
        wy = _gdn_wy([g for i in chunks for g in chunk_groups(i)], c, masks)
        for j, i in enumerate(chunks):
            for gi in range(n_groups):
                for ref, val in zip(wy_refs, wy[j * n_groups + gi]):
                    ref[i, gi] = val
        return carry

    lax.fori_loop(0, n_chunks // per_iter, wy_body, 0)

    def state_body(i, carry):
        rows = pl.ds(pl.multiple_of(i * c, c), c)
        wy = [tuple(ref[i, gi] for ref in wy_refs) for gi in range(n_groups)]
        s_prev = [[s_ref[gi * GDN_STACK + j] for j in range(GDN_STACK)] for gi in range(n_groups)]
        for gi, (o, s_new) in enumerate(_gdn_state_step(wy, s_prev, c)):
            for j in range(GDN_STACK):
                hh = gi * GDN_STACK + j
                ln = _head_lanes(hh)
                s_ref[hh] = s_new[j]
                oh = o[j * c:(j + 1) * c]
                ms = jnp.mean(oh * oh, axis=-1, keepdims=True)
                o_ref[rows, ln] = oh * lax.rsqrt(ms + 1e-6) * ng * _silu(z_ref[rows, ln])
        return carry

    lax.fori_loop(0, n_chunks, state_body, 0)


def _gdn_recurrence(proj, conv0, s0, layer, conv_w, a_log, dt_bias, norm_g):
    b, l, _ = proj.shape
    c, tl = _rec_tiling(l)
    hb = GDN_HEADS_PER_STEP
    ng_ = HEADS // hb
    w = hb * HEAD_DIM
    col = lambda off: pl.BlockSpec((None, tl, w), lambda bi, hg, t: (bi, t, off * ng_ + hg))
    c0 = lambda off: pl.BlockSpec((None, None, CONV_W - 1, w), lambda bi, hg, t: (layer, bi, 0, off * ng_ + hg))
    cw = lambda off: pl.BlockSpec((CONV_W, w), lambda bi, hg, t: (0, off * ng_ + hg))
    per_head = pl.BlockSpec((1, w), lambda bi, hg, t: (0, hg))
    state = pl.BlockSpec((None, hb, HEAD_DIM, HEAD_DIM), lambda bi, hg, t: (bi, hg, 0, 0))
    a_neg = jnp.repeat(-jnp.exp(a_log.astype(F32)), HEAD_DIM)[None, :]
    dt_b = jnp.repeat(dt_bias.astype(F32), HEAD_DIM)[None, :]
    xbuf = pltpu.VMEM((HIST + tl, w), F32)
    return pl.pallas_call(
        functools.partial(_gdn_kernel, c=c, n_chunks=tl // c, tl=tl, hb=hb),
        grid=(b, ng_, l // tl),
        in_specs=[col(0), col(1), col(2), col(3),
                  pl.BlockSpec((None, tl, LANES), lambda bi, hg, t: (bi, t, 4 * HEADS)),
                  c0(0), c0(1), c0(2), cw(0), cw(1), cw(2),
                  per_head, per_head,
                  pl.BlockSpec((1, HEAD_DIM), lambda bi, hg, t: (0, 0)),
                  pl.BlockSpec((None, None, hb, HEAD_DIM, HEAD_DIM), lambda bi, hg, t: (layer, bi, hg, 0, 0))],
        out_specs=[col(0), state],
        out_shape=[jax.ShapeDtypeStruct((b, l, HEADS * HEAD_DIM), F32),
                   jax.ShapeDtypeStruct((b, HEADS, HEAD_DIM, HEAD_DIM), F32)],
        scratch_shapes=[xbuf, xbuf, xbuf] + [
            pltpu.VMEM((tl // c, hb // GDN_STACK, GDN_STACK * c, width), F32)
            for width in (2 * HEAD_DIM, GDN_STACK * c, HEAD_DIM, HEAD_DIM, HEAD_DIM)],
        name="gdn_recurrence",
        compiler_params=_cparams("parallel", "parallel", "arbitrary"),
    )(proj, proj, proj, proj, proj, conv0, conv0, conv0, conv_w, conv_w, conv_w,
      a_neg, dt_b, norm_g.reshape(1, -1), s0)


def _rel_bucket(dist):
    exact = REL_BUCKETS // 2
    d = jnp.maximum(dist, exact).astype(F32)
    large = exact + (jnp.log(d / exact) / math.log(REL_MAX_DIST / exact) * (REL_BUCKETS - exact)).astype(jnp.int32)
    return jnp.where(dist < exact, dist, jnp.minimum(large, REL_BUCKETS - 1))


def _top_mask(gate, n_valid, k, axis):
    idx = lax.broadcasted_iota(jnp.int32, gate.shape, axis)
    big = jnp.int32(2 ** 30)
    work = jnp.where(idx < n_valid, gate, NEG)
    sel = jnp.zeros(gate.shape, F32)
    for r in range(k):
        mx = jnp.max(work, axis=axis, keepdims=True)
        first = jnp.min(jnp.where(work == mx, idx, big), axis=axis, keepdims=True)
        pick = (idx == first) & (idx < n_valid)
        sel = jnp.where(pick, 1.0, sel)
        work = jnp.where(idx == first, -jnp.inf, work)
    return sel


MOBA_HEADS_PER_STEP = 4


def _moba_prompt_kernel(q_ref, k_ref, v_ref, bd_ref, bp_ref, bf_ref, o_ref, means_ref, sel_ref, *, nb, hb):
    i = pl.program_id(2)
    blk = MOBA_BLOCK

    @pl.when(i == 0)
    def _():
        means_ref[...] = jnp.zeros_like(means_ref)
        for hh in range(hb):
            for n in range(nb):
                means_ref[hh, n:n + 1, :] = jnp.mean(k_ref[n * blk:(n + 1) * blk, _head_lanes(hh)],
                                                     axis=0, keepdims=True)

    causal = ~_tri(blk, strict=True)
    r0 = pl.multiple_of(i * blk, blk)
    heads = range(hb)
    qs = [q_ref[:, _head_lanes(hh)] * (HEAD_DIM ** -0.5) for hh in heads]
    qbs = [q.astype(BF16) for q in qs]
    gates = [lax.dot_general(means_ref[hh], qs[hh], (((1,), (1,)), ((), ())),
                             preferred_element_type=F32, precision=HI) for hh in heads]
    qk = [_bdot_nt(k_ref[pl.ds(r0, blk), _head_lanes(hh)], qbs[hh]) for hh in heads]
    for hh in heads:
        sel_ref[hh] = _top_mask(gates[hh], i, MOBA_TOPK, 0)
    s0 = [jnp.where(causal, qk[hh] + bd_ref[hh], NEG) for hh in heads]
    m0 = [jnp.max(s, axis=0, keepdims=True) for s in s0]
    p0 = [jnp.exp(s0[hh] - m0[hh]) for hh in heads]
    init = [(m0[hh], jnp.sum(p0[hh], axis=0, keepdims=True),
             _bdot_tn(v_ref[pl.ds(r0, blk), _head_lanes(hh)], p0[hh])) for hh in heads]

    def past_blocks(n, count, carry, bias_of_head, limit):
        rn = pl.multiple_of(n * blk, blk)
        rows = pl.ds(rn, count * blk)
        heads = range(hb)
        qk = [_bdot_nt(k_ref[rows, _head_lanes(hh)], qbs[hh]) for hh in heads]
        sn = []
        for hh in heads:
            parts = []
            for j in range(count):
                picked = sel_ref[hh, pl.ds(n + j, 1), :] * (n + j < limit).astype(F32) > 0.0
                parts.append(jnp.where(picked, qk[hh][j * blk:(j + 1) * blk] + bias_of_head(hh), NEG))
            sn.append(parts[0] if count == 1 else jnp.concatenate(parts, axis=0))
        m_new = [jnp.maximum(carry[hh][0], jnp.max(sn[hh], axis=0, keepdims=True)) for hh in heads]
        pn = [jnp.exp(sn[hh] - m_new[hh]) for hh in heads]
        pv = [_bdot_tn(v_ref[rows, _head_lanes(hh)], pn[hh]) for hh in heads]
        out = []
        for hh in heads:
            m, l, acc = carry[hh]
            alpha = jnp.exp(m - m_new[hh])
            out.append((m_new[hh], l * alpha + jnp.sum(pn[hh], axis=0, keepdims=True), acc * alpha + pv[hh]))
        return tuple(out)

    n_far = jnp.maximum(i - 1, 0)
    carry = lax.fori_loop(0, (n_far + 1) // 2,
                          lambda n2, cr: past_blocks(2 * n2, 2, cr, lambda hh: bf_ref[:, _head_lanes(hh)][:, 0:1],
                                                     n_far),
                          tuple(init))
    final = past_blocks(n_far, 1, carry, lambda hh: bp_ref[hh], i)
    for hh in range(hb):
        _, l, acc = final[hh]
        o_ref[:, _head_lanes(hh)] = (acc / l).T


def _bias_of_distance(rel_bias, dist):
    bucket = _rel_bucket(jnp.maximum(dist, 0))[None]
    out = jnp.zeros((HEADS,) + dist.shape, F32)
    for b in range(REL_BUCKETS):
        out = jnp.where(bucket == b, rel_bias[b].astype(F32).reshape((HEADS,) + (1,) * dist.ndim), out)
    return out


def _moba_bias_tables(rel_bias, blk):
    t = jnp.arange(blk, dtype=jnp.int32)
    d0 = t[None, :] - t[:, None]
    bias_diag = _bias_of_distance(rel_bias, d0)
    bias_prev = _bias_of_distance(rel_bias, d0 + blk)
    far = jnp.repeat(rel_bias[REL_BUCKETS - 1].astype(F32), HEAD_DIM)[None, :]
    return bias_diag, bias_prev, far


def _moba_prompt(q, k, v, rel_bias):
    b, l, _ = q.shape
    blk = MOBA_BLOCK
    assert l % blk == 0 and l // blk <= LANES
    assert 2 * blk > REL_MAX_DIST
    nb = l // blk
    bias_diag, bias_prev, far = _moba_bias_tables(rel_bias, blk)
    hb = MOBA_HEADS_PER_STEP
    ng_ = HEADS // hb
    w = hb * HEAD_DIM
    tile = pl.BlockSpec((hb, blk, blk), lambda bi, hg, i: (hg, 0, 0))
    return pl.pallas_call(
        functools.partial(_moba_prompt_kernel, nb=nb, hb=hb),
        grid=(b, ng_, nb),
        in_specs=[pl.BlockSpec((None, blk, w), lambda bi, hg, i: (bi, i, hg)),
                  pl.BlockSpec((None, l, w), lambda bi, hg, i: (bi, 0, hg)),
                  pl.BlockSpec((None, l, w), lambda bi, hg, i: (bi, 0, hg)),
                  tile, tile,
                  pl.BlockSpec((1, w), lambda bi, hg, i: (0, hg))],
        out_specs=pl.BlockSpec((None, blk, w), lambda bi, hg, i: (bi, i, hg)),
        out_shape=jax.ShapeDtypeStruct((b, l, HEADS * HEAD_DIM), F32),
        scratch_shapes=[pltpu.VMEM((hb, _round_up(nb, SUBLANES), HEAD_DIM), F32),
                        pltpu.VMEM((hb, _round_up(nb, SUBLANES), blk), F32)],
        name="moba_prompt",
        compiler_params=_cparams("parallel", "parallel", "arbitrary"),
    )(q, k, v, bias_diag, bias_prev, far)


def _block_diag_rows(x, lq):
    return jnp.concatenate([x[h * lq:(h + 1) * lq, h * HEAD_DIM:(h + 1) * HEAD_DIM] for h in range(HEADS)], axis=0)


MOBA_SAMPLE_BLOCKS_PER_STEP = 8


def _moba_sample_kernel(pt_ref, *refs, nb, lq, bps):
    n_pages = bps * (MOBA_BLOCK // PAGE_SIZE)
    k_refs, v_refs = refs[:n_pages], refs[n_pages:2 * n_pages]
    (q_ref, kn_ref, vn_ref, bl_ref, bc_ref, bf_ref, o_ref,
     qbd_ref, m_ref, l_ref, acc_ref, means_ref) = refs[2 * n_pages:]
    step = pl.program_id(1)
    d_model = HEADS * HEAD_DIM
    hq = HEADS * lq

    @pl.when(step == 0)
    def _():
        q = q_ref[...] * (HEAD_DIM ** -0.5)
        rows = lax.broadcasted_iota(jnp.int32, (d_model, hq), 0) // HEAD_DIM
        cols = lax.broadcasted_iota(jnp.int32, (d_model, hq), 1)
        spread = (lax.broadcasted_iota(jnp.int32, (lq, hq), 1) % lq
                  == lax.broadcasted_iota(jnp.int32, (lq, hq), 0)).astype(F32)
        qt = lax.dot_general(q, spread, (((0,), (0,)), ((), ())),
                             preferred_element_type=F32, precision=HI)
        qbd_ref[...] = jnp.where(rows == cols // lq, qt, 0.0)
        means_ref[...] = jnp.zeros_like(means_ref)
        m_ref[...] = jnp.zeros_like(m_ref)
        l_ref[...] = jnp.zeros_like(l_ref)

    qbd = qbd_ref[...]

    def rows_by_lanes(*pages):
        return jnp.concatenate(
            [jnp.concatenate([pg[pl.ds(h, PAGE_SIZE, stride=HEADS), :] for h in range(HEADS)], axis=1)
             for pg in pages], axis=0)

    js = range(bps)
    ns = [step * bps + j for j in js]
    kblk = [rows_by_lanes(k_refs[2 * j], k_refs[2 * j + 1]) for j in js]
    vblk = [rows_by_lanes(v_refs[2 * j], v_refs[2 * j + 1]) for j in js]
    s = [_bdot(kblk[j], qbd) + jnp.where(ns[j] == nb - 1, bl_ref[...], bf_ref[...]) for j in js]
    for j in js:
        means_ref[pl.ds(ns[j], 1), :] = jnp.mean(kblk[j], axis=0, keepdims=True)
    m = [jnp.max(s[j], axis=0, keepdims=True) for j in js]
    p = [jnp.exp(s[j] - m[j]) for j in js]
    pv = [_bdot_tn(p[j], vblk[j]) for j in js]
    for j in js:
        m_ref[pl.ds(ns[j], 1), :] = m[j]
        l_ref[pl.ds(ns[j], 1), :] = jnp.sum(p[j], axis=0, keepdims=True)
        acc_ref[ns[j]] = _block_diag_rows(pv[j], lq)

    @pl.when(step == pl.num_programs(1) - 1)
    def _():
        k_new = kn_ref[...]
        v_new = vn_ref[...]
        jq = lax.broadcasted_iota(jnp.int32, (lq, hq), 1) % lq
        jk = lax.broadcasted_iota(jnp.int32, (lq, hq), 0)
        sc = jnp.where(jk <= jq, _bdot(k_new, qbd) + bc_ref[...], NEG)
        mc = jnp.max(sc, axis=0, keepdims=True)
        pc = jnp.exp(sc - mc)
        lc = jnp.sum(pc, axis=0, keepdims=True)
        oc = _block_diag_rows(_bdot_tn(pc, v_new), lq)

        gate = _fdot(means_ref[...], qbd)
        sel = _top_mask(gate, nb, MOBA_TOPK, 0)
        m_all = m_ref[...]
        m_tot = jnp.maximum(jnp.max(jnp.where(sel > 0.0, m_all, NEG), axis=0, keepdims=True), mc)
        w = jnp.where(sel > 0.0, jnp.exp(jnp.where(sel > 0.0, m_all - m_tot, 0.0)), 0.0)
        wc = jnp.exp(mc - m_tot)
        denom = jnp.sum(w * l_ref[...], axis=0, keepdims=True) + wc * lc
        nbp = w.shape[0]
        w_all = jnp.concatenate([w, wc, jnp.zeros((SUBLANES - 1, hq), F32)], axis=0) / denom
        ne = nbp + SUBLANES
        eye = (lax.broadcasted_iota(jnp.int32, (ne, ne), 0)
               == lax.broadcasted_iota(jnp.int32, (ne, ne), 1)).astype(F32)
        wt = lax.dot_general(w_all, eye, (((0,), (0,)), ((), ())),
                             preferred_element_type=F32, precision=HI)
        out = wt[:, nbp:nbp + 1] * oc
        for b_i in range(nb):
            out = out + wt[:, b_i:b_i + 1] * acc_ref[b_i]
        o_ref[...] = out


def _moba_sample(q, k, v, cache_k, cache_v, layer, page_table, rel_bias):
    b, lq, _ = q.shape
    n_pages = page_table.shape[1]
    d_model = HEADS * HEAD_DIM
    pages_per_block = MOBA_BLOCK // PAGE_SIZE
    assert pages_per_block == 2 and n_pages % pages_per_block == 0 and lq <= MOBA_BLOCK
    assert cache_k.shape[2:] == (PAGE_SIZE, HEADS, HEAD_DIM)
    nb = n_pages // pages_per_block
    assert nb >= 1 and MOBA_BLOCK >= REL_MAX_DIST
    nbp = _round_up(nb, SUBLANES)
    hq = HEADS * lq
    jq = jnp.arange(lq, dtype=jnp.int32)
    t = jnp.arange(MOBA_BLOCK, dtype=jnp.int32)
    by_cols = lambda tab: jnp.transpose(tab, (1, 0, 2)).reshape(tab.shape[1], hq)
    bias_last = by_cols(_bias_of_distance(rel_bias, (MOBA_BLOCK + jq)[None, :] - t[:, None]))
    bias_cur = by_cols(_bias_of_distance(rel_bias, jq[None, :] - jq[:, None]))
    bias_far = jnp.repeat(rel_bias[REL_BUCKETS - 1].astype(F32), lq)[None, :]

    n_c, n_pool = cache_k.shape[:2]
    cache_k = cache_k.reshape(n_c, n_pool, PAGE_SIZE * HEADS, HEAD_DIM)
    cache_v = cache_v.reshape(n_c, n_pool, PAGE_SIZE * HEADS, HEAD_DIM)
    bps = math.gcd(MOBA_SAMPLE_BLOCKS_PER_STEP, nb)
    pages_per_step = bps * pages_per_block
    page = lambda off: pl.BlockSpec((None, None, PAGE_SIZE * HEADS, HEAD_DIM),
                                    lambda bi, n, pt: (layer, pt[bi, pages_per_step * n + off], 0, 0))
    pages = [page(off) for off in range(pages_per_step)]
    full = lambda shape: pl.BlockSpec(shape, lambda bi, n, pt: (0,) * len(shape))
    grid_spec = pltpu.PrefetchScalarGridSpec(
        num_scalar_prefetch=1,
        grid=(b, nb // bps),
        in_specs=pages + pages + [
                  *([pl.BlockSpec((None, lq, d_model), lambda bi, n, pt: (bi, 0, 0))] * 3),
                  full((MOBA_BLOCK, hq)), full((lq, hq)), full((1, hq))],
        out_specs=pl.BlockSpec((None, hq, HEAD_DIM), lambda bi, n, pt: (bi, 0, 0)),
        scratch_shapes=[pltpu.VMEM((d_model, hq), F32),
                        pltpu.VMEM((nbp, hq), F32),
                        pltpu.VMEM((nbp, hq), F32),
                        pltpu.VMEM((nb, hq, HEAD_DIM), F32),
                        pltpu.VMEM((nbp, d_model), F32)])
    return pl.pallas_call(
        functools.partial(_moba_sample_kernel, nb=nb, lq=lq, bps=bps),
        grid_spec=grid_spec,
        out_shape=jax.ShapeDtypeStruct((b, hq, HEAD_DIM), F32),
        name="moba_sample",
        compiler_params=_cparams("parallel", "arbitrary"),
    )(page_table, *([cache_k] * pages_per_step), *([cache_v] * pages_per_step), q, k, v,
      bias_last, bias_cur, bias_far)


def _trunk(x, mods, s_hgrn, s_conv, s_gdn, paged, W):
    b, l, d = x.shape
    tiling = _row_tile(b, l)
    x2 = x.reshape(b * l, d)
    new_h, new_c, new_g, new_k, new_v = [], [], [], [], []
    for i in range(DEPTH):
        mod = _mod_layout(mods[i], b, l)
        x2 = _ffn_block(x2, mod, tiling, 0, W['ffn1'], i, W['ln_g'][i, 0], W['ln_b'][i, 0])
        kind, li = i % N_MIXERS, i // N_MIXERS
        if kind == 0:
            proj = _proj_in(x2, mod, tiling, 1, W['hgrn_w_in'][li], 1024)
            o, s = _hgrn_recurrence(proj.reshape(b, l, -1), s_hgrn, li, W['hgrn_lb'][li], W['hgrn_norm_g'][li])
            new_h.append(s)
            w_o = W['hgrn_w_o'][li]
            o2 = o.reshape(b * l, d)
        elif kind == 1:
            proj = _proj_in(x2, mod, tiling, 1, W['gdn_w_in'][li], W['gdn_w_in'][li].shape[1] // 3)
            proj = proj.reshape(b, l, -1)
            o, s = _gdn_recurrence(proj, s_conv, s_gdn, li, W['gdn_conv_w'][li], W['gdn_a_log'][li],
                                   W['gdn_dt_bias'][li], W['gdn_norm_g'][li])
            n_qkv = 3 * HEADS * HEAD_DIM
            assert l >= CONV_W - 1
            new_c.append(proj[:, l - (CONV_W - 1):, :n_qkv])
            new_g.append(s)
            w_o = W['gdn_w_o'][li]
            o2 = o.reshape(b * l, d)
        else:
            q, k, v = (t.reshape(b, l, d) for t in _proj_in(x2, mod, tiling, 1, W['moba_w_qkv'][li], 1024, n_out=3))
            if paged is None:
                o2 = _moba_prompt(q, k, v, W['rel_bias']).reshape(b * l, d)
            else:
                ck, cv, pt = paged
                o = _moba_sample(q, k, v, ck, cv, li, pt, W['rel_bias'])
                o2 = jnp.transpose(o.reshape(b, HEADS, l, HEAD_DIM), (0, 2, 1, 3)).reshape(b * l, d)
            new_k.append(k.reshape(b, l, HEADS, HEAD_DIM))
            new_v.append(v.reshape(b, l, HEADS, HEAD_DIM))
            w_o = W['moba_w_o'][li]
        x2 = _mixer_out_ffn(o2, x2, mod, tiling, w_o, W['ffn2'], i, W['ln_g'][i], W['ln_b'][i])
    return (x2.reshape(b, l, d), jnp.stack(new_h), jnp.stack(new_c), jnp.stack(new_g),
            jnp.stack(new_k), jnp.stack(new_v))


def kernel(x_prompt, x_sample, state_hgrn, state_gdn_conv, state_gdn, cache_k, cache_v, page_table, c_prompt, c_sample, w_ada, b_ada, ln_g, ln_b, w_ffn1_up, w_ffn1_down, w_ffn2_up, w_ffn2_down, hgrn_w_in, hgrn_lb_logits, hgrn_norm_g, hgrn_w_o, gdn_w_in, gdn_conv_w, gdn_a_log, gdn_dt_bias, gdn_norm_g, gdn_w_o, moba_w_qkv, moba_w_o, rel_bias):
    bp, _, d = x_prompt.shape
    bs = x_sample.shape[0]
    assert d == HEADS * HEAD_DIM

    nc = _round_up(bp + bs, SUBLANES)
    c_all = jnp.pad(jnp.concatenate([c_prompt, c_sample], axis=0), ((0, nc - bp - bs), (0, 0)))
    mods = _ada_mods(c_all, w_ada, b_ada).reshape(DEPTH, nc, 9, d)

    gdn_n = _round_up(gdn_w_in.shape[2], 3 * LANES)
    lbp = jax.nn.softmax(hgrn_lb_logits.astype(F32), axis=0)
    W = {
        'ffn1': _prep_ffn_weights(w_ffn1_up, w_ffn1_down), 'ffn2': _prep_ffn_weights(w_ffn2_up, w_ffn2_down),
        'ln_g': ln_g, 'ln_b': ln_b,
        'hgrn_w_in': hgrn_w_in.astype(BF16), 'hgrn_lb': jnp.cumsum(lbp, axis=0) - lbp[0],
        'hgrn_norm_g': hgrn_norm_g, 'hgrn_w_o': hgrn_w_o.astype(BF16),
        'gdn_w_in': jnp.pad(gdn_w_in, ((0, 0), (0, 0), (0, gdn_n - gdn_w_in.shape[2]))).astype(BF16),
        'gdn_conv_w': gdn_conv_w, 'gdn_a_log': gdn_a_log, 'gdn_dt_bias': gdn_dt_bias,
        'gdn_norm_g': gdn_norm_g, 'gdn_w_o': gdn_w_o.astype(BF16),
        'moba_w_qkv': moba_w_qkv.astype(BF16), 'moba_w_o': moba_w_o.astype(BF16), 'rel_bias': rel_bias,
    }

    n_a, n_b = state_hgrn.shape[0], state_gdn.shape[0]
    z_hgrn = jnp.zeros((n_a, bp) + state_hgrn.shape[2:], F32)
    z_conv = jnp.zeros((n_b, bp) + state_gdn_conv.shape[2:], F32)
    z_gdn = jnp.zeros((n_b, bp) + state_gdn.shape[2:], F32)
    y_p, hgrn_p, conv_p, gdn_p, k_p, v_p = _trunk(x_prompt, mods[:, :bp], z_hgrn, z_conv, z_gdn, None, W)

    y_s, hgrn_s, conv_s, gdn_s, k_s, v_s = _trunk(x_sample, mods[:, bp:bp + bs], state_hgrn, state_gdn_conv,
                                                  state_gdn, (cache_k, cache_v, page_table), W)
    return (y_p, y_s, hgrn_p, hgrn_s, conv_p, conv_s, gdn_p, gdn_s, k_p, k_s, v_p, v_s)
```

```python
import functools
import math

import jax
import jax.numpy as jnp
from jax import lax
from jax.experimental import pallas as pl
from jax.experimental.pallas import tpu as pltpu

F32 = jnp.float32
BF16 = jnp.bfloat16
HI = lax.Precision.HIGHEST

DEPTH = 4
N_MIXERS = 3
HEADS = 8
HEAD_DIM = 128
CONV_W = 4
CHUNK = 64
MOBA_BLOCK = 256
MOBA_TOPK = 3
REL_BUCKETS = 32
REL_MAX_DIST = 128
PAGE_SIZE = 128
DN_ALPHA = (2 * DEPTH) ** 0.25
LN_EPS = 1e-5
NEG = -1e30

LANES = 128
SUBLANES = 8
MXU_WIDTH = 256
VMEM_LIMIT_BYTES = 56 * 1024 * 1024


def _cparams(*sem):
    return pltpu.CompilerParams(dimension_semantics=sem, vmem_limit_bytes=VMEM_LIMIT_BYTES)


def _round_up(n, m):
    return (n + m - 1) // m * m


def _bdot(a, b):
    return jnp.dot(a.astype(BF16), b.astype(BF16), preferred_element_type=F32)


def _bdot_nt(a, b):
    return lax.dot_general(a.astype(BF16), b.astype(BF16), (((1,), (1,)), ((), ())),
                           preferred_element_type=F32)


def _bdot_tn(a, b):
    return lax.dot_general(a.astype(BF16), b.astype(BF16), (((0,), (0,)), ((), ())),
                           preferred_element_type=F32)


def _fdot(a, b):
    return jnp.dot(a, b, preferred_element_type=F32, precision=HI)


def _block_tril3(r, c):
    row = lax.broadcasted_iota(jnp.int32, (r, 3 * r), 0)
    col = lax.broadcasted_iota(jnp.int32, (r, 3 * r), 1) % r
    return jnp.where(((row // c) == (col // c)) & (row >= col), 1.0, 0.0).astype(BF16)


def _prefix_dot(sel3, x):
    hi = x.astype(BF16)
    rest = x - hi.astype(F32)
    mid = rest.astype(BF16)
    lo = (rest - mid.astype(F32)).astype(BF16)
    return jnp.dot(sel3, jnp.concatenate([hi, mid, lo], axis=0), preferred_element_type=F32)


def _sigmoid(x):
    return 1.0 / (1.0 + jnp.exp(-x))


def _silu(x):
    return x * _sigmoid(x)


def _layer_norm_rows(z, g, b):
    mu = jnp.mean(z, axis=-1, keepdims=True)
    zc = z - mu
    var = jnp.mean(zc * zc, axis=-1, keepdims=True)
    return zc * lax.rsqrt(var + LN_EPS) * g + b


def _ada_kernel(c_ref, w_ref, b_ref, o_ref):
    o_ref[...] = _bdot(_silu(c_ref[...]), w_ref[...]) + b_ref[...]


def _ada_mods(c_all, w_ada, b_ada):
    depth, d, n = w_ada.shape
    nb = c_all.shape[0]
    tn = n // 8
    assert n % 8 == 0 and tn % LANES == 0
    return pl.pallas_call(
        _ada_kernel,
        grid=(depth, n // tn),
        in_specs=[pl.BlockSpec((nb, d), lambda l, j: (0, 0)),
                  pl.BlockSpec((None, d, tn), lambda l, j: (l, 0, j)),
                  pl.BlockSpec((None, 1, tn), lambda l, j: (l, 0, j))],
        out_specs=pl.BlockSpec((None, nb, tn), lambda l, j: (l, 0, j)),
        out_shape=jax.ShapeDtypeStruct((depth, nb, n), F32),
        name="ada_mods",
        compiler_params=_cparams("parallel", "parallel"),
    )(c_all, w_ada, b_ada.reshape(depth, 1, n))


MAX_ROW_TILE = 1024


def _row_tile(batch, length):
    if length >= MXU_WIDTH:
        tm = MAX_ROW_TILE
        while length % tm:
            tm //= 2
        return tm, length // tm, True
    rows = batch * length
    assert rows % SUBLANES == 0 and rows <= MAX_ROW_TILE
    return rows, 1, False


def _mod_layout(m, batch, length):
    tm, _, per_batch = _row_tile(batch, length)
    mt = jnp.transpose(m, (1, 0, 2))
    if per_batch:
        return mt[:, :, None, :]
    return jnp.repeat(mt, length, axis=1)[:, None, :, :]


def _mod_spec(mod, tiles_per_block):
    _, _, rm, d = mod.shape
    return pl.BlockSpec((9, None, rm, d), lambda i, *_: (0, i // tiles_per_block, 0, 0))


FF_CHUNK = MXU_WIDTH


def _ffn_sublayer(x_ref, mod_ref, wa_ref, wu_ref, wd_ref, g_ref, b_ref, o_ref, h_ref, j, n_chunks):
    h_ref[...] = (x_ref[...] * (1.0 + mod_ref[3 * j + 1]) + mod_ref[3 * j]).astype(BF16)

    for c in range(n_chunks):
        cols = slice(c * FF_CHUNK, (c + 1) * FF_CHUNK)
        h = h_ref[...]
        a = jnp.dot(h, wa_ref[:, cols], preferred_element_type=F32)
        u = jnp.dot(h, wu_ref[:, cols], preferred_element_type=F32)
        act = (_silu(a) * u).astype(BF16)
        part = jnp.dot(act, wd_ref[cols, :], preferred_element_type=F32)
        if c == 0:
            o_ref[...] = part
        else:
            o_ref[...] += part
    z = DN_ALPHA * x_ref[...] + (1.0 + mod_ref[3 * j + 2]) * (0.5 * o_ref[...])
    o_ref[...] = _layer_norm_rows(z, g_ref[...], b_ref[...])


def _ffn_kernel(x_ref, mod_ref, wa_ref, wu_ref, wd_ref, g_ref, b_ref, o_ref, h_ref, *, j, n_chunks):
    _ffn_sublayer(x_ref, mod_ref, wa_ref, wu_ref, wd_ref, g_ref, b_ref, o_ref, h_ref, j, n_chunks)


def _mixer_out_ffn_kernel(m_ref, x_ref, mod_ref, wo_ref, g1_ref, b1_ref, wa_ref, wu_ref, wd_ref, g2_ref, b2_ref,
                          o_ref, h_ref, x1_ref, *, n_chunks):
    y = jnp.dot(m_ref[...].astype(BF16), wo_ref[...], preferred_element_type=F32)
    z = DN_ALPHA * x_ref[...] + (1.0 + mod_ref[3 * 1 + 2]) * y
    x1_ref[...] = _layer_norm_rows(z, g1_ref[...], b1_ref[...])
    _ffn_sublayer(x1_ref, mod_ref, wa_ref, wu_ref, wd_ref, g2_ref, b2_ref, o_ref, h_ref, 2, n_chunks)


def _mixer_out_ffn(m2d, x2d, mod, tiling, w_o, weights, layer, ln_g, ln_b):
    tm, tpb, _ = tiling
    rows, d = x2d.shape
    k = m2d.shape[1]
    w_a, w_u, w_down = weights
    fp = w_a.shape[-1]
    row = lambda: pl.BlockSpec((1, d), lambda i: (0, 0))
    resident = lambda shape, idx: pl.BlockSpec(shape, idx, pipeline_mode=pl.Buffered(1))
    return pl.pallas_call(
        functools.partial(_mixer_out_ffn_kernel, n_chunks=fp // FF_CHUNK),
        grid=(rows // tm,),
        in_specs=[pl.BlockSpec((tm, k), lambda i: (i, 0)),
                  pl.BlockSpec((tm, d), lambda i: (i, 0)),
                  _mod_spec(mod, tpb),
                  resident((k, d), lambda i: (0, 0)),
                  row(), row(),
                  resident((None, d, fp), lambda i: (layer, 0, 0)),
                  resident((None, d, fp), lambda i: (layer, 0, 0)),
                  resident((None, fp, d), lambda i: (layer, 0, 0)),
                  row(), row()],
        out_specs=pl.BlockSpec((tm, d), lambda i: (i, 0)),
        out_shape=jax.ShapeDtypeStruct((rows, d), F32),
        scratch_shapes=[pltpu.VMEM((tm, d), BF16), pltpu.VMEM((tm, d), F32)],
        name="mixer_out_ffn",
        compiler_params=_cparams("parallel"),
    )(m2d, x2d, mod, w_o, ln_g[1].reshape(1, d), ln_b[1].reshape(1, d), w_a, w_u, w_down,
      ln_g[2].reshape(1, d), ln_b[2].reshape(1, d))


def _ffn_block(x2d, mod, tiling, j, weights, layer, ln_g, ln_b):
    tm, tpb, _ = tiling
    rows, d = x2d.shape
    w_a, w_u, w_down = weights
    fp = w_a.shape[-1]
    kern = functools.partial(_ffn_kernel, j=j, n_chunks=fp // FF_CHUNK)
    return pl.pallas_call(
        kern,
        grid=(rows // tm,),
        in_specs=[pl.BlockSpec((tm, d), lambda i: (i, 0)),
                  _mod_spec(mod, tpb),
                  pl.BlockSpec((None, d, fp), lambda i: (layer, 0, 0), pipeline_mode=pl.Buffered(1)),
                  pl.BlockSpec((None, d, fp), lambda i: (layer, 0, 0), pipeline_mode=pl.Buffered(1)),
                  pl.BlockSpec((None, fp, d), lambda i: (layer, 0, 0), pipeline_mode=pl.Buffered(1)),
                  pl.BlockSpec((1, d), lambda i: (0, 0)),
                  pl.BlockSpec((1, d), lambda i: (0, 0))],
        out_specs=pl.BlockSpec((tm, d), lambda i: (i, 0)),
        out_shape=jax.ShapeDtypeStruct((rows, d), F32),
        scratch_shapes=[pltpu.VMEM((tm, d), BF16)],
        name="ffn_block",
        compiler_params=_cparams("parallel"),
    )(x2d, mod, w_a, w_u, w_down, ln_g.reshape(1, d), ln_b.reshape(1, d))


def _prep_ffn_weights(w_up, w_down):
    d_ff = w_up.shape[2] // 2
    fp = _round_up(d_ff, FF_CHUNK)
    pad_cols = lambda w: jnp.pad(w.astype(BF16), ((0, 0), (0, 0), (0, fp - d_ff)))
    down = jnp.pad(w_down.astype(BF16), ((0, 0), (0, fp - d_ff), (0, 0)))
    return pad_cols(w_up[:, :, :d_ff]), pad_cols(w_up[:, :, d_ff:]), down


PROJ_ROW_TILE = 512


def _proj_in_kernel(x_ref, mod_ref, w_ref, *o_refs, j, tn):
    h = (x_ref[...] * (1.0 + mod_ref[3 * j + 1]) + mod_ref[3 * j]).astype(BF16)
    per_out = o_refs[0].shape[1] // tn
    for k in range(w_ref.shape[1] // tn):
        cols = slice((k % per_out) * tn, (k % per_out + 1) * tn)
        o_refs[k // per_out][:, cols] = jnp.dot(h, w_ref[:, k * tn:(k + 1) * tn], preferred_element_type=F32)


def _proj_in(x2d, mod, tiling, j, w, tn, n_out=1):
    tm, tpb, per_batch = tiling
    if per_batch and tm % PROJ_ROW_TILE == 0:
        tpb = tpb * (tm // PROJ_ROW_TILE)
        tm = PROJ_ROW_TILE
    rows, d = x2d.shape
    n = w.shape[1]
    assert n % (n_out * tn) == 0
    outs = pl.pallas_call(
        functools.partial(_proj_in_kernel, j=j, tn=tn),
        grid=(rows // tm,),
        in_specs=[pl.BlockSpec((tm, d), lambda i: (i, 0)),
                  _mod_spec(mod, tpb),
                  pl.BlockSpec((d, n), lambda i: (0, 0), pipeline_mode=pl.Buffered(1))],
        out_specs=[pl.BlockSpec((tm, n // n_out), lambda i: (i, 0))] * n_out,
        out_shape=[jax.ShapeDtypeStruct((rows, n // n_out), F32)] * n_out,
        name="proj_in",
        compiler_params=_cparams("parallel"),
    )(x2d, mod, w)
    return outs[0] if n_out == 1 else outs


def _tri(c, strict=False):
    r = lax.broadcasted_iota(jnp.int32, (c, c), 0)
    s = lax.broadcasted_iota(jnp.int32, (c, c), 1)
    return (r > s) if strict else (r >= s)


def _hgrn_diag_masks(c):
    row = lax.broadcasted_iota(jnp.int32, (c, c), 0)
    col = lax.broadcasted_iota(jnp.int32, (c, c), 1)
    return [(col == row - d) & ((row % SUBLANES) >= d) for d in range(SUBLANES)]


def _hgrn_chunk(q, k, cum, v, st_prev, c, diag_masks):
    hs = range(len(q))
    nb = c // SUBLANES
    last_row = [cum[h][c - 1:c, :] for h in hs]
    o_state = [_bdot_nt(q[h] * jnp.exp(cum[h]), st_prev[h]) for h in hs]
    q3 = [q[h].reshape(nb, SUBLANES, LANES) for h in hs]
    k3 = [k[h].reshape(nb, SUBLANES, LANES) for h in hs]
    cum3 = [cum[h].reshape(nb, SUBLANES, LANES) for h in hs]

    att_off = None
    if nb > 1:
        blk = lax.broadcasted_iota(jnp.int32, (nb, SUBLANES, LANES), 0)
        q_cat, k_cat = [], []
        for h in hs:
            e = cum3[h][:, SUBLANES - 1:SUBLANES, :]
            k_t = k3[h] * jnp.exp(e - cum3[h])
            q_parts, k_parts = [], []
            for jb in range(nb - 1):
                qj = q3[h][jb + 1:] * jnp.exp(cum3[h][jb + 1:] - e[jb:jb + 1])
                qj = jnp.concatenate([jnp.zeros((jb + 1, SUBLANES, LANES), F32), qj], axis=0)
                q_parts.append(qj.reshape(c, LANES).astype(BF16))
                k_parts.append(jnp.where(blk == jb, k_t, 0.0).reshape(c, LANES).astype(BF16))
            q_cat.append(jnp.concatenate(q_parts, axis=1))
            k_cat.append(jnp.concatenate(k_parts, axis=1))
        att_off = [_bdot_nt(q_cat[h], k_cat[h]) for h in hs]

    att = [jnp.zeros((c, c), F32) for _ in hs]
    for d in reversed(range(SUBLANES)):
        for h in hs:
            if d == 0:
                p = q3[h] * k3[h]
            else:
                p = q3[h] * pltpu.roll(k3[h], d, 1) * jnp.exp(cum3[h] - pltpu.roll(cum3[h], d, 1))
            a_d = jnp.sum(p.reshape(c, LANES), axis=-1, keepdims=True)
            att[h] = jnp.where(diag_masks[d], a_d, att[h])
    if att_off is not None:
        att = [att[h] + att_off[h] for h in hs]

    o = [o_state[h] + _bdot(att[h], v[h]) for h in hs]
    st_new = [st_prev[h] * jnp.exp(last_row[h]) + _bdot_tn(v[h], k[h] * jnp.exp(last_row[h] - cum[h])) for h in hs]
    return o, st_new


REC_L_TILE = 512
HGRN_HEADS_PER_STEP = 8
GDN_HEADS_PER_STEP = 8
GDN_STACK = 4


def _head_lanes(hh):
    return slice(hh * HEAD_DIM, (hh + 1) * HEAD_DIM)


def _rec_tiling(l):
    c = min(CHUNK, l)
    tl = min(REC_L_TILE, l)
    assert l % tl == 0 and tl % c == 0 and c % SUBLANES == 0
    return c, tl


def _hgrn_kernel(q_ref, f_ref, v_ref, g_ref, lb_ref, ng_ref, s0_ref, o_ref, s_ref, *, c, n_chunks, hb):
    @pl.when(pl.program_id(2) == 0)
    def _():
        for hh in range(hb):
            s_ref[hh] = s0_ref[hh].T

    ng = ng_ref[...]
    lb = lb_ref[...]
    tril3 = _block_tril3(c, c)
    diag_masks = _hgrn_diag_masks(c)

    def body(i, carry):
        rows = pl.ds(pl.multiple_of(i * c, c), c)
        fr = f_ref[rows, :]
        q_all = _silu(q_ref[rows, :]) * (HEAD_DIM ** -0.5)
        sg = _sigmoid(fr)
        k_all = (1.0 - lb) * (1.0 - sg)
        cum_all = _prefix_dot(tril3, jnp.log(lb + (1.0 - lb) * sg))
        lanes = [_head_lanes(hh) for hh in range(hb)]
        o, st_new = _hgrn_chunk([q_all[:, ln] for ln in lanes], [k_all[:, ln] for ln in lanes],
                                [cum_all[:, ln] for ln in lanes], [v_ref[rows, ln] for ln in lanes],
                                [s_ref[hh] for hh in range(hb)], c, diag_masks)
        for hh, ln in enumerate(lanes):
            s_ref[hh] = st_new[hh]
            ms = jnp.mean(o[hh] * o[hh], axis=-1, keepdims=True)
            o_ref[rows, ln] = o[hh] * lax.rsqrt(ms + 1e-6) * ng * _sigmoid(g_ref[rows, ln])
        return carry

    lax.fori_loop(0, n_chunks, body, 0)

    @pl.when(pl.program_id(2) == pl.num_programs(2) - 1)
    def _():
        for hh in range(hb):
            s_ref[hh] = s_ref[hh].T


def _hgrn_recurrence(proj, s0, layer, lb, norm_g):
    b, l, _ = proj.shape
    c, tl = _rec_tiling(l)
    hb = HGRN_HEADS_PER_STEP
    ng_ = HEADS // hb
    col = lambda off: pl.BlockSpec((None, tl, hb * HEAD_DIM), lambda bi, hg, t: (bi, t, off * ng_ + hg))
    state = pl.BlockSpec((None, hb, HEAD_DIM, HEAD_DIM), lambda bi, hg, t: (bi, hg, 0, 0))
    return pl.pallas_call(
        functools.partial(_hgrn_kernel, c=c, n_chunks=tl // c, hb=hb),
        grid=(b, ng_, l // tl),
        in_specs=[col(0), col(1), col(2), col(3),
                  pl.BlockSpec((1, hb * HEAD_DIM), lambda bi, hg, t: (0, hg)),
                  pl.BlockSpec((1, HEAD_DIM), lambda bi, hg, t: (0, 0)),
                  pl.BlockSpec((None, None, hb, HEAD_DIM, HEAD_DIM), lambda bi, hg, t: (layer, bi, hg, 0, 0))],
        out_specs=[col(0), state],
        out_shape=[jax.ShapeDtypeStruct((b, l, HEADS * HEAD_DIM), F32),
                   jax.ShapeDtypeStruct((b, HEADS, HEAD_DIM, HEAD_DIM), F32)],
        name="hgrn_recurrence",
        compiler_params=_cparams("parallel", "parallel", "arbitrary"),
    )(proj, proj, proj, proj, lb.reshape(1, -1), norm_g.reshape(1, -1), s0)


HIST = SUBLANES


def _gdn_conv(x_ref, r0, lanes, cw, c):
    x_ext = x_ref[pl.ds(r0, c + HIST), lanes]
    acc = x_ext * cw[CONV_W - 1:CONV_W, :]
    for w in range(CONV_W - 1):
        acc = acc + pltpu.roll(x_ext, CONV_W - 1 - w, 0) * cw[w:w + 1, :]
    return _silu(acc[HIST:, :])


def _l2norm_rows(x):
    return x * lax.rsqrt(jnp.sum(x * x, axis=-1, keepdims=True) + 1e-6)


def _gdn_masks(r, c):
    row = lax.broadcasted_iota(jnp.int32, (r, r), 0)
    col = lax.broadcasted_iota(jnp.int32, (r, r), 1)
    same = (row // c) == (col // c)
    low = same & (row >= col)
    return low, same & (row > col), jnp.where(row == col, 1.0, 0.0), _block_tril3(r, c)


def _gdn_wy(groups, c, masks):
    low, strict, eye, tril3 = masks
    gr = range(len(groups))
    qs, ks, vs, betas, gs = zip(*groups)
    r = qs[0].shape[0]
    n = r // c
    pre = [_prefix_dot(tril3, jnp.concatenate(
        [jnp.where(strict, jnp.broadcast_to(gs[g], (r, r)), 0.0), jnp.broadcast_to(gs[g], (r, LANES))], axis=1))
        for g in gr]
    cum = [pre[g][:, r:] for g in gr]
    decay = [jnp.where(low, jnp.exp(jnp.where(low, pre[g][:, :r], 0.0)), 0.0) for g in gr]
    ecum = [jnp.exp(cum[g]) for g in gr]
    kb = [ks[g] * betas[g] for g in gr]
    kk = [_bdot_nt(jnp.concatenate([kb[g], qs[g]], axis=0), ks[g]) for g in gr]
    p = [-jnp.where(strict, kk[g][:r] * decay[g], 0.0) for g in gr]
    att = [kk[g][r:] * decay[g] for g in gr]
    t_inv = [eye + p[g] for g in gr]
    x = p
    for _ in range(int(math.log2(c)) - 1):
        x = [_bdot(x[g], x[g]) for g in gr]
        t_inv = [t_inv[g] + _bdot(x[g], t_inv[g]) for g in gr]
    uw = [_bdot(t_inv[g], jnp.concatenate([vs[g] * betas[g], kb[g] * ecum[g]], axis=1)) for g in gr]
    qe = [qs[g] * ecum[g] for g in gr]
    last = [jnp.concatenate([jnp.broadcast_to(cum[g][(h + 1) * c - 1:(h + 1) * c, :], (c, LANES))
                             for h in range(n)], axis=0) for g in gr]
    kt = [ks[g] * jnp.exp(last[g] - cum[g]) for g in gr]
    return [(uw[g], att[g], qe[g], kt[g], jnp.exp(last[g])) for g in gr]


def _gdn_state_step(wy, s_prev, c):
    gr = range(len(wy))
    uw, att, qe, kt, ecl = zip(*wy)
    n = len(s_prev[0])
    head_rows = [slice(h * c, (h + 1) * c) for h in range(n)]
    ws = [[_bdot(jnp.concatenate([uw[g][rows, HEAD_DIM:], qe[g][rows]], axis=0), s_prev[g][h])
           for h, rows in enumerate(head_rows)] for g in gr]
    v_new = [jnp.concatenate([uw[g][rows, :HEAD_DIM] - ws[g][h][:c] for h, rows in enumerate(head_rows)], axis=0)
             for g in gr]
    o = [jnp.concatenate([ws[g][h][c:] for h in range(n)], axis=0) + _bdot(att[g], v_new[g]) for g in gr]
    s_new = [[s_prev[g][h] * ecl[g][h * c:h * c + 1, :] + _bdot_tn(kt[g][rows], v_new[g][rows])
              for h, rows in enumerate(head_rows)] for g in gr]
    return [(o[g], s_new[g]) for g in gr]


def _gdn_kernel(q_ref, k_ref, v_ref, z_ref, ba_ref, c0q_ref, c0k_ref, c0v_ref, cwq_ref, cwk_ref, cwv_ref,
                an_ref, dt_ref, ng_ref, s0_ref, o_ref, s_ref, qx_ref, kx_ref, vx_ref, *wy_refs,
                c, n_chunks, tl, hb):
    hg = pl.program_id(1)
    t = pl.program_id(2)
    staged = ((q_ref, c0q_ref, qx_ref), (k_ref, c0k_ref, kx_ref), (v_ref, c0v_ref, vx_ref))

    @pl.when(t == 0)
    def _():
        s_ref[...] = s0_ref[...]
        for _, c0, dst in staged:
            dst[0:HIST, :] = jnp.zeros((HIST, hb * HEAD_DIM), F32)
            dst[HIST - (CONV_W - 1):HIST, :] = c0[...]

    @pl.when(t > 0)
    def _():
        for _, _, dst in staged:
            dst[0:HIST, :] = dst[tl:tl + HIST, :]

    for src, _, dst in staged:
        dst[HIST:HIST + tl, :] = src[...]
    ng = ng_ref[...]
    lane = lax.broadcasted_iota(jnp.int32, (c, LANES), 1)
    masks = _gdn_masks(GDN_STACK * c, c)
    n_groups = hb // GDN_STACK
    stack = lambda parts: jnp.concatenate(parts, axis=0)

    def chunk_groups(i):
        r0 = pl.multiple_of(i * c, c)
        ba = ba_ref[pl.ds(r0, c), :]
        groups = []
        for h0 in range(0, hb, GDN_STACK):
            qs, ks, vs, betas, gs = [], [], [], [], []
            for hh in range(h0, h0 + GDN_STACK):
                ln = _head_lanes(hh)
                head = hg * hb + hh
                qs.append(_l2norm_rows(_gdn_conv(qx_ref, r0, ln, cwq_ref[:, ln], c)) * (HEAD_DIM ** -0.5))
                ks.append(_l2norm_rows(_gdn_conv(kx_ref, r0, ln, cwk_ref[:, ln], c)))
                vs.append(_gdn_conv(vx_ref, r0, ln, cwv_ref[:, ln], c))
                betas.append(_sigmoid(jnp.sum(jnp.where(lane == head, ba, 0.0), axis=-1, keepdims=True)))
                a_raw = jnp.sum(jnp.where(lane == HEADS + head, ba, 0.0), axis=-1, keepdims=True)
                xa = a_raw + dt_ref[:, ln][:, 0:1]
                softplus = jnp.maximum(xa, 0.0) + jnp.log(1.0 + jnp.exp(-jnp.abs(xa)))
                gs.append(an_ref[:, ln][:, 0:1] * softplus)
            groups.append((stack(qs), stack(ks), stack(vs), stack(betas), stack(gs)))
        return groups

    per_iter = 2 if n_chunks % 2 == 0 else 1

    def wy_body(ip, carry):
        chunks = [ip * per_iter + j for j in range(per_iter)]
        wy = _gdn_wy([g for i in chunks for g in chunk_groups(i)], c, masks)
        for j, i in enumerate(chunks):
            for gi in range(n_groups):
                for ref, val in zip(wy_refs, wy[j * n_groups + gi]):
                    ref[i, gi] = val
        return carry

    lax.fori_loop(0, n_chunks // per_iter, wy_body, 0)

    def state_body(i, carry):
        rows = pl.ds(pl.multiple_of(i * c, c), c)
        wy = [tuple(ref[i, gi] for ref in wy_refs) for gi in range(n_groups)]
        s_prev = [[s_ref[gi * GDN_STACK + j] for j in range(GDN_STACK)] for gi in range(n_groups)]
        for gi, (o, s_new) in enumerate(_gdn_state_step(wy, s_prev, c)):
            for j in range(GDN_STACK):
                hh = gi * GDN_STACK + j
                ln = _head_lanes(hh)
                s_ref[hh] = s_new[j]
                oh = o[j * c:(j + 1) * c]
                ms = jnp.mean(oh * oh, axis=-1, keepdims=True)
                o_ref[rows, ln] = oh * lax.rsqrt(ms + 1e-6) * ng * _silu(z_ref[rows, ln])
        return carry

    lax.fori_loop(0, n_chunks, state_body, 0)


def _gdn_recurrence(proj, conv0, s0, layer, conv_w, a_log, dt_bias, norm_g):
    b, l, _ = proj.shape
    c, tl = _rec_tiling(l)
    hb = GDN_HEADS_PER_STEP
    ng_ = HEADS // hb
    w = hb * HEAD_DIM
    col = lambda off: pl.BlockSpec((None, tl, w), lambda bi, hg, t: (bi, t, off * ng_ + hg))
    c0 = lambda off: pl.BlockSpec((None, None, CONV_W - 1, w), lambda bi, hg, t: (layer, bi, 0, off * ng_ + hg))
    cw = lambda off: pl.BlockSpec((CONV_W, w), lambda bi, hg, t: (0, off * ng_ + hg))
    per_head = pl.BlockSpec((1, w), lambda bi, hg, t: (0, hg))
    state = pl.BlockSpec((None, hb, HEAD_DIM, HEAD_DIM), lambda bi, hg, t: (bi, hg, 0, 0))
    a_neg = jnp.repeat(-jnp.exp(a_log.astype(F32)), HEAD_DIM)[None, :]
    dt_b = jnp.repeat(dt_bias.astype(F32), HEAD_DIM)[None, :]
    xbuf = pltpu.VMEM((HIST + tl, w), F32)
    return pl.pallas_call(
        functools.partial(_gdn_kernel, c=c, n_chunks=tl // c, tl=tl, hb=hb),
        grid=(b, ng_, l // tl),
        in_specs=[col(0), col(1), col(2), col(3),
                  pl.BlockSpec((None, tl, LANES), lambda bi, hg, t: (bi, t, 4 * HEADS)),
                  c0(0), c0(1), c0(2), cw(0), cw(1), cw(2),
                  per_head, per_head,
                  pl.BlockSpec((1, HEAD_DIM), lambda bi, hg, t: (0, 0)),
                  pl.BlockSpec((None, None, hb, HEAD_DIM, HEAD_DIM), lambda bi, hg, t: (layer, bi, hg, 0, 0))],
        out_specs=[col(0), state],
        out_shape=[jax.ShapeDtypeStruct((b, l, HEADS * HEAD_DIM), F32),
                   jax.ShapeDtypeStruct((b, HEADS, HEAD_DIM, HEAD_DIM), F32)],
        scratch_shapes=[xbuf, xbuf, xbuf] + [
            pltpu.VMEM((tl // c, hb // GDN_STACK, GDN_STACK * c, width), F32)
            for width in (2 * HEAD_DIM, GDN_STACK * c, HEAD_DIM, HEAD_DIM, HEAD_DIM)],
        name="gdn_recurrence",
        compiler_params=_cparams("parallel", "parallel", "arbitrary"),
    )(proj, proj, proj, proj, proj, conv0, conv0, conv0, conv_w, conv_w, conv_w,
      a_neg, dt_b, norm_g.reshape(1, -1), s0)


def _rel_bucket(dist):
    exact = REL_BUCKETS // 2
    d = jnp.maximum(dist, exact).astype(F32)
    large = exact + (jnp.log(d / exact) / math.log(REL_MAX_DIST / exact) * (REL_BUCKETS - exact)).astype(jnp.int32)
    return jnp.where(dist < exact, dist, jnp.minimum(large, REL_BUCKETS - 1))


def _top_mask(gate, n_valid, k, axis):
    idx = lax.broadcasted_iota(jnp.int32, gate.shape, axis)
    big = jnp.int32(2 ** 30)
    work = jnp.where(idx < n_valid, gate, NEG)
    sel = jnp.zeros(gate.shape, F32)
    for r in range(k):
        mx = jnp.max(work, axis=axis, keepdims=True)
        first = jnp.min(jnp.where(work == mx, idx, big), axis=axis, keepdims=True)
        pick = (idx == first) & (idx < n_valid)
        sel = jnp.where(pick, 1.0, sel)
        work = jnp.where(idx == first, -jnp.inf, work)
    return sel


MOBA_HEADS_PER_STEP = 4
MOBA_FAR_BLOCKS_PER_STEP = 3


def _moba_prompt_kernel(q_ref, k_ref, v_ref, bd_ref, bp_ref, bf_ref, o_ref, means_ref, sel_ref, *, nb, hb):
    i = pl.program_id(2)
    blk = MOBA_BLOCK

    @pl.when(i == 0)
    def _():
        means_ref[...] = jnp.zeros_like(means_ref)
        for hh in range(hb):
            for n in range(nb):
                means_ref[hh, n:n + 1, :] = jnp.mean(k_ref[n * blk:(n + 1) * blk, _head_lanes(hh)],
                                                     axis=0, keepdims=True)

    causal = ~_tri(blk, strict=True)
    r0 = pl.multiple_of(i * blk, blk)
    heads = range(hb)
    qs = [q_ref[:, _head_lanes(hh)] * (HEAD_DIM ** -0.5) for hh in heads]
    qbs = [q.astype(BF16) for q in qs]
    gates = [lax.dot_general(means_ref[hh], qs[hh], (((1,), (1,)), ((), ())),
                             preferred_element_type=F32, precision=HI) for hh in heads]
    qk = [_bdot_nt(k_ref[pl.ds(r0, blk), _head_lanes(hh)], qbs[hh]) for hh in heads]
    for hh in heads:
        sel_ref[hh] = _top_mask(gates[hh], i, MOBA_TOPK, 0)
    s0 = [jnp.where(causal, qk[hh] + bd_ref[hh], NEG) for hh in heads]
    m0 = [jnp.max(s, axis=0, keepdims=True) for s in s0]
    p0 = [jnp.exp(s0[hh] - m0[hh]) for hh in heads]
    init = [(m0[hh], jnp.sum(p0[hh], axis=0, keepdims=True),
             _bdot_tn(v_ref[pl.ds(r0, blk), _head_lanes(hh)], p0[hh])) for hh in heads]

    def past_blocks(n, count, carry, bias_of_head, limit):
        rn = pl.multiple_of(n * blk, blk)
        rows = pl.ds(rn, count * blk)
        heads = range(hb)
        qk = [_bdot_nt(k_ref[rows, _head_lanes(hh)], qbs[hh]) for hh in heads]
        sn = []
        for hh in heads:
            parts = []
            for j in range(count):
                picked = sel_ref[hh, pl.ds(n + j, 1), :] * (n + j < limit).astype(F32) > 0.0
                parts.append(jnp.where(picked, qk[hh][j * blk:(j + 1) * blk] + bias_of_head(hh), NEG))
            sn.append(parts[0] if count == 1 else jnp.concatenate(parts, axis=0))
        m_new = [jnp.maximum(carry[hh][0], jnp.max(sn[hh], axis=0, keepdims=True)) for hh in heads]
        pn = [jnp.exp(sn[hh] - m_new[hh]) for hh in heads]
        pv = [_bdot_tn(v_ref[rows, _head_lanes(hh)], pn[hh]) for hh in heads]
        out = []
        for hh in heads:
            m, l, acc = carry[hh]
            alpha = jnp.exp(m - m_new[hh])
            out.append((m_new[hh], l * alpha + jnp.sum(pn[hh], axis=0, keepdims=True), acc * alpha + pv[hh]))
        return tuple(out)

    n_far = jnp.maximum(i - 1, 0)
    per = MOBA_FAR_BLOCKS_PER_STEP
    carry = lax.fori_loop(0, (n_far + per - 1) // per,
                          lambda g, cr: past_blocks(per * g, per, cr, lambda hh: bf_ref[:, _head_lanes(hh)][:, 0:1],
                                                    n_far),
                          tuple(init))
    final = past_blocks(n_far, 1, carry, lambda hh: bp_ref[hh], i)
    for hh in range(hb):
        _, l, acc = final[hh]
        o_ref[:, _head_lanes(hh)] = (acc / l).T


def _bias_of_distance(rel_bias, dist):
    bucket = _rel_bucket(jnp.maximum(dist, 0))[None]
    out = jnp.zeros((HEADS,) + dist.shape, F32)
    for b in range(REL_BUCKETS):
        out = jnp.where(bucket == b, rel_bias[b].astype(F32).reshape((HEADS,) + (1,) * dist.ndim), out)
    return out


def _moba_bias_tables(rel_bias, blk):
    t = jnp.arange(blk, dtype=jnp.int32)
    d0 = t[None, :] - t[:, None]
    bias_diag = _bias_of_distance(rel_bias, d0)
    bias_prev = _bias_of_distance(rel_bias, d0 + blk)
    far = jnp.repeat(rel_bias[REL_BUCKETS - 1].astype(F32), HEAD_DIM)[None, :]
    return bias_diag, bias_prev, far


def _moba_prompt(q, k, v, rel_bias):
    b, l, _ = q.shape
    blk = MOBA_BLOCK
    assert l % blk == 0 and l // blk <= LANES
    assert 2 * blk > REL_MAX_DIST
    nb = l // blk
    bias_diag, bias_prev, far = _moba_bias_tables(rel_bias, blk)
    hb = MOBA_HEADS_PER_STEP
    ng_ = HEADS // hb
    w = hb * HEAD_DIM
    tile = pl.BlockSpec((hb, blk, blk), lambda bi, hg, i: (hg, 0, 0))
    return pl.pallas_call(
        functools.partial(_moba_prompt_kernel, nb=nb, hb=hb),
        grid=(b, ng_, nb),
        in_specs=[pl.BlockSpec((None, blk, w), lambda bi, hg, i: (bi, i, hg)),
                  pl.BlockSpec((None, l, w), lambda bi, hg, i: (bi, 0, hg)),
                  pl.BlockSpec((None, l, w), lambda bi, hg, i: (bi, 0, hg)),
                  tile, tile,
                  pl.BlockSpec((1, w), lambda bi, hg, i: (0, hg))],
        out_specs=pl.BlockSpec((None, blk, w), lambda bi, hg, i: (bi, i, hg)),
        out_shape=jax.ShapeDtypeStruct((b, l, HEADS * HEAD_DIM), F32),
        scratch_shapes=[pltpu.VMEM((hb, _round_up(nb, SUBLANES), HEAD_DIM), F32),
                        pltpu.VMEM((hb, _round_up(nb, SUBLANES), blk), F32)],
        name="moba_prompt",
        compiler_params=_cparams("parallel", "parallel", "arbitrary"),
    )(q, k, v, bias_diag, bias_prev, far)


def _block_diag_rows(x, lq):
    return jnp.concatenate([x[h * lq:(h + 1) * lq, h * HEAD_DIM:(h + 1) * HEAD_DIM] for h in range(HEADS)], axis=0)


MOBA_SAMPLE_BLOCKS_PER_STEP = 8


def _moba_sample_kernel(pt_ref, *refs, nb, lq, bps):
    n_pages = bps * (MOBA_BLOCK // PAGE_SIZE)
    k_refs, v_refs = refs[:n_pages], refs[n_pages:2 * n_pages]
    (q_ref, kn_ref, vn_ref, bl_ref, bc_ref, bf_ref, o_ref,
     qbd_ref, m_ref, l_ref, acc_ref, means_ref) = refs[2 * n_pages:]
    step = pl.program_id(1)
    d_model = HEADS * HEAD_DIM
    hq = HEADS * lq

    @pl.when(step == 0)
    def _():
        q = q_ref[...] * (HEAD_DIM ** -0.5)
        rows = lax.broadcasted_iota(jnp.int32, (d_model, hq), 0) // HEAD_DIM
        cols = lax.broadcasted_iota(jnp.int32, (d_model, hq), 1)
        spread = (lax.broadcasted_iota(jnp.int32, (lq, hq), 1) % lq
                  == lax.broadcasted_iota(jnp.int32, (lq, hq), 0)).astype(F32)
        qt = lax.dot_general(q, spread, (((0,), (0,)), ((), ())),
                             preferred_element_type=F32, precision=HI)
        qbd_ref[...] = jnp.where(rows == cols // lq, qt, 0.0)
        means_ref[...] = jnp.zeros_like(means_ref)
        m_ref[...] = jnp.zeros_like(m_ref)
        l_ref[...] = jnp.zeros_like(l_ref)

    qbd = qbd_ref[...]

    def rows_by_lanes(*pages):
        return jnp.concatenate(
            [jnp.concatenate([pg[pl.ds(h, PAGE_SIZE, stride=HEADS), :] for h in range(HEADS)], axis=1)
             for pg in pages], axis=0)

    js = range(bps)
    ns = [step * bps + j for j in js]
    kblk = [rows_by_lanes(k_refs[2 * j], k_refs[2 * j + 1]) for j in js]
    vblk = [rows_by_lanes(v_refs[2 * j], v_refs[2 * j + 1]) for j in js]
    s = [_bdot(kblk[j], qbd) + jnp.where(ns[j] == nb - 1, bl_ref[...], bf_ref[...]) for j in js]
    for j in js:
        means_ref[pl.ds(ns[j], 1), :] = jnp.mean(kblk[j], axis=0, keepdims=True)
    m = [jnp.max(s[j], axis=0, keepdims=True) for j in js]
    p = [jnp.exp(s[j] - m[j]) for j in js]
    pv = [_bdot_tn(p[j], vblk[j]) for j in js]
    for j in js:
        m_ref[pl.ds(ns[j], 1), :] = m[j]
        l_ref[pl.ds(ns[j], 1), :] = jnp.sum(p[j], axis=0, keepdims=True)
        acc_ref[ns[j]] = _block_diag_rows(pv[j], lq)

    @pl.when(step == pl.num_programs(1) - 1)
    def _():
        k_new = kn_ref[...]
        v_new = vn_ref[...]
        jq = lax.broadcasted_iota(jnp.int32, (lq, hq), 1) % lq
        jk = lax.broadcasted_iota(jnp.int32, (lq, hq), 0)
        sc = jnp.where(jk <= jq, _bdot(k_new, qbd) + bc_ref[...], NEG)
        mc = jnp.max(sc, axis=0, keepdims=True)
        pc = jnp.exp(sc - mc)
        lc = jnp.sum(pc, axis=0, keepdims=True)
        oc = _block_diag_rows(_bdot_tn(pc, v_new), lq)

        gate = _fdot(means_ref[...], qbd)
        sel = _top_mask(gate, nb, MOBA_TOPK, 0)
        m_all = m_ref[...]
        m_tot = jnp.maximum(jnp.max(jnp.where(sel > 0.0, m_all, NEG), axis=0, keepdims=True), mc)
        w = jnp.where(sel > 0.0, jnp.exp(jnp.where(sel > 0.0, m_all - m_tot, 0.0)), 0.0)
        wc = jnp.exp(mc - m_tot)
        denom = jnp.sum(w * l_ref[...], axis=0, keepdims=True) + wc * lc
        nbp = w.shape[0]
        w_all = jnp.concatenate([w, wc, jnp.zeros((SUBLANES - 1, hq), F32)], axis=0) / denom
        ne = nbp + SUBLANES
        eye = (lax.broadcasted_iota(jnp.int32, (ne, ne), 0)
               == lax.broadcasted_iota(jnp.int32, (ne, ne), 1)).astype(F32)
        wt = lax.dot_general(w_all, eye, (((0,), (0,)), ((), ())),
                             preferred_element_type=F32, precision=HI)
        out = wt[:, nbp:nbp + 1] * oc
        for b_i in range(nb):
            out = out + wt[:, b_i:b_i + 1] * acc_ref[b_i]
        o_ref[...] = out


def _moba_sample(q, k, v, cache_k, cache_v, layer, page_table, rel_bias):
    b, lq, _ = q.shape
    n_pages = page_table.shape[1]
    d_model = HEADS * HEAD_DIM
    pages_per_block = MOBA_BLOCK // PAGE_SIZE
    assert pages_per_block == 2 and n_pages % pages_per_block == 0 and lq <= MOBA_BLOCK
    assert cache_k.shape[2:] == (PAGE_SIZE, HEADS, HEAD_DIM)
    nb = n_pages // pages_per_block
    assert nb >= 1 and MOBA_BLOCK >= REL_MAX_DIST
    nbp = _round_up(nb, SUBLANES)
    hq = HEADS * lq
    jq = jnp.arange(lq, dtype=jnp.int32)
    t = jnp.arange(MOBA_BLOCK, dtype=jnp.int32)
    by_cols = lambda tab: jnp.transpose(tab, (1, 0, 2)).reshape(tab.shape[1], hq)
    bias_last = by_cols(_bias_of_distance(rel_bias, (MOBA_BLOCK + jq)[None, :] - t[:, None]))
    bias_cur = by_cols(_bias_of_distance(rel_bias, jq[None, :] - jq[:, None]))
    bias_far = jnp.repeat(rel_bias[REL_BUCKETS - 1].astype(F32), lq)[None, :]

    n_c, n_pool = cache_k.shape[:2]
    cache_k = cache_k.reshape(n_c, n_pool, PAGE_SIZE * HEADS, HEAD_DIM)
    cache_v = cache_v.reshape(n_c, n_pool, PAGE_SIZE * HEADS, HEAD_DIM)
    bps = math.gcd(MOBA_SAMPLE_BLOCKS_PER_STEP, nb)
    pages_per_step = bps * pages_per_block
    page = lambda off: pl.BlockSpec((None, None, PAGE_SIZE * HEADS, HEAD_DIM),
                                    lambda bi, n, pt: (layer, pt[bi, pages_per_step * n + off], 0, 0))
    pages = [page(off) for off in range(pages_per_step)]
    full = lambda shape: pl.BlockSpec(shape, lambda bi, n, pt: (0,) * len(shape))
    grid_spec = pltpu.PrefetchScalarGridSpec(
        num_scalar_prefetch=1,
        grid=(b, nb // bps),
        in_specs=pages + pages + [
                  *([pl.BlockSpec((None, lq, d_model), lambda bi, n, pt: (bi, 0, 0))] * 3),
                  full((MOBA_BLOCK, hq)), full((lq, hq)), full((1, hq))],
        out_specs=pl.BlockSpec((None, hq, HEAD_DIM), lambda bi, n, pt: (bi, 0, 0)),
        scratch_shapes=[pltpu.VMEM((d_model, hq), F32),
                        pltpu.VMEM((nbp, hq), F32),
                        pltpu.VMEM((nbp, hq), F32),
                        pltpu.VMEM((nb, hq, HEAD_DIM), F32),
                        pltpu.VMEM((nbp, d_model), F32)])
    return pl.pallas_call(
        functools.partial(_moba_sample_kernel, nb=nb, lq=lq, bps=bps),
        grid_spec=grid_spec,
        out_shape=jax.ShapeDtypeStruct((b, hq, HEAD_DIM), F32),
        name="moba_sample",
        compiler_params=_cparams("parallel", "arbitrary"),
    )(page_table, *([cache_k] * pages_per_step), *([cache_v] * pages_per_step), q, k, v,
      bias_last, bias_cur, bias_far)


def _trunk(x, mods, s_hgrn, s_conv, s_gdn, paged, W):
    b, l, d = x.shape
    tiling = _row_tile(b, l)
    x2 = x.reshape(b * l, d)
    new_h, new_c, new_g, new_k, new_v = [], [], [], [], []
    for i in range(DEPTH):
        mod = _mod_layout(mods[i], b, l)
        x2 = _ffn_block(x2, mod, tiling, 0, W['ffn1'], i, W['ln_g'][i, 0], W['ln_b'][i, 0])
        kind, li = i % N_MIXERS, i // N_MIXERS
        if kind == 0:
            proj = _proj_in(x2, mod, tiling, 1, W['hgrn_w_in'][li], 1024)
            o, s = _hgrn_recurrence(proj.reshape(b, l, -1), s_hgrn, li, W['hgrn_lb'][li], W['hgrn_norm_g'][li])
            new_h.append(s)
            w_o = W['hgrn_w_o'][li]
            o2 = o.reshape(b * l, d)
        elif kind == 1:
            proj = _proj_in(x2, mod, tiling, 1, W['gdn_w_in'][li], W['gdn_w_in'][li].shape[1] // 3)
            proj = proj.reshape(b, l, -1)
            o, s = _gdn_recurrence(proj, s_conv, s_gdn, li, W['gdn_conv_w'][li], W['gdn_a_log'][li],
                                   W['gdn_dt_bias'][li], W['gdn_norm_g'][li])
            n_qkv = 3 * HEADS * HEAD_DIM
            assert l >= CONV_W - 1
            new_c.append(proj[:, l - (CONV_W - 1):, :n_qkv])
            new_g.append(s)
            w_o = W['gdn_w_o'][li]
            o2 = o.reshape(b * l, d)
        else:
            q, k, v = (t.reshape(b, l, d) for t in _proj_in(x2, mod, tiling, 1, W['moba_w_qkv'][li], 1024, n_out=3))
            if paged is None:
                o2 = _moba_prompt(q, k, v, W['rel_bias']).reshape(b * l, d)
            else:
                ck, cv, pt = paged
                o = _moba_sample(q, k, v, ck, cv, li, pt, W['rel_bias'])
                o2 = jnp.transpose(o.reshape(b, HEADS, l, HEAD_DIM), (0, 2, 1, 3)).reshape(b * l, d)
            new_k.append(k.reshape(b, l, HEADS, HEAD_DIM))
            new_v.append(v.reshape(b, l, HEADS, HEAD_DIM))
            w_o = W['moba_w_o'][li]
        x2 = _mixer_out_ffn(o2, x2, mod, tiling, w_o, W['ffn2'], i, W['ln_g'][i], W['ln_b'][i])
    return (x2.reshape(b, l, d), jnp.stack(new_h), jnp.stack(new_c), jnp.stack(new_g),
            jnp.stack(new_k), jnp.stack(new_v))


def kernel(x_prompt, x_sample, state_hgrn, state_gdn_conv, state_gdn, cache_k, cache_v, page_table, c_prompt, c_sample, w_ada, b_ada, ln_g, ln_b, w_ffn1_up, w_ffn1_down, w_ffn2_up, w_ffn2_down, hgrn_w_in, hgrn_lb_logits, hgrn_norm_g, hgrn_w_o, gdn_w_in, gdn_conv_w, gdn_a_log, gdn_dt_bias, gdn_norm_g, gdn_w_o, moba_w_qkv, moba_w_o, rel_bias):
    bp, _, d = x_prompt.shape
    bs = x_sample.shape[0]
    assert d == HEADS * HEAD_DIM

    nc = _round_up(bp + bs, SUBLANES)
    c_all = jnp.pad(jnp.concatenate([c_prompt, c_sample], axis=0), ((0, nc - bp - bs), (0, 0)))
    mods = _ada_mods(c_all, w_ada, b_ada).reshape(DEPTH, nc, 9, d)

    gdn_n = _round_up(gdn_w_in.shape[2], 3 * LANES)
    lbp = jax.nn.softmax(hgrn_lb_logits.astype(F32), axis=0)
    W = {
        'ffn1': _prep_ffn_weights(w_ffn1_up, w_ffn1_down), 'ffn2': _prep_ffn_weights(w_ffn2_up, w_ffn2_down),
        'ln_g': ln_g, 'ln_b': ln_b,
        'hgrn_w_in': hgrn_w_in.astype(BF16), 'hgrn_lb': jnp.cumsum(lbp, axis=0) - lbp[0],
        'hgrn_norm_g': hgrn_norm_g, 'hgrn_w_o': hgrn_w_o.astype(BF16),
        'gdn_w_in': jnp.pad(gdn_w_in, ((0, 0), (0, 0), (0, gdn_n - gdn_w_in.shape[2]))).astype(BF16),
        'gdn_conv_w': gdn_conv_w, 'gdn_a_log': gdn_a_log, 'gdn_dt_bias': gdn_dt_bias,
        'gdn_norm_g': gdn_norm_g, 'gdn_w_o': gdn_w_o.astype(BF16),
        'moba_w_qkv': moba_w_qkv.astype(BF16), 'moba_w_o': moba_w_o.astype(BF16), 'rel_bias': rel_bias,
    }

    n_a, n_b = state_hgrn.shape[0], state_gdn.shape[0]
    z_hgrn = jnp.zeros((n_a, bp) + state_hgrn.shape[2:], F32)
    z_conv = jnp.zeros((n_b, bp) + state_gdn_conv.shape[2:], F32)
    z_gdn = jnp.zeros((n_b, bp) + state_gdn.shape[2:], F32)
    y_p, hgrn_p, conv_p, gdn_p, k_p, v_p = _trunk(x_prompt, mods[:, :bp], z_hgrn, z_conv, z_gdn, None, W)

    y_s, hgrn_s, conv_s, gdn_s, k_s, v_s = _trunk(x_sample, mods[:, bp:bp + bs], state_hgrn, state_gdn_conv,
                                                  state_gdn, (cache_k, cache_v, page_table), W)
    return (y_p, y_s, hgrn_p, hgrn_s, conv_p, conv_s, gdn_p, gdn_s, k_p, k_s, v_p, v_s)
```

```python
import functools
import math

import jax
import jax.numpy as jnp
from jax import lax
from jax.experimental import pallas as pl
from jax.experimental.pallas import tpu as pltpu

F32 = jnp.float32
BF16 = jnp.bfloat16
HI = lax.Precision.HIGHEST

DEPTH = 4
N_MIXERS = 3
HEADS = 8
HEAD_DIM = 128
CONV_W = 4
CHUNK = 64
MOBA_BLOCK = 256
MOBA_TOPK = 3
REL_BUCKETS = 32
REL_MAX_DIST = 128
PAGE_SIZE = 128
DN_ALPHA = (2 * DEPTH) ** 0.25
LN_EPS = 1e-5
NEG = -1e30

LANES = 128
SUBLANES = 8
MXU_WIDTH = 256
VMEM_LIMIT_BYTES = 56 * 1024 * 1024


def _cparams(*sem):
    return pltpu.CompilerParams(dimension_semantics=sem, vmem_limit_bytes=VMEM_LIMIT_BYTES)


def _round_up(n, m):
    return (n + m - 1) // m * m


def _bdot(a, b):
    return jnp.dot(a.astype(BF16), b.astype(BF16), preferred_element_type=F32)


def _bdot_nt(a, b):
    return lax.dot_general(a.astype(BF16), b.astype(BF16), (((1,), (1,)), ((), ())),
                           preferred_element_type=F32)


def _bdot_tn(a, b):
    return lax.dot_general(a.astype(BF16), b.astype(BF16), (((0,), (0,)), ((), ())),
                           preferred_element_type=F32)


def _fdot(a, b):
    return jnp.dot(a, b, preferred_element_type=F32, precision=HI)


def _block_tril3(r, c):
    row = lax.broadcasted_iota(jnp.int32, (r, 3 * r), 0)
    col = lax.broadcasted_iota(jnp.int32, (r, 3 * r), 1) % r
    return jnp.where(((row // c) == (col // c)) & (row >= col), 1.0, 0.0).astype(BF16)


def _prefix_dot(sel3, x):
    hi = x.astype(BF16)
    rest = x - hi.astype(F32)
    mid = rest.astype(BF16)
    lo = (rest - mid.astype(F32)).astype(BF16)
    return jnp.dot(sel3, jnp.concatenate([hi, mid, lo], axis=0), preferred_element_type=F32)


def _sigmoid(x):
    return 1.0 / (1.0 + jnp.exp(-x))


def _silu(x):
    return x * _sigmoid(x)


def _layer_norm_rows(z, g, b):
    mu = jnp.mean(z, axis=-1, keepdims=True)
    zc = z - mu
    var = jnp.mean(zc * zc, axis=-1, keepdims=True)
    return zc * lax.rsqrt(var + LN_EPS) * g + b


def _ada_kernel(c_ref, w_ref, b_ref, o_ref):
    o_ref[...] = _bdot(_silu(c_ref[...]), w_ref[...]) + b_ref[...]


def _ada_mods(c_all, w_ada, b_ada):
    depth, d, n = w_ada.shape
    nb = c_all.shape[0]
    tn = n // 8
    assert n % 8 == 0 and tn % LANES == 0
    return pl.pallas_call(
        _ada_kernel,
        grid=(depth, n // tn),
        in_specs=[pl.BlockSpec((nb, d), lambda l, j: (0, 0)),
                  pl.BlockSpec((None, d, tn), lambda l, j: (l, 0, j)),
                  pl.BlockSpec((None, 1, tn), lambda l, j: (l, 0, j))],
        out_specs=pl.BlockSpec((None, nb, tn), lambda l, j: (l, 0, j)),
        out_shape=jax.ShapeDtypeStruct((depth, nb, n), F32),
        name="ada_mods",
        compiler_params=_cparams("parallel", "parallel"),
    )(c_all, w_ada, b_ada.reshape(depth, 1, n))


MAX_ROW_TILE = 1024


def _row_tile(batch, length):
    if length >= MXU_WIDTH:
        tm = MAX_ROW_TILE
        while length % tm:
            tm //= 2
        return tm, length // tm, True
    rows = batch * length
    assert rows % SUBLANES == 0 and rows <= MAX_ROW_TILE
    return rows, 1, False


def _mod_layout(m, batch, length):
    tm, _, per_batch = _row_tile(batch, length)
    mt = jnp.transpose(m, (1, 0, 2))
    if per_batch:
        return mt[:, :, None, :]
    return jnp.repeat(mt, length, axis=1)[:, None, :, :]


def _mod_spec(mod, tiles_per_block):
    _, _, rm, d = mod.shape
    return pl.BlockSpec((9, None, rm, d), lambda i, *_: (0, i // tiles_per_block, 0, 0))


FF_CHUNK = MXU_WIDTH


def _ffn_sublayer(x_ref, mod_ref, wa_ref, wu_ref, wd_ref, g_ref, b_ref, o_ref, h_ref, j, n_chunks):
    h_ref[...] = (x_ref[...] * (1.0 + mod_ref[3 * j + 1]) + mod_ref[3 * j]).astype(BF16)

    for c in range(n_chunks):
        cols = slice(c * FF_CHUNK, (c + 1) * FF_CHUNK)
        h = h_ref[...]
        a = jnp.dot(h, wa_ref[:, cols], preferred_element_type=F32)
        u = jnp.dot(h, wu_ref[:, cols], preferred_element_type=F32)
        act = (_silu(a) * u).astype(BF16)
        part = jnp.dot(act, wd_ref[cols, :], preferred_element_type=F32)
        if c == 0:
            o_ref[...] = part
        else:
            o_ref[...] += part
    z = DN_ALPHA * x_ref[...] + (1.0 + mod_ref[3 * j + 2]) * (0.5 * o_ref[...])
    o_ref[...] = _layer_norm_rows(z, g_ref[...], b_ref[...])


def _ffn_kernel(x_ref, mod_ref, wa_ref, wu_ref, wd_ref, g_ref, b_ref, o_ref, h_ref, *, j, n_chunks):
    _ffn_sublayer(x_ref, mod_ref, wa_ref, wu_ref, wd_ref, g_ref, b_ref, o_ref, h_ref, j, n_chunks)


def _mixer_out_ffn_kernel(m_ref, x_ref, mod_ref, wo_ref, g1_ref, b1_ref, wa_ref, wu_ref, wd_ref, g2_ref, b2_ref,
                          o_ref, h_ref, x1_ref, *, n_chunks):
    y = jnp.dot(m_ref[...].astype(BF16), wo_ref[...], preferred_element_type=F32)
    z = DN_ALPHA * x_ref[...] + (1.0 + mod_ref[3 * 1 + 2]) * y
    x1_ref[...] = _layer_norm_rows(z, g1_ref[...], b1_ref[...])
    _ffn_sublayer(x1_ref, mod_ref, wa_ref, wu_ref, wd_ref, g2_ref, b2_ref, o_ref, h_ref, 2, n_chunks)


def _mixer_out_ffn(m2d, x2d, mod, tiling, w_o, weights, layer, ln_g, ln_b):
    tm, tpb, _ = tiling
    rows, d = x2d.shape
    k = m2d.shape[1]
    w_a, w_u, w_down = weights
    fp = w_a.shape[-1]
    row = lambda: pl.BlockSpec((1, d), lambda i: (0, 0))
    resident = lambda shape, idx: pl.BlockSpec(shape, idx, pipeline_mode=pl.Buffered(1))
    return pl.pallas_call(
        functools.partial(_mixer_out_ffn_kernel, n_chunks=fp // FF_CHUNK),
        grid=(rows // tm,),
        in_specs=[pl.BlockSpec((tm, k), lambda i: (i, 0)),
                  pl.BlockSpec((tm, d), lambda i: (i, 0)),
                  _mod_spec(mod, tpb),
                  resident((k, d), lambda i: (0, 0)),
                  row(), row(),
                  resident((None, d, fp), lambda i: (layer, 0, 0)),
                  resident((None, d, fp), lambda i: (layer, 0, 0)),
                  resident((None, fp, d), lambda i: (layer, 0, 0)),
                  row(), row()],
        out_specs=pl.BlockSpec((tm, d), lambda i: (i, 0)),
        out_shape=jax.ShapeDtypeStruct((rows, d), F32),
        scratch_shapes=[pltpu.VMEM((tm, d), BF16), pltpu.VMEM((tm, d), F32)],
        name="mixer_out_ffn",
        compiler_params=_cparams("parallel"),
    )(m2d, x2d, mod, w_o, ln_g[1].reshape(1, d), ln_b[1].reshape(1, d), w_a, w_u, w_down,
      ln_g[2].reshape(1, d), ln_b[2].reshape(1, d))


def _ffn_block(x2d, mod, tiling, j, weights, layer, ln_g, ln_b):
    tm, tpb, _ = tiling
    rows, d = x2d.shape
    w_a, w_u, w_down = weights
    fp = w_a.shape[-1]
    kern = functools.partial(_ffn_kernel, j=j, n_chunks=fp // FF_CHUNK)
    return pl.pallas_call(
        kern,
        grid=(rows // tm,),
        in_specs=[pl.BlockSpec((tm, d), lambda i: (i, 0)),
                  _mod_spec(mod, tpb),
                  pl.BlockSpec((None, d, fp), lambda i: (layer, 0, 0), pipeline_mode=pl.Buffered(1)),
                  pl.BlockSpec((None, d, fp), lambda i: (layer, 0, 0), pipeline_mode=pl.Buffered(1)),
                  pl.BlockSpec((None, fp, d), lambda i: (layer, 0, 0), pipeline_mode=pl.Buffered(1)),
                  pl.BlockSpec((1, d), lambda i: (0, 0)),
                  pl.BlockSpec((1, d), lambda i: (0, 0))],
        out_specs=pl.BlockSpec((tm, d), lambda i: (i, 0)),
        out_shape=jax.ShapeDtypeStruct((rows, d), F32),
        scratch_shapes=[pltpu.VMEM((tm, d), BF16)],
        name="ffn_block",
        compiler_params=_cparams("parallel"),
    )(x2d, mod, w_a, w_u, w_down, ln_g.reshape(1, d), ln_b.reshape(1, d))


def _prep_ffn_weights(w_up, w_down):
    d_ff = w_up.shape[2] // 2
    fp = _round_up(d_ff, FF_CHUNK)
    pad_cols = lambda w: jnp.pad(w.astype(BF16), ((0, 0), (0, 0), (0, fp - d_ff)))
    down = jnp.pad(w_down.astype(BF16), ((0, 0), (0, fp - d_ff), (0, 0)))
    return pad_cols(w_up[:, :, :d_ff]), pad_cols(w_up[:, :, d_ff:]), down


PROJ_ROW_TILE = 512


def _proj_in_kernel(x_ref, mod_ref, w_ref, *o_refs, j, tn):
    h = (x_ref[...] * (1.0 + mod_ref[3 * j + 1]) + mod_ref[3 * j]).astype(BF16)
    per_out = o_refs[0].shape[1] // tn
    for k in range(w_ref.shape[1] // tn):
        cols = slice((k % per_out) * tn, (k % per_out + 1) * tn)
        o_refs[k // per_out][:, cols] = jnp.dot(h, w_ref[:, k * tn:(k + 1) * tn], preferred_element_type=F32)


def _proj_in(x2d, mod, tiling, j, w, tn, n_out=1):
    tm, tpb, per_batch = tiling
    if per_batch and tm % PROJ_ROW_TILE == 0:
        tpb = tpb * (tm // PROJ_ROW_TILE)
        tm = PROJ_ROW_TILE
    rows, d = x2d.shape
    n = w.shape[1]
    assert n % (n_out * tn) == 0
    outs = pl.pallas_call(
        functools.partial(_proj_in_kernel, j=j, tn=tn),
        grid=(rows // tm,),
        in_specs=[pl.BlockSpec((tm, d), lambda i: (i, 0)),
                  _mod_spec(mod, tpb),
                  pl.BlockSpec((d, n), lambda i: (0, 0), pipeline_mode=pl.Buffered(1))],
        out_specs=[pl.BlockSpec((tm, n // n_out), lambda i: (i, 0))] * n_out,
        out_shape=[jax.ShapeDtypeStruct((rows, n // n_out), F32)] * n_out,
        name="proj_in",
        compiler_params=_cparams("parallel"),
    )(x2d, mod, w)
    return outs[0] if n_out == 1 else outs


def _tri(c, strict=False):
    r = lax.broadcasted_iota(jnp.int32, (c, c), 0)
    s = lax.broadcasted_iota(jnp.int32, (c, c), 1)
    return (r > s) if strict else (r >= s)


def _hgrn_diag_masks(c):
    row = lax.broadcasted_iota(jnp.int32, (c, c), 0)
    col = lax.broadcasted_iota(jnp.int32, (c, c), 1)
    return [(col == row - d) & ((row % SUBLANES) >= d) for d in range(SUBLANES)]


def _hgrn_chunk(q, k, cum, v, st_prev, c, diag_masks):
    hs = range(len(q))
    nb = c // SUBLANES
    last_row = [cum[h][c - 1:c, :] for h in hs]
    o_state = [_bdot_nt(q[h] * jnp.exp(cum[h]), st_prev[h]) for h in hs]
    q3 = [q[h].reshape(nb, SUBLANES, LANES) for h in hs]
    k3 = [k[h].reshape(nb, SUBLANES, LANES) for h in hs]
    cum3 = [cum[h].reshape(nb, SUBLANES, LANES) for h in hs]

    att_off = None
    if nb > 1:
        blk = lax.broadcasted_iota(jnp.int32, (nb, SUBLANES, LANES), 0)
        q_cat, k_cat = [], []
        for h in hs:
            e = cum3[h][:, SUBLANES - 1:SUBLANES, :]
            k_t = k3[h] * jnp.exp(e - cum3[h])
            q_parts, k_parts = [], []
            for jb in range(nb - 1):
                qj = q3[h][jb + 1:] * jnp.exp(cum3[h][jb + 1:] - e[jb:jb + 1])
                qj = jnp.concatenate([jnp.zeros((jb + 1, SUBLANES, LANES), F32), qj], axis=0)
                q_parts.append(qj.reshape(c, LANES).astype(BF16))
                k_parts.append(jnp.where(blk == jb, k_t, 0.0).reshape(c, LANES).astype(BF16))
            q_cat.append(jnp.concatenate(q_parts, axis=1))
            k_cat.append(jnp.concatenate(k_parts, axis=1))
        att_off = [_bdot_nt(q_cat[h], k_cat[h]) for h in hs]

    att = [jnp.zeros((c, c), F32) for _ in hs]
    for d in reversed(range(SUBLANES)):
        for h in hs:
            if d == 0:
                p = q3[h] * k3[h]
            else:
                p = q3[h] * pltpu.roll(k3[h], d, 1) * jnp.exp(cum3[h] - pltpu.roll(cum3[h], d, 1))
            a_d = jnp.sum(p.reshape(c, LANES), axis=-1, keepdims=True)
            att[h] = jnp.where(diag_masks[d], a_d, att[h])
    if att_off is not None:
        att = [att[h] + att_off[h] for h in hs]

    o = [o_state[h] + _bdot(att[h], v[h]) for h in hs]
    st_new = [st_prev[h] * jnp.exp(last_row[h]) + _bdot_tn(v[h], k[h] * jnp.exp(last_row[h] - cum[h])) for h in hs]
    return o, st_new


REC_L_TILE = 512
HGRN_HEADS_PER_STEP = 8
GDN_HEADS_PER_STEP = 8
GDN_STACK = 2


def _head_lanes(hh):
    return slice(hh * HEAD_DIM, (hh + 1) * HEAD_DIM)


def _rec_tiling(l):
    c = min(CHUNK, l)
    tl = min(REC_L_TILE, l)
    assert l % tl == 0 and tl % c == 0 and c % SUBLANES == 0
    return c, tl


def _hgrn_kernel(q_ref, f_ref, v_ref, g_ref, lb_ref, ng_ref, s0_ref, o_ref, s_ref, *, c, n_chunks, hb):
    @pl.when(pl.program_id(2) == 0)
    def _():
        for hh in range(hb):
            s_ref[hh] = s0_ref[hh].T

    ng = ng_ref[...]
    lb = lb_ref[...]
    tril3 = _block_tril3(c, c)
    diag_masks = _hgrn_diag_masks(c)

    def body(i, carry):
        rows = pl.ds(pl.multiple_of(i * c, c), c)
        fr = f_ref[rows, :]
        q_all = _silu(q_ref[rows, :]) * (HEAD_DIM ** -0.5)
        sg = _sigmoid(fr)
        k_all = (1.0 - lb) * (1.0 - sg)
        cum_all = _prefix_dot(tril3, jnp.log(lb + (1.0 - lb) * sg))
        lanes = [_head_lanes(hh) for hh in range(hb)]
        o, st_new = _hgrn_chunk([q_all[:, ln] for ln in lanes], [k_all[:, ln] for ln in lanes],
                                [cum_all[:, ln] for ln in lanes], [v_ref[rows, ln] for ln in lanes],
                                [s_ref[hh] for hh in range(hb)], c, diag_masks)
        for hh, ln in enumerate(lanes):
            s_ref[hh] = st_new[hh]
            ms = jnp.mean(o[hh] * o[hh], axis=-1, keepdims=True)
            o_ref[rows, ln] = o[hh] * lax.rsqrt(ms + 1e-6) * ng * _sigmoid(g_ref[rows, ln])
        return carry

    lax.fori_loop(0, n_chunks, body, 0)

    @pl.when(pl.program_id(2) == pl.num_programs(2) - 1)
    def _():
        for hh in range(hb):
            s_ref[hh] = s_ref[hh].T


def _hgrn_recurrence(proj, s0, layer, lb, norm_g):
    b, l, _ = proj.shape
    c, tl = _rec_tiling(l)
    hb = HGRN_HEADS_PER_STEP
    ng_ = HEADS // hb
    col = lambda off: pl.BlockSpec((None, tl, hb * HEAD_DIM), lambda bi, hg, t: (bi, t, off * ng_ + hg))
    state = pl.BlockSpec((None, hb, HEAD_DIM, HEAD_DIM), lambda bi, hg, t: (bi, hg, 0, 0))
    return pl.pallas_call(
        functools.partial(_hgrn_kernel, c=c, n_chunks=tl // c, hb=hb),
        grid=(b, ng_, l // tl),
        in_specs=[col(0), col(1), col(2), col(3),
                  pl.BlockSpec((1, hb * HEAD_DIM), lambda bi, hg, t: (0, hg)),
                  pl.BlockSpec((1, HEAD_DIM), lambda bi, hg, t: (0, 0)),
                  pl.BlockSpec((None, None, hb, HEAD_DIM, HEAD_DIM), lambda bi, hg, t: (layer, bi, hg, 0, 0))],
        out_specs=[col(0), state],
        out_shape=[jax.ShapeDtypeStruct((b, l, HEADS * HEAD_DIM), F32),
                   jax.ShapeDtypeStruct((b, HEADS, HEAD_DIM, HEAD_DIM), F32)],
        name="hgrn_recurrence",
        compiler_params=_cparams("parallel", "parallel", "arbitrary"),
    )(proj, proj, proj, proj, lb.reshape(1, -1), norm_g.reshape(1, -1), s0)


HIST = SUBLANES


def _gdn_conv(x_ref, r0, lanes, cw, c):
    x_ext = x_ref[pl.ds(r0, c + HIST), lanes]
    acc = x_ext * cw[CONV_W - 1:CONV_W, :]
    for w in range(CONV_W - 1):
        acc = acc + pltpu.roll(x_ext, CONV_W - 1 - w, 0) * cw[w:w + 1, :]
    return _silu(acc[HIST:, :])


def _l2norm_rows(x):
    return x * lax.rsqrt(jnp.sum(x * x, axis=-1, keepdims=True) + 1e-6)


def _gdn_masks(r, c):
    row = lax.broadcasted_iota(jnp.int32, (r, r), 0)
    col = lax.broadcasted_iota(jnp.int32, (r, r), 1)
    same = (row // c) == (col // c)
    low = same & (row >= col)
    return low, same & (row > col), jnp.where(row == col, 1.0, 0.0), _block_tril3(r, c)


def _gdn_wy(groups, c, masks):
    low, strict, eye, tril3 = masks
    gr = range(len(groups))
    qs, ks, vs, betas, gs = zip(*groups)
    r = qs[0].shape[0]
    n = r // c
    pre = [_prefix_dot(tril3, jnp.concatenate(
        [jnp.where(strict, jnp.broadcast_to(gs[g], (r, r)), 0.0), jnp.broadcast_to(gs[g], (r, LANES))], axis=1))
        for g in gr]
    cum = [pre[g][:, r:] for g in gr]
    decay = [jnp.where(low, jnp.exp(jnp.where(low, pre[g][:, :r], 0.0)), 0.0) for g in gr]
    ecum = [jnp.exp(cum[g]) for g in gr]
    kb = [ks[g] * betas[g] for g in gr]
    kk = [_bdot_nt(jnp.concatenate([kb[g], qs[g]], axis=0), ks[g]) for g in gr]
    p = [-jnp.where(strict, kk[g][:r] * decay[g], 0.0) for g in gr]
    att = [kk[g][r:] * decay[g] for g in gr]
    t_inv = [eye + p[g] for g in gr]
    x = p
    for _ in range(int(math.log2(c)) - 1):
        x = [_bdot(x[g], x[g]) for g in gr]
        t_inv = [t_inv[g] + _bdot(x[g], t_inv[g]) for g in gr]
    uw = [_bdot(t_inv[g], jnp.concatenate([vs[g] * betas[g], kb[g] * ecum[g]], axis=1)) for g in gr]
    qe = [qs[g] * ecum[g] for g in gr]
    last = [jnp.concatenate([jnp.broadcast_to(cum[g][(h + 1) * c - 1:(h + 1) * c, :], (c, LANES))
                             for h in range(n)], axis=0) for g in gr]
    kt = [ks[g] * jnp.exp(last[g] - cum[g]) for g in gr]
    return [(uw[g], att[g], qe[g], kt[g], jnp.exp(last[g])) for g in gr]


def _gdn_state_step(wy, s_prev, c):
    gr = range(len(wy))
    uw, att, qe, kt, ecl = zip(*wy)
    n = len(s_prev[0])
    head_rows = [slice(h * c, (h + 1) * c) for h in range(n)]
    ws = [[_bdot(jnp.concatenate([uw[g][rows, HEAD_DIM:], qe[g][rows]], axis=0), s_prev[g][h])
           for h, rows in enumerate(head_rows)] for g in gr]
    v_new = [jnp.concatenate([uw[g][rows, :HEAD_DIM] - ws[g][h][:c] for h, rows in enumerate(head_rows)], axis=0)
             for g in gr]
    o = [jnp.concatenate([ws[g][h][c:] for h in range(n)], axis=0) + _bdot(att[g], v_new[g]) for g in gr]
    s_new = [[s_prev[g][h] * ecl[g][h * c:h * c + 1, :] + _bdot_tn(kt[g][rows], v_new[g][rows])
              for h, rows in enumerate(head_rows)] for g in gr]
    return [(o[g], s_new[g]) for g in gr]


def _gdn_kernel(q_ref, k_ref, v_ref, z_ref, ba_ref, c0q_ref, c0k_ref, c0v_ref, cwq_ref, cwk_ref, cwv_ref,
                an_ref, dt_ref, ng_ref, s0_ref, o_ref, s_ref, qx_ref, kx_ref, vx_ref, *wy_refs,
                c, n_chunks, tl, hb):
    hg = pl.program_id(1)
    t = pl.program_id(2)
    staged = ((q_ref, c0q_ref, qx_ref), (k_ref, c0k_ref, kx_ref), (v_ref, c0v_ref, vx_ref))

    @pl.when(t == 0)
    def _():
        s_ref[...] = s0_ref[...]
        for _, c0, dst in staged:
            dst[0:HIST, :] = jnp.zeros((HIST, hb * HEAD_DIM), F32)
            dst[HIST - (CONV_W - 1):HIST, :] = c0[...]

    @pl.when(t > 0)
    def _():
        for _, _, dst in staged:
            dst[0:HIST, :] = dst[tl:tl + HIST, :]

    for src, _, dst in staged:
        dst[HIST:HIST + tl, :] = src[...]
    ng = ng_ref[...]
    lane = lax.broadcasted_iota(jnp.int32, (c, LANES), 1)
    masks = _gdn_masks(GDN_STACK * c, c)
    n_groups = hb // GDN_STACK
    stack = lambda parts: jnp.concatenate(parts, axis=0)

    def chunk_groups(i):
        r0 = pl.multiple_of(i * c, c)
        ba = ba_ref[pl.ds(r0, c), :]
        groups = []
        for h0 in range(0, hb, GDN_STACK):
            qs, ks, vs, betas, gs = [], [], [], [], []
            for hh in range(h0, h0 + GDN_STACK):
                ln = _head_lanes(hh)
                head = hg * hb + hh
                qs.append(_l2norm_rows(_gdn_conv(qx_ref, r0, ln, cwq_ref[:, ln], c)) * (HEAD_DIM ** -0.5))
                ks.append(_l2norm_rows(_gdn_conv(kx_ref, r0, ln, cwk_ref[:, ln], c)))
                vs.append(_gdn_conv(vx_ref, r0, ln, cwv_ref[:, ln], c))
                betas.append(_sigmoid(jnp.sum(jnp.where(lane == head, ba, 0.0), axis=-1, keepdims=True)))
                a_raw = jnp.sum(jnp.where(lane == HEADS + head, ba, 0.0), axis=-1, keepdims=True)
                xa = a_raw + dt_ref[:, ln][:, 0:1]
                softplus = jnp.maximum(xa, 0.0) + jnp.log(1.0 + jnp.exp(-jnp.abs(xa)))
                gs.append(an_ref[:, ln][:, 0:1] * softplus)
            groups.append((stack(qs), stack(ks), stack(vs), stack(betas), stack(gs)))
        return groups

    per_iter = 2 if n_chunks % 2 == 0 else 1

    def wy_body(ip, carry):
        chunks = [ip * per_iter + j for j in range(per_iter)]
        wy = _gdn_wy([g for i in chunks for g in chunk_groups(i)], c, masks)
        for j, i in enumerate(chunks):
            for gi in range(n_groups):
                for ref, val in zip(wy_refs, wy[j * n_groups + gi]):
                    ref[i, gi] = val
        return carry

    lax.fori_loop(0, n_chunks // per_iter, wy_body, 0)

    def state_body(i, carry):
        rows = pl.ds(pl.multiple_of(i * c, c), c)
        wy = [tuple(ref[i, gi] for ref in wy_refs) for gi in range(n_groups)]
        s_prev = [[s_ref[gi * GDN_STACK + j] for j in range(GDN_STACK)] for gi in range(n_groups)]
        for gi, (o, s_new) in enumerate(_gdn_state_step(wy, s_prev, c)):
            for j in range(GDN_STACK):
                hh = gi * GDN_STACK + j
                ln = _head_lanes(hh)
                s_ref[hh] = s_new[j]
                oh = o[j * c:(j + 1) * c]
                ms = jnp.mean(oh * oh, axis=-1, keepdims=True)
                o_ref[rows, ln] = oh * lax.rsqrt(ms + 1e-6) * ng * _silu(z_ref[rows, ln])
        return carry

    lax.fori_loop(0, n_chunks, state_body, 0)


def _gdn_recurrence(proj, conv0, s0, layer, conv_w, a_log, dt_bias, norm_g):
    b, l, _ = proj.shape
    c, tl = _rec_tiling(l)
    hb = GDN_HEADS_PER_STEP
    ng_ = HEADS // hb
    w = hb * HEAD_DIM
    col = lambda off: pl.BlockSpec((None, tl, w), lambda bi, hg, t: (bi, t, off * ng_ + hg))
    c0 = lambda off: pl.BlockSpec((None, None, CONV_W - 1, w), lambda bi, hg, t: (layer, bi, 0, off * ng_ + hg))
    cw = lambda off: pl.BlockSpec((CONV_W, w), lambda bi, hg, t: (0, off * ng_ + hg))
    per_head = pl.BlockSpec((1, w), lambda bi, hg, t: (0, hg))
    state = pl.BlockSpec((None, hb, HEAD_DIM, HEAD_DIM), lambda bi, hg, t: (bi, hg, 0, 0))
    a_neg = jnp.repeat(-jnp.exp(a_log.astype(F32)), HEAD_DIM)[None, :]
    dt_b = jnp.repeat(dt_bias.astype(F32), HEAD_DIM)[None, :]
    xbuf = pltpu.VMEM((HIST + tl, w), F32)
    return pl.pallas_call(
        functools.partial(_gdn_kernel, c=c, n_chunks=tl // c, tl=tl, hb=hb),
        grid=(b, ng_, l // tl),
        in_specs=[col(0), col(1), col(2), col(3),
                  pl.BlockSpec((None, tl, LANES), lambda bi, hg, t: (bi, t, 4 * HEADS)),
                  c0(0), c0(1), c0(2), cw(0), cw(1), cw(2),
                  per_head, per_head,
                  pl.BlockSpec((1, HEAD_DIM), lambda bi, hg, t: (0, 0)),
                  pl.BlockSpec((None, None, hb, HEAD_DIM, HEAD_DIM), lambda bi, hg, t: (layer, bi, hg, 0, 0))],
        out_specs=[col(0), state],
        out_shape=[jax.ShapeDtypeStruct((b, l, HEADS * HEAD_DIM), F32),
                   jax.ShapeDtypeStruct((b, HEADS, HEAD_DIM, HEAD_DIM), F32)],
        scratch_shapes=[xbuf, xbuf, xbuf] + [
            pltpu.VMEM((tl // c, hb // GDN_STACK, GDN_STACK * c, width), F32)
            for width in (2 * HEAD_DIM, GDN_STACK * c, HEAD_DIM, HEAD_DIM, HEAD_DIM)],
        name="gdn_recurrence",
        compiler_params=_cparams("parallel", "parallel", "arbitrary"),
    )(proj, proj, proj, proj, proj, conv0, conv0, conv0, conv_w, conv_w, conv_w,
      a_neg, dt_b, norm_g.reshape(1, -1), s0)


def _rel_bucket(dist):
    exact = REL_BUCKETS // 2
    d = jnp.maximum(dist, exact).astype(F32)
    large = exact + (jnp.log(d / exact) / math.log(REL_MAX_DIST / exact) * (REL_BUCKETS - exact)).astype(jnp.int32)
    return jnp.where(dist < exact, dist, jnp.minimum(large, REL_BUCKETS - 1))


def _top_mask(gate, n_valid, k, axis):
    idx = lax.broadcasted_iota(jnp.int32, gate.shape, axis)
    big = jnp.int32(2 ** 30)
    work = jnp.where(idx < n_valid, gate, NEG)
    sel = jnp.zeros(gate.shape, F32)
    for r in range(k):
        mx = jnp.max(work, axis=axis, keepdims=True)
        first = jnp.min(jnp.where(work == mx, idx, big), axis=axis, keepdims=True)
        pick = (idx == first) & (idx < n_valid)
        sel = jnp.where(pick, 1.0, sel)
        work = jnp.where(idx == first, -jnp.inf, work)
    return sel


MOBA_HEADS_PER_STEP = 4
MOBA_FAR_BLOCKS_PER_STEP = 3


def _moba_prompt_kernel(q_ref, k_ref, v_ref, bd_ref, bp_ref, bf_ref, o_ref, means_ref, sel_ref, *, nb, hb):
    i = pl.program_id(2)
    blk = MOBA_BLOCK

    @pl.when(i == 0)
    def _():
        means_ref[...] = jnp.zeros_like(means_ref)
        for hh in range(hb):
            for n in range(nb):
                means_ref[hh, n:n + 1, :] = jnp.mean(k_ref[n * blk:(n + 1) * blk, _head_lanes(hh)],
                                                     axis=0, keepdims=True)

    causal = ~_tri(blk, strict=True)
    r0 = pl.multiple_of(i * blk, blk)
    heads = range(hb)
    qs = [q_ref[:, _head_lanes(hh)] * (HEAD_DIM ** -0.5) for hh in heads]
    qbs = [q.astype(BF16) for q in qs]
    gates = [lax.dot_general(means_ref[hh], qs[hh], (((1,), (1,)), ((), ())),
                             preferred_element_type=F32, precision=HI) for hh in heads]
    qk = [_bdot_nt(k_ref[pl.ds(r0, blk), _head_lanes(hh)], qbs[hh]) for hh in heads]
    for hh in heads:
        sel_ref[hh] = _top_mask(gates[hh], i, MOBA_TOPK, 0)
    s0 = [jnp.where(causal, qk[hh] + bd_ref[hh], NEG) for hh in heads]
    m0 = [jnp.max(s, axis=0, keepdims=True) for s in s0]
    p0 = [jnp.exp(s0[hh] - m0[hh]) for hh in heads]
    init = [(m0[hh], jnp.sum(p0[hh], axis=0, keepdims=True),
             _bdot_tn(v_ref[pl.ds(r0, blk), _head_lanes(hh)], p0[hh])) for hh in heads]

    def past_blocks(n, count, carry, bias_of_head, limit):
        rn = pl.multiple_of(n * blk, blk)
        rows = pl.ds(rn, count * blk)
        heads = range(hb)
        qk = [_bdot_nt(k_ref[rows, _head_lanes(hh)], qbs[hh]) for hh in heads]
        sn = []
        for hh in heads:
            parts = []
            for j in range(count):
                picked = sel_ref[hh, pl.ds(n + j, 1), :] * (n + j < limit).astype(F32) > 0.0
                parts.append(jnp.where(picked, qk[hh][j * blk:(j + 1) * blk] + bias_of_head(hh), NEG))
            sn.append(parts[0] if count == 1 else jnp.concatenate(parts, axis=0))
        m_new = [jnp.maximum(carry[hh][0], jnp.max(sn[hh], axis=0, keepdims=True)) for hh in heads]
        pn = [jnp.exp(sn[hh] - m_new[hh]) for hh in heads]
        pv = [_bdot_tn(v_ref[rows, _head_lanes(hh)], pn[hh]) for hh in heads]
        out = []
        for hh in heads:
            m, l, acc = carry[hh]
            alpha = jnp.exp(m - m_new[hh])
            out.append((m_new[hh], l * alpha + jnp.sum(pn[hh], axis=0, keepdims=True), acc * alpha + pv[hh]))
        return tuple(out)

    n_far = jnp.maximum(i - 1, 0)
    per = MOBA_FAR_BLOCKS_PER_STEP
    carry = lax.fori_loop(0, (n_far + per - 1) // per,
                          lambda g, cr: past_blocks(per * g, per, cr, lambda hh: bf_ref[:, _head_lanes(hh)][:, 0:1],
                                                    n_far),
                          tuple(init))
    final = past_blocks(n_far, 1, carry, lambda hh: bp_ref[hh], i)
    for hh in range(hb):
        _, l, acc = final[hh]
        o_ref[:, _head_lanes(hh)] = (acc / l).T


def _bias_of_distance(rel_bias, dist):
    bucket = _rel_bucket(jnp.maximum(dist, 0))[None]
    out = jnp.zeros((HEADS,) + dist.shape, F32)
    for b in range(REL_BUCKETS):
        out = jnp.where(bucket == b, rel_bias[b].astype(F32).reshape((HEADS,) + (1,) * dist.ndim), out)
    return out


def _moba_bias_tables(rel_bias, blk):
    t = jnp.arange(blk, dtype=jnp.int32)
    d0 = t[None, :] - t[:, None]
    bias_diag = _bias_of_distance(rel_bias, d0)
    bias_prev = _bias_of_distance(rel_bias, d0 + blk)
    far = jnp.repeat(rel_bias[REL_BUCKETS - 1].astype(F32), HEAD_DIM)[None, :]
    return bias_diag, bias_prev, far


def _moba_prompt(q, k, v, rel_bias):
    b, l, _ = q.shape
    blk = MOBA_BLOCK
    assert l % blk == 0 and l // blk <= LANES
    assert 2 * blk > REL_MAX_DIST
    nb = l // blk
    bias_diag, bias_prev, far = _moba_bias_tables(rel_bias, blk)
    hb = MOBA_HEADS_PER_STEP
    ng_ = HEADS // hb
    w = hb * HEAD_DIM
    tile = pl.BlockSpec((hb, blk, blk), lambda bi, hg, i: (hg, 0, 0))
    return pl.pallas_call(
        functools.partial(_moba_prompt_kernel, nb=nb, hb=hb),
        grid=(b, ng_, nb),
        in_specs=[pl.BlockSpec((None, blk, w), lambda bi, hg, i: (bi, i, hg)),
                  pl.BlockSpec((None, l, w), lambda bi, hg, i: (bi, 0, hg)),
                  pl.BlockSpec((None, l, w), lambda bi, hg, i: (bi, 0, hg)),
                  tile, tile,
                  pl.BlockSpec((1, w), lambda bi, hg, i: (0, hg))],
        out_specs=pl.BlockSpec((None, blk, w), lambda bi, hg, i: (bi, i, hg)),
        out_shape=jax.ShapeDtypeStruct((b, l, HEADS * HEAD_DIM), F32),
        scratch_shapes=[pltpu.VMEM((hb, _round_up(nb, SUBLANES), HEAD_DIM), F32),
                        pltpu.VMEM((hb, _round_up(nb, SUBLANES), blk), F32)],
        name="moba_prompt",
        compiler_params=_cparams("parallel", "parallel", "arbitrary"),
    )(q, k, v, bias_diag, bias_prev, far)


def _block_diag_rows(x, lq):
    return jnp.concatenate([x[h * lq:(h + 1) * lq, h * HEAD_DIM:(h + 1) * HEAD_DIM] for h in range(HEADS)], axis=0)


MOBA_SAMPLE_BLOCKS_PER_STEP = 8


def _moba_sample_kernel(pt_ref, *refs, nb, lq, bps):
    n_pages = bps * (MOBA_BLOCK // PAGE_SIZE)
    k_refs, v_refs = refs[:n_pages], refs[n_pages:2 * n_pages]
    (q_ref, kn_ref, vn_ref, bl_ref, bc_ref, bf_ref, o_ref,
     qbd_ref, m_ref, l_ref, acc_ref, means_ref) = refs[2 * n_pages:]
    step = pl.program_id(1)
    d_model = HEADS * HEAD_DIM
    hq = HEADS * lq

    @pl.when(step == 0)
    def _():
        q = q_ref[...] * (HEAD_DIM ** -0.5)
        rows = lax.broadcasted_iota(jnp.int32, (d_model, hq), 0) // HEAD_DIM
        cols = lax.broadcasted_iota(jnp.int32, (d_model, hq), 1)
        spread = (lax.broadcasted_iota(jnp.int32, (lq, hq), 1) % lq
                  == lax.broadcasted_iota(jnp.int32, (lq, hq), 0)).astype(F32)
        qt = lax.dot_general(q, spread, (((0,), (0,)), ((), ())),
                             preferred_element_type=F32, precision=HI)
        qbd_ref[...] = jnp.where(rows == cols // lq, qt, 0.0)
        means_ref[...] = jnp.zeros_like(means_ref)
        m_ref[...] = jnp.zeros_like(m_ref)
        l_ref[...] = jnp.zeros_like(l_ref)

    qbd = qbd_ref[...]

    def rows_by_lanes(*pages):
        return jnp.concatenate(
            [jnp.concatenate([pg[pl.ds(h, PAGE_SIZE, stride=HEADS), :] for h in range(HEADS)], axis=1)
             for pg in pages], axis=0)

    js = range(bps)
    ns = [step * bps + j for j in js]
    kblk = [rows_by_lanes(k_refs[2 * j], k_refs[2 * j + 1]) for j in js]
    vblk = [rows_by_lanes(v_refs[2 * j], v_refs[2 * j + 1]) for j in js]
    s = [_bdot(kblk[j], qbd) + jnp.where(ns[j] == nb - 1, bl_ref[...], bf_ref[...]) for j in js]
    for j in js:
        means_ref[pl.ds(ns[j], 1), :] = jnp.mean(kblk[j], axis=0, keepdims=True)
    m = [jnp.max(s[j], axis=0, keepdims=True) for j in js]
    p = [jnp.exp(s[j] - m[j]) for j in js]
    pv = [_bdot_tn(p[j], vblk[j]) for j in js]
    for j in js:
        m_ref[pl.ds(ns[j], 1), :] = m[j]
        l_ref[pl.ds(ns[j], 1), :] = jnp.sum(p[j], axis=0, keepdims=True)
        acc_ref[ns[j]] = _block_diag_rows(pv[j], lq)

    @pl.when(step == pl.num_programs(1) - 1)
    def _():
        k_new = kn_ref[...]
        v_new = vn_ref[...]
        jq = lax.broadcasted_iota(jnp.int32, (lq, hq), 1) % lq
        jk = lax.broadcasted_iota(jnp.int32, (lq, hq), 0)
        sc = jnp.where(jk <= jq, _bdot(k_new, qbd) + bc_ref[...], NEG)
        mc = jnp.max(sc, axis=0, keepdims=True)
        pc = jnp.exp(sc - mc)
        lc = jnp.sum(pc, axis=0, keepdims=True)
        oc = _block_diag_rows(_bdot_tn(pc, v_new), lq)

        gate = _fdot(means_ref[...], qbd)
        sel = _top_mask(gate, nb, MOBA_TOPK, 0)
        m_all = m_ref[...]
        m_tot = jnp.maximum(jnp.max(jnp.where(sel > 0.0, m_all, NEG), axis=0, keepdims=True), mc)
        w = jnp.where(sel > 0.0, jnp.exp(jnp.where(sel > 0.0, m_all - m_tot, 0.0)), 0.0)
        wc = jnp.exp(mc - m_tot)
        denom = jnp.sum(w * l_ref[...], axis=0, keepdims=True) + wc * lc
        nbp = w.shape[0]
        w_all = jnp.concatenate([w, wc, jnp.zeros((SUBLANES - 1, hq), F32)], axis=0) / denom
        ne = nbp + SUBLANES
        eye = (lax.broadcasted_iota(jnp.int32, (ne, ne), 0)
               == lax.broadcasted_iota(jnp.int32, (ne, ne), 1)).astype(F32)
        wt = lax.dot_general(w_all, eye, (((0,), (0,)), ((), ())),
                             preferred_element_type=F32, precision=HI)
        out = wt[:, nbp:nbp + 1] * oc
        for b_i in range(nb):
            out = out + wt[:, b_i:b_i + 1] * acc_ref[b_i]
        o_ref[...] = out


def _moba_sample(q, k, v, cache_k, cache_v, layer, page_table, rel_bias):
    b, lq, _ = q.shape
    n_pages = page_table.shape[1]
    d_model = HEADS * HEAD_DIM
    pages_per_block = MOBA_BLOCK // PAGE_SIZE
    assert pages_per_block == 2 and n_pages % pages_per_block == 0 and lq <= MOBA_BLOCK
    assert cache_k.shape[2:] == (PAGE_SIZE, HEADS, HEAD_DIM)
    nb = n_pages // pages_per_block
    assert nb >= 1 and MOBA_BLOCK >= REL_MAX_DIST
    nbp = _round_up(nb, SUBLANES)
    hq = HEADS * lq
    jq = jnp.arange(lq, dtype=jnp.int32)
    t = jnp.arange(MOBA_BLOCK, dtype=jnp.int32)
    by_cols = lambda tab: jnp.transpose(tab, (1, 0, 2)).reshape(tab.shape[1], hq)
    bias_last = by_cols(_bias_of_distance(rel_bias, (MOBA_BLOCK + jq)[None, :] - t[:, None]))
    bias_cur = by_cols(_bias_of_distance(rel_bias, jq[None, :] - jq[:, None]))
    bias_far = jnp.repeat(rel_bias[REL_BUCKETS - 1].astype(F32), lq)[None, :]

    n_c, n_pool = cache_k.shape[:2]
    cache_k = cache_k.reshape(n_c, n_pool, PAGE_SIZE * HEADS, HEAD_DIM)
    cache_v = cache_v.reshape(n_c, n_pool, PAGE_SIZE * HEADS, HEAD_DIM)
    bps = math.gcd(MOBA_SAMPLE_BLOCKS_PER_STEP, nb)
    pages_per_step = bps * pages_per_block
    page = lambda off: pl.BlockSpec((None, None, PAGE_SIZE * HEADS, HEAD_DIM),
                                    lambda bi, n, pt: (layer, pt[bi, pages_per_step * n + off], 0, 0))
    pages = [page(off) for off in range(pages_per_step)]
    full = lambda shape: pl.BlockSpec(shape, lambda bi, n, pt: (0,) * len(shape))
    grid_spec = pltpu.PrefetchScalarGridSpec(
        num_scalar_prefetch=1,
        grid=(b, nb // bps),
        in_specs=pages + pages + [
                  *([pl.BlockSpec((None, lq, d_model), lambda bi, n, pt: (bi, 0, 0))] * 3),
                  full((MOBA_BLOCK, hq)), full((lq, hq)), full((1, hq))],
        out_specs=pl.BlockSpec((None, hq, HEAD_DIM), lambda bi, n, pt: (bi, 0, 0)),
        scratch_shapes=[pltpu.VMEM((d_model, hq), F32),
                        pltpu.VMEM((nbp, hq), F32),
                        pltpu.VMEM((nbp, hq), F32),
                        pltpu.VMEM((nb, hq, HEAD_DIM), F32),
                        pltpu.VMEM((nbp, d_model), F32)])
    return pl.pallas_call(
        functools.partial(_moba_sample_kernel, nb=nb, lq=lq, bps=bps),
        grid_spec=grid_spec,
        out_shape=jax.ShapeDtypeStruct((b, hq, HEAD_DIM), F32),
        name="moba_sample",
        compiler_params=_cparams("parallel", "arbitrary"),
    )(page_table, *([cache_k] * pages_per_step), *([cache_v] * pages_per_step), q, k, v,
      bias_last, bias_cur, bias_far)


def _trunk(x, mods, s_hgrn, s_conv, s_gdn, paged, W):
    b, l, d = x.shape
    tiling = _row_tile(b, l)
    x2 = x.reshape(b * l, d)
    new_h, new_c, new_g, new_k, new_v = [], [], [], [], []
    for i in range(DEPTH):
        mod = _mod_layout(mods[i], b, l)
        x2 = _ffn_block(x2, mod, tiling, 0, W['ffn1'], i, W['ln_g'][i, 0], W['ln_b'][i, 0])
        kind, li = i % N_MIXERS, i // N_MIXERS
        if kind == 0:
            proj = _proj_in(x2, mod, tiling, 1, W['hgrn_w_in'][li], 1024)
            o, s = _hgrn_recurrence(proj.reshape(b, l, -1), s_hgrn, li, W['hgrn_lb'][li], W['hgrn_norm_g'][li])
            new_h.append(s)
            w_o = W['hgrn_w_o'][li]
            o2 = o.reshape(b * l, d)
        elif kind == 1:
            proj = _proj_in(x2, mod, tiling, 1, W['gdn_w_in'][li], W['gdn_w_in'][li].shape[1] // 3)
            proj = proj.reshape(b, l, -1)
            o, s = _gdn_recurrence(proj, s_conv, s_gdn, li, W['gdn_conv_w'][li], W['gdn_a_log'][li],
                                   W['gdn_dt_bias'][li], W['gdn_norm_g'][li])
            n_qkv = 3 * HEADS * HEAD_DIM
            assert l >= CONV_W - 1
            new_c.append(proj[:, l - (CONV_W - 1):, :n_qkv])
            new_g.append(s)
            w_o = W['gdn_w_o'][li]
            o2 = o.reshape(b * l, d)
        else:
            q, k, v = (t.reshape(b, l, d) for t in _proj_in(x2, mod, tiling, 1, W['moba_w_qkv'][li], 1024, n_out=3))
            if paged is None:
                o2 = _moba_prompt(q, k, v, W['rel_bias']).reshape(b * l, d)
            else:
                ck, cv, pt = paged
                o = _moba_sample(q, k, v, ck, cv, li, pt, W['rel_bias'])
                o2 = jnp.transpose(o.reshape(b, HEADS, l, HEAD_DIM), (0, 2, 1, 3)).reshape(b * l, d)
            new_k.append(k.reshape(b, l, HEADS, HEAD_DIM))
            new_v.append(v.reshape(b, l, HEADS, HEAD_DIM))
            w_o = W['moba_w_o'][li]
        x2 = _mixer_out_ffn(o2, x2, mod, tiling, w_o, W['ffn2'], i, W['ln_g'][i], W['ln_b'][i])
    return (x2.reshape(b, l, d), jnp.stack(new_h), jnp.stack(new_c), jnp.stack(new_g),
            jnp.stack(new_k), jnp.stack(new_v))


def kernel(x_prompt, x_sample, state_hgrn, state_gdn_conv, state_gdn, cache_k, cache_v, page_table, c_prompt, c_sample, w_ada, b_ada, ln_g, ln_b, w_ffn1_up, w_ffn1_down, w_ffn2_up, w_ffn2_down, hgrn_w_in, hgrn_lb_logits, hgrn_norm_g, hgrn_w_o, gdn_w_in, gdn_conv_w, gdn_a_log, gdn_dt_bias, gdn_norm_g, gdn_w_o, moba_w_qkv, moba_w_o, rel_bias):
    bp, _, d = x_prompt.shape
    bs = x_sample.shape[0]
    assert d == HEADS * HEAD_DIM

    nc = _round_up(bp + bs, SUBLANES)
    c_all = jnp.pad(jnp.concatenate([c_prompt, c_sample], axis=0), ((0, nc - bp - bs), (0, 0)))
    mods = _ada_mods(c_all, w_ada, b_ada).reshape(DEPTH, nc, 9, d)

    gdn_n = _round_up(gdn_w_in.shape[2], 3 * LANES)
    lbp = jax.nn.softmax(hgrn_lb_logits.astype(F32), axis=0)
    W = {
        'ffn1': _prep_ffn_weights(w_ffn1_up, w_ffn1_down), 'ffn2': _prep_ffn_weights(w_ffn2_up, w_ffn2_down),
        'ln_g': ln_g, 'ln_b': ln_b,
        'hgrn_w_in': hgrn_w_in.astype(BF16), 'hgrn_lb': jnp.cumsum(lbp, axis=0) - lbp[0],
        'hgrn_norm_g': hgrn_norm_g, 'hgrn_w_o': hgrn_w_o.astype(BF16),
        'gdn_w_in': jnp.pad(gdn_w_in, ((0, 0), (0, 0), (0, gdn_n - gdn_w_in.shape[2]))).astype(BF16),
        'gdn_conv_w': gdn_conv_w, 'gdn_a_log': gdn_a_log, 'gdn_dt_bias': gdn_dt_bias,
        'gdn_norm_g': gdn_norm_g, 'gdn_w_o': gdn_w_o.astype(BF16),
        'moba_w_qkv': moba_w_qkv.astype(BF16), 'moba_w_o': moba_w_o.astype(BF16), 'rel_bias': rel_bias,
    }

    n_a, n_b = state_hgrn.shape[0], state_gdn.shape[0]
    z_hgrn = jnp.zeros((n_a, bp) + state_hgrn.shape[2:], F32)
    z_conv = jnp.zeros((n_b, bp) + state_gdn_conv.shape[2:], F32)
    z_gdn = jnp.zeros((n_b, bp) + state_gdn.shape[2:], F32)
    y_p, hgrn_p, conv_p, gdn_p, k_p, v_p = _trunk(x_prompt, mods[:, :bp], z_hgrn, z_conv, z_gdn, None, W)

    y_s, hgrn_s, conv_s, gdn_s, k_s, v_s = _trunk(x_sample, mods[:, bp:bp + bs], state_hgrn, state_gdn_conv,
                                                  state_gdn, (cache_k, cache_v, page_table), W)
    return (y_p, y_s, hgrn_p, hgrn_s, conv_p, conv_s, gdn_p, gdn_s, k_p, k_s, v_p, v_s)
```
